```python
import math
import jax, jax.numpy as jnp
from jax import lax
import numpy as np

D_MODEL = 2048
BATCH = 4
SEQ = 8192
DEPTH = 2
DEC_BATCH = 8
DEC_SEQ = 4096
PAST_LEN = 128

MIX_WIDTH = D_MODEL
POOL_WIDTH = MIX_WIDTH // 2
POOL_WINDOWS = (2, 4, 8, 16)
POOL_GROUPS = len(POOL_WINDOWS)
POOL_GROUP_WIDTH = POOL_WIDTH // POOL_GROUPS
ATTN_WIDTH = MIX_WIDTH - POOL_WIDTH
N_DIFF_HEADS = 8
DIFF_VDIM = ATTN_WIDTH // N_DIFF_HEADS
DIFF_QKDIM = DIFF_VDIM // 2
QK_WIDTH = N_DIFF_HEADS * 2 * DIFF_QKDIM
IN_WIDTH = 2 * QK_WIDTH + ATTN_WIDTH + POOL_WIDTH
N_MEM = 256
N_CROSS_HEADS = 4
CROSS_HDIM = D_MODEL // N_CROSS_HEADS
D_FF = 5632
CONV_WIDTH = 3
ROPE_THETA = 10000.0
EPS = 1e-6
Q_BLOCK = 128

kernel_name = 'hybrid_diffattn_pool_memory_encoder'


def rms_norm(x, g):
    xf = x.astype(jnp.float32)
    y = xf * lax.rsqrt(jnp.mean(xf * xf, axis=-1, keepdims=True) + EPS)
    return (y * g.astype(jnp.float32)).astype(x.dtype)


def rope(x):
    S, dh = x.shape[1], x.shape[-1]
    half = dh // 2
    inv = ROPE_THETA ** (-jnp.arange(half, dtype=jnp.float32) / half)
    ang = jnp.arange(S, dtype=jnp.float32)[:, None] * inv[None, :]
    shp = (S,) + (1,) * (x.ndim - 3) + (half,)
    cos = jnp.cos(ang).reshape(shp)
    sin = jnp.sin(ang).reshape(shp)
    xf = x.astype(jnp.float32)
    x1, x2 = xf[..., :half], xf[..., half:]
    out = jnp.concatenate([x1 * cos - x2 * sin, x2 * cos + x1 * sin], axis=-1)
    return out.astype(x.dtype)


def multiscale_pool(u, w_pool, pool_scale):
    B, S, _ = u.shape
    ug = u.reshape(B, S, POOL_GROUPS, POOL_GROUP_WIDTH)
    uf = ug.astype(jnp.float32)
    c = jnp.pad(jnp.cumsum(uf, axis=1), ((0, 0), (1, 0), (0, 0), (0, 0)))
    t = jnp.arange(S)
    outs = []
    for g, w in enumerate(POOL_WINDOWS):
        lo = jnp.clip(t - w // 2, 0, S)
        hi = jnp.clip(t + w // 2, 0, S)
        cg = c[:, :, g]
        win_sum = jnp.take(cg, hi, axis=1) - jnp.take(cg, lo, axis=1)
        mean = win_sum / (hi - lo).astype(jnp.float32)[None, :, None]
        outs.append(mean - uf[:, :, g])
    z = jnp.stack(outs, axis=2).astype(u.dtype)
    y = jnp.einsum('bsgc,gcd->bsgd', z, w_pool).reshape(B, S, POOL_WIDTH)
    return y * pool_scale


def diff_attention(q, k, v, lam, lam_init, g_sub):
    B, S = q.shape[0], q.shape[1]
    nb = S // Q_BLOCK
    scale = DIFF_QKDIM ** -0.5
    qb = q.reshape(B, nb, Q_BLOCK, N_DIFF_HEADS, 2, DIFF_QKDIM).transpose(1, 0, 2, 3, 4, 5)

    def block(qblk):
        s = jnp.einsum('bqhcd,bkhcd->bhcqk', qblk, k).astype(jnp.float32) * scale
        p = jax.nn.softmax(s, axis=-1)
        a = p[:, :, 0] - lam * p[:, :, 1]
        return jnp.einsum('bhqk,bkhd->bqhd', a.astype(v.dtype), v)

    o = lax.map(block, qb)
    o = o.transpose(1, 0, 2, 3, 4).reshape(B, S, N_DIFF_HEADS, DIFF_VDIM)
    o = rms_norm(o, g_sub) * (1.0 - lam_init)
    return o.reshape(B, S, ATTN_WIDTH)


def memory_cross_attention(h, mem, g_mem, wc_q, wc_kv, gc_q, gc_k, wc_o):
    B, S, _ = h.shape
    m = rms_norm(mem, g_mem)
    q = (h @ wc_q).reshape(B, S, N_CROSS_HEADS, CROSS_HDIM)
    kv = (m @ wc_kv).reshape(B, N_MEM, 2, N_CROSS_HEADS, CROSS_HDIM)
    q = rms_norm(q, gc_q)
    k = rms_norm(kv[:, :, 0], gc_k)
    v = kv[:, :, 1]
    s = jnp.einsum('bqhd,bkhd->bhqk', q, k).astype(jnp.float32) * (CROSS_HDIM ** -0.5)
    p = jax.nn.softmax(s, axis=-1)
    o = jnp.einsum('bhqk,bkhd->bqhd', p.astype(v.dtype), v).reshape(B, S, D_MODEL)
    return o @ wc_o


def conv_glu_ffn(h, w_up, conv_w, conv_b, w_down):
    u = h @ w_up
    up = jnp.pad(u, ((0, 0), (1, 1), (0, 0)))
    u = up[:, :-2] * conv_w[0] + up[:, 1:-1] * conv_w[1] + up[:, 2:] * conv_w[2] + conv_b
    gate, val = u[..., :D_FF], u[..., D_FF:]
    return (jax.nn.gelu(gate) * val) @ w_down


def run_trunk(x, mem, g_mix, w_in, g_q, g_k, lam_q1, lam_k1, lam_q2, lam_k2, g_sub, w_pool, pool_scale, w_out,
              g_cross, g_mem, wc_q, wc_kv, gc_q, gc_k, wc_o, g_ffn, w_up, conv_w, conv_b, w_down):
    B, S, _ = x.shape
    for l in range(DEPTH):
        lam_init = 0.8 - 0.6 * math.exp(-0.3 * l)
        lam = (jnp.exp(jnp.sum(lam_q1[l].astype(jnp.float32) * lam_k1[l].astype(jnp.float32)))
               - jnp.exp(jnp.sum(lam_q2[l].astype(jnp.float32) * lam_k2[l].astype(jnp.float32))) + lam_init)
        h = rms_norm(x, g_mix[l])
        z = h @ w_in[l]
        q = z[..., :QK_WIDTH].reshape(B, S, N_DIFF_HEADS, 2, DIFF_QKDIM)
        k = z[..., QK_WIDTH:2 * QK_WIDTH].reshape(B, S, N_DIFF_HEADS, 2, DIFF_QKDIM)
        v = z[..., 2 * QK_WIDTH:2 * QK_WIDTH + ATTN_WIDTH].reshape(B, S, N_DIFF_HEADS, DIFF_VDIM)
        u = z[..., 2 * QK_WIDTH + ATTN_WIDTH:]
        q = rope(rms_norm(q, g_q[l]))
        k = rope(rms_norm(k, g_k[l]))
        a_out = diff_attention(q, k, v, lam, lam_init, g_sub[l])
        p_out = multiscale_pool(u, w_pool[l], pool_scale[l])
        x = x + jnp.concatenate([a_out, p_out], axis=-1) @ w_out[l]
        x = x + memory_cross_attention(rms_norm(x, g_cross[l]), mem, g_mem[l], wc_q[l], wc_kv[l], gc_q[l], gc_k[l], wc_o[l])
        x = x + conv_glu_ffn(rms_norm(x, g_ffn[l]), w_up[l], conv_w[l], conv_b[l], w_down[l])
    return x


def setup_inputs(seed: int = 0) -> dict:
    key = jax.random.key(seed)
    ks = jax.random.split(key, 32)
    f32 = jnp.float32

    def nrm(k, shape, scale):
        return jax.random.normal(k, shape, f32) * scale

    def gain(k, shape):
        return 1.0 + 0.05 * jax.random.normal(k, shape, f32)

    L, D = DEPTH, D_MODEL
    return {
        'x_prompt': nrm(ks[0], (BATCH, SEQ, D), 1.0),
        'x_sample': nrm(ks[1], (DEC_BATCH, DEC_SEQ, D), 1.0),
        'mem_prompt': nrm(ks[2], (BATCH, N_MEM, D), 1.0),
        'mem_sample': nrm(ks[3], (DEC_BATCH, N_MEM, D), 1.0),
        'g_mix': gain(ks[4], (L, D)),
        'w_in': nrm(ks[5], (L, D, IN_WIDTH), D ** -0.5),
        'g_q': gain(ks[6], (L, DIFF_QKDIM)),
        'g_k': gain(ks[7], (L, DIFF_QKDIM)),
        'lam_q1': nrm(ks[8], (L, DIFF_QKDIM), 0.1),
        'lam_k1': nrm(ks[9], (L, DIFF_QKDIM), 0.1),
        'lam_q2': nrm(ks[10], (L, DIFF_QKDIM), 0.1),
        'lam_k2': nrm(ks[11], (L, DIFF_QKDIM), 0.1),
        'g_sub': gain(ks[12], (L, DIFF_VDIM)),
        'w_pool': nrm(ks[13], (L, POOL_GROUPS, POOL_GROUP_WIDTH, POOL_GROUP_WIDTH), POOL_GROUP_WIDTH ** -0.5),
        'pool_scale': gain(ks[14], (L, POOL_WIDTH)),
        'w_out': nrm(ks[15], (L, MIX_WIDTH, D), MIX_WIDTH ** -0.5),
        'g_cross': gain(ks[16], (L, D)),
        'g_mem': gain(ks[17], (L, D)),
        'wc_q': nrm(ks[18], (L, D, D), D ** -0.5),
        'wc_kv': nrm(ks[19], (L, D, 2 * D), D ** -0.5),
        'gc_q': gain(ks[20], (L, CROSS_HDIM)),
        'gc_k': gain(ks[21], (L, CROSS_HDIM)),
        'wc_o': nrm(ks[22], (L, D, D), D ** -0.5),
        'g_ffn': gain(ks[23], (L, D)),
        'w_up': nrm(ks[24], (L, D, 2 * D_FF), D ** -0.5),
        'conv_w': nrm(ks[25], (L, CONV_WIDTH, 2 * D_FF), CONV_WIDTH ** -0.5),
        'conv_b': nrm(ks[26], (L, 2 * D_FF), 0.02),
        'w_down': nrm(ks[27], (L, D_FF, D), D_FF ** -0.5),
    }


def reference(x_prompt, x_sample, mem_prompt, mem_sample, g_mix, w_in, g_q, g_k, lam_q1, lam_k1, lam_q2, lam_k2,
              g_sub, w_pool, pool_scale, w_out, g_cross, g_mem, wc_q, wc_kv, gc_q, gc_k, wc_o, g_ffn, w_up,
              conv_w, conv_b, w_down):
    y_prompt = run_trunk(x_prompt, mem_prompt, g_mix, w_in, g_q, g_k, lam_q1, lam_k1, lam_q2, lam_k2, g_sub,
                         w_pool, pool_scale, w_out, g_cross, g_mem, wc_q, wc_kv, gc_q, gc_k, wc_o, g_ffn, w_up,
                         conv_w, conv_b, w_down)
    y_sample = run_trunk(x_sample, mem_sample, g_mix, w_in, g_q, g_k, lam_q1, lam_k1, lam_q2, lam_k2, g_sub,
                         w_pool, pool_scale, w_out, g_cross, g_mem, wc_q, wc_kv, gc_q, gc_k, wc_o, g_ffn, w_up,
                         conv_w, conv_b, w_down)
    return (y_prompt, y_sample)
```

```python
import functools
import math

import jax
import jax.numpy as jnp
from jax import lax
from jax.experimental import pallas as pl
from jax.experimental.pallas import tpu as pltpu

F32 = jnp.float32
BF16 = jnp.bfloat16

N_DIFF_HEADS = 8
DIFF_QKDIM = 64
DIFF_VDIM = 128
HEAD_COLS = 2 * DIFF_QKDIM
POOL_WINDOWS = (2, 4, 8, 16)
POOL_GROUP_WIDTH = 256
POOL_HALO = 16
N_CROSS_HEADS = 4
CONV_HALO = 8
ROPE_THETA = 10000.0
EPS = 1e-6

V7X_VMEM_BYTES = 64 * 1024 * 1024
V7X_LANES = 128
VMEM_LIMIT_CAP = V7X_VMEM_BYTES - 6 * 1024 * 1024


def _vmem_limit(estimate_bytes):
    return int(min(VMEM_LIMIT_CAP, max(32 * 1024 * 1024, estimate_bytes)))


def _params(semantics, vmem_estimate):
    return pltpu.CompilerParams(dimension_semantics=semantics, vmem_limit_bytes=_vmem_limit(vmem_estimate))


def _resident(shape):
    return pl.BlockSpec(shape, lambda *_: (0,) * len(shape), pipeline_mode=pl.Buffered(1))


def _rms(x, gain):
    ms = jnp.mean(x * x, axis=-1, keepdims=True)
    return x * lax.rsqrt(ms + EPS) * gain


def _dot(a, b):
    return jnp.dot(a, b, preferred_element_type=F32)


def _dot_nt(a, b):
    return lax.dot_general(a, b, (((1,), (1,)), ((), ())), preferred_element_type=F32)


def _mix_in_kernel(x_ref, g_ref, w_ref, seg_ref, aq_ref, bq_ref, ak_ref, bk_ref,
                   q_ref, k_ref, v_ref, u_ref, *, qk_width, v_width):
    h = _rms(x_ref[0], g_ref[...]).astype(BF16)
    lane = lax.broadcasted_iota(jnp.int32, (1, V7X_LANES), 1)
    partner_is_above = (lane & (DIFF_QKDIM // 2)) == 0
    seg = seg_ref[...]
    chunk = seg.shape[0]
    for col0, a_ref, b_ref, o_ref in ((0, aq_ref, bq_ref, q_ref), (qk_width, ak_ref, bk_ref, k_ref)):
        a = a_ref[...]
        b = b_ref[...]
        for c0 in range(0, qk_width, chunk):
            z = _dot(h, w_ref[:, col0 + c0:col0 + c0 + chunk])
            ss = _dot((z * z).astype(BF16), seg)
            zn = z * lax.rsqrt(ss * (1.0 / DIFF_QKDIM) + EPS)
            for c in range(0, chunk, V7X_LANES):
                zc = zn[:, c:c + V7X_LANES]
                partner = jnp.where(partner_is_above,
                                    pltpu.roll(zc, V7X_LANES - DIFF_QKDIM // 2, 1),
                                    pltpu.roll(zc, DIFF_QKDIM // 2, 1))
                o_ref[0, :, c0 + c:c0 + c + V7X_LANES] = (zc * a + partner * b).astype(o_ref.dtype)
    v0 = 2 * qk_width
    v_ref[0] = _dot(h, w_ref[:, v0:v0 + v_width]).astype(v_ref.dtype)
    u_ref[0] = _dot(h, w_ref[:, v0 + v_width:])


def _mix_in(x, g, w_in, seg, aq, bq, ak, bk, *, tm):
    B, S, D = x.shape
    qk_width = N_DIFF_HEADS * HEAD_COLS
    v_width = N_DIFF_HEADS * DIFF_VDIM
    u_width = w_in.shape[1] - 2 * qk_width - v_width
    row = lambda b, i: (b, i, 0)
    tab = pl.BlockSpec((tm, V7X_LANES), lambda b, i: (i, 0))
    est = (w_in.size * 2 + 2 * tm * D * 4 + 2 * tm * (2 * qk_width + v_width) * 2 + 2 * tm * u_width * 4
           + tm * D * 2 + 8 * tm * 1024 * 4 + 8 * tm * V7X_LANES * 4 + (4 << 20))
    return pl.pallas_call(
        functools.partial(_mix_in_kernel, qk_width=qk_width, v_width=v_width),
        grid=(B, S // tm),
        in_specs=[pl.BlockSpec((1, tm, D), row),
                  pl.BlockSpec((1, D), lambda b, i: (0, 0)),
                  _resident(w_in.shape),
                  pl.BlockSpec(seg.shape, lambda b, i: (0, 0)),
                  tab, tab, tab, tab],
        out_specs=[pl.BlockSpec((1, tm, qk_width), row),
                   pl.BlockSpec((1, tm, qk_width), row),
                   pl.BlockSpec((1, tm, v_width), row),
                   pl.BlockSpec((1, tm, u_width), row)],
        out_shape=[jax.ShapeDtypeStruct((B, S, qk_width), BF16),
                   jax.ShapeDtypeStruct((B, S, qk_width), BF16),
                   jax.ShapeDtypeStruct((B, S, v_width), BF16),
                   jax.ShapeDtypeStruct((B, S, u_width), F32)],
        compiler_params=_params(("parallel", "parallel"), est),
        name="mix_in",
    )(x, g, w_in, seg, aq, bq, ak, bk)


def _diff_attn_kernel(lam_ref, q_ref, k_ref, v_ref, gsub_ref, o_ref, qbd_ref, m_ref, l_ref, acc_ref,
                      *, tq, out_scale):
    ki = pl.program_id(3)

    @pl.when(ki == 0)
    def _():
        q = q_ref[0]
        lane = lax.broadcasted_iota(jnp.int32, (1, HEAD_COLS), 1)
        zero = jnp.zeros_like(q)
        qbd_ref[0:tq, :] = jnp.where(lane < DIFF_QKDIM, q, zero)
        qbd_ref[tq:2 * tq, :] = jnp.where(lane >= DIFF_QKDIM, q, zero)
        m_ref[...] = jnp.full(m_ref.shape, -jnp.inf, F32)
        l_ref[...] = jnp.zeros(l_ref.shape, F32)
        acc_ref[...] = jnp.zeros(acc_ref.shape, F32)

    s = _dot_nt(qbd_ref[...], k_ref[0])
    m_prev = m_ref[...]
    m_new = jnp.maximum(m_prev, jnp.max(s, axis=-1, keepdims=True))
    alpha = jnp.exp(m_prev - m_new)
    p = jnp.exp(s - m_new)
    l_ref[...] = alpha * l_ref[...] + jnp.sum(p, axis=-1, keepdims=True)
    acc_ref[...] = alpha * acc_ref[...] + _dot(p.astype(BF16), v_ref[0])
    m_ref[...] = m_new

    @pl.when(ki == pl.num_programs(3) - 1)
    def _():
        o = acc_ref[...] / l_ref[...]
        o = o[0:tq] - lam_ref[0] * o[tq:2 * tq]
        o_ref[0] = (_rms(o, gsub_ref[...]) * out_scale).astype(o_ref.dtype)


def _diff_attn(lam, q, k, v, g_sub, lam_init, *, tq, tk):
    B, S, _ = q.shape
    est = (2 * (tq + 2 * tk) * HEAD_COLS * 2 + 2 * tq * DIFF_VDIM * 2 + 2 * tq * HEAD_COLS * 2
           + 3 * 2 * tq * V7X_LANES * 4 + 6 * 2 * tq * tk * 4 + (4 << 20))
    return pl.pallas_call(
        functools.partial(_diff_attn_kernel, tq=tq, out_scale=1.0 - lam_init),
        grid=(B, N_DIFF_HEADS, S // tq, S // tk),
        in_specs=[pl.BlockSpec(memory_space=pltpu.SMEM),
                  pl.BlockSpec((1, tq, HEAD_COLS), lambda b, h, i, j: (b, i, h)),
                  pl.BlockSpec((1, tk, HEAD_COLS), lambda b, h, i, j: (b, j, h)),
                  pl.BlockSpec((1, tk, DIFF_VDIM), lambda b, h, i, j: (b, j, h)),
                  pl.BlockSpec((1, DIFF_VDIM), lambda b, h, i, j: (0, 0))],
        out_specs=pl.BlockSpec((1, tq, DIFF_VDIM), lambda b, h, i, j: (b, i, h)),
        out_shape=jax.ShapeDtypeStruct((B, S, N_DIFF_HEADS * DIFF_VDIM), BF16),
        scratch_shapes=[pltpu.VMEM((2 * tq, HEAD_COLS), BF16),
                        pltpu.VMEM((2 * tq, 1), F32),
                        pltpu.VMEM((2 * tq, 1), F32),
                        pltpu.VMEM((2 * tq, DIFF_VDIM), F32)],
        compiler_params=_params(("parallel", "parallel", "parallel", "arbitrary"), est),
        name="diff_attn",
    )(lam, q, k, v, g_sub)


def _mix_out_kernel(a_ref, u_ref, up_ref, un_ref, x_ref, wp_ref, ps_ref, wo_ref, o_ref, *, tm, seq):
    i = pl.program_id(1)
    u_prev = jnp.where(i > 0, up_ref[0], 0.0)
    u_next = jnp.where(i < pl.num_programs(1) - 1, un_ref[0], 0.0)
    ue = jnp.concatenate([u_prev, u_ref[0], u_next], axis=0)
    pos = i * tm + lax.broadcasted_iota(jnp.int32, (tm, 1), 0)
    a_width = a_ref.shape[2]
    acc = x_ref[0] + _dot(a_ref[0], wo_ref[0:a_width, :])
    for g, w in enumerate(POOL_WINDOWS):
        c0 = g * POOL_GROUP_WIDTH
        ug = ue[:, c0:c0 + POOL_GROUP_WIDTH]
        win = ug[POOL_HALO - w // 2:POOL_HALO - w // 2 + tm]
        for j in range(-(w // 2) + 1, w // 2):
            win = win + ug[POOL_HALO + j:POOL_HALO + j + tm]
        cnt = jnp.minimum(pos + w // 2, seq) - jnp.maximum(pos - w // 2, 0)
        z = win / cnt.astype(F32) - ug[POOL_HALO:POOL_HALO + tm]
        pg = _dot(z.astype(BF16), wp_ref[g]) * ps_ref[:, c0:c0 + POOL_GROUP_WIDTH]
        acc = acc + _dot(pg.astype(BF16), wo_ref[a_width + c0:a_width + c0 + POOL_GROUP_WIDTH, :])
    o_ref[0] = acc


def _mix_out(a, u, x, w_pool, pool_scale, w_out, *, tm):
    B, S, D = x.shape
    a_width, u_width = a.shape[2], u.shape[2]
    halo_blocks = tm // POOL_HALO
    n_halo = S // POOL_HALO
    row = lambda b, i: (b, i, 0)
    est = (w_out.size * 2 + w_pool.size * 2 * 2 + 4 * tm * D * 4 + 2 * tm * a_width * 2 + 2 * tm * u_width * 4
           + 6 * tm * u_width * 4 + 2 * tm * D * 4 + (4 << 20))
    return pl.pallas_call(
        functools.partial(_mix_out_kernel, tm=tm, seq=S),
        grid=(B, S // tm),
        in_specs=[pl.BlockSpec((1, tm, a_width), row),
                  pl.BlockSpec((1, tm, u_width), row),
                  pl.BlockSpec((1, POOL_HALO, u_width),
                               lambda b, i: (b, jnp.maximum(i * halo_blocks - 1, 0), 0)),
                  pl.BlockSpec((1, POOL_HALO, u_width),
                               lambda b, i: (b, jnp.minimum((i + 1) * halo_blocks, n_halo - 1), 0)),
                  pl.BlockSpec((1, tm, D), row),
                  pl.BlockSpec(w_pool.shape, lambda b, i: (0, 0, 0)),
                  pl.BlockSpec((1, u_width), lambda b, i: (0, 0)),
                  _resident(w_out.shape)],
        out_specs=pl.BlockSpec((1, tm, D), row),
        out_shape=jax.ShapeDtypeStruct((B, S, D), F32),
        compiler_params=_params(("parallel", "parallel"), est),
        name="mix_out",
    )(a, u, u, u, x, w_pool, pool_scale, w_out)


def _mem_kv_kernel(mem_ref, g_ref, w_ref, gk_ref, o_ref, *, normalise):
    m = _rms(mem_ref[0], g_ref[...]).astype(BF16)
    kv = _dot(m, w_ref[...])
    if normalise:
        kv = _rms(kv, gk_ref[...])
    o_ref[0] = kv.astype(o_ref.dtype)


def _mem_kv(mem, g_mem, wc_kv, gc_k, *, normalise, col_block0):
    B, n_mem, D = mem.shape
    hd = D // N_CROSS_HEADS
    est = 2 * n_mem * D * 4 + 2 * D * hd * 2 + 4 * n_mem * hd * 4 + n_mem * D * 4 + (4 << 20)
    return pl.pallas_call(
        functools.partial(_mem_kv_kernel, normalise=normalise),
        grid=(N_CROSS_HEADS, B),
        in_specs=[pl.BlockSpec((1, n_mem, D), lambda h, b: (b, 0, 0)),
                  pl.BlockSpec((1, D), lambda h, b: (0, 0)),
                  pl.BlockSpec((D, hd), lambda h, b: (0, col_block0 + h)),
                  pl.BlockSpec((1, hd), lambda h, b: (0, 0))],
        out_specs=pl.BlockSpec((1, n_mem, hd), lambda h, b: (b, 0, h)),
        out_shape=jax.ShapeDtypeStruct((B, n_mem, D), BF16),
        compiler_params=_params(("parallel", "parallel"), est),
        name="mem_k" if normalise else "mem_v",
    )(mem, g_mem, wc_kv, gc_k)


def _cross_attn_kernel(x_ref, g_ref, wq_ref, gq_ref, k_ref, v_ref, wo_ref, o_ref):
    x = x_ref[0]
    h = _rms(x, g_ref[...]).astype(BF16)
    q = _dot(h, wq_ref[...])
    hd = gq_ref.shape[1]
    scale = hd ** -0.5
    acc = x
    for c0 in range(0, q.shape[1], hd):
        qn = _rms(q[:, c0:c0 + hd], gq_ref[...]).astype(BF16)
        s = _dot_nt(qn, k_ref[0, :, c0:c0 + hd]) * scale
        p = jnp.exp(s - jnp.max(s, axis=-1, keepdims=True))
        p = p / jnp.sum(p, axis=-1, keepdims=True)
        oh = _dot(p.astype(BF16), v_ref[0, :, c0:c0 + hd])
        acc = acc + _dot(oh.astype(BF16), wo_ref[c0:c0 + hd, :])
    o_ref[0] = acc


def _cross_attn(x, g_cross, wc_q, gc_q, k_mem, v_mem, wc_o, *, tm):
    B, S, D = x.shape
    n_mem = k_mem.shape[1]
    row = lambda b, i: (b, i, 0)
    est = (wc_q.size * 2 + wc_o.size * 2 + 4 * n_mem * D * 2 + 4 * tm * D * 4 + 5 * tm * D * 4 + (4 << 20))
    return pl.pallas_call(
        _cross_attn_kernel,
        grid=(B, S // tm),
        in_specs=[pl.BlockSpec((1, tm, D), row),
                  pl.BlockSpec((1, D), lambda b, i: (0, 0)),
                  _resident(wc_q.shape),
                  pl.BlockSpec(gc_q.shape, lambda b, i: (0, 0)),
                  pl.BlockSpec((1, n_mem, D), lambda b, i: (b, 0, 0)),
                  pl.BlockSpec((1, n_mem, D), lambda b, i: (b, 0, 0)),
                  _resident(wc_o.shape)],
        out_specs=pl.BlockSpec((1, tm, D), row),
        out_shape=jax.ShapeDtypeStruct((B, S, D), F32),
        compiler_params=_params(("parallel", "parallel"), est),
        name="cross_attn",
    )(x, g_cross, wc_q, gc_q, k_mem, v_mem, wc_o)


def _conv_ffn_kernel(x_ref, xp_ref, xn_ref, g_ref, wg_ref, wv_ref, cwg_ref, cwv_ref, cbg_ref, cbv_ref, wd_ref,
                     o_ref, h_ref, *, tm):
    i = pl.program_id(1)
    f = pl.program_id(2)
    rows = tm + 2 * CONV_HALO

    @pl.when(f == 0)
    def _():
        g = g_ref[...]
        h_prev = jnp.where(i > 0, _rms(xp_ref[0], g), 0.0)
        h_next = jnp.where(i < pl.num_programs(1) - 1, _rms(xn_ref[0], g), 0.0)
        h_ref[0:CONV_HALO, :] = h_prev.astype(BF16)
        h_ref[CONV_HALO:CONV_HALO + tm, :] = _rms(x_ref[0], g).astype(BF16)
        h_ref[CONV_HALO + tm:rows, :] = h_next.astype(BF16)

    h = h_ref[...]

    def conv(w_ref, cw_ref, cb_ref):
        u = _dot(h, w_ref[...])
        below = pltpu.roll(u, 1, 0)[CONV_HALO:CONV_HALO + tm]
        above = pltpu.roll(u, rows - 1, 0)[CONV_HALO:CONV_HALO + tm]
        mid = u[CONV_HALO:CONV_HALO + tm]
        return below * cw_ref[0:1, :] + mid * cw_ref[1:2, :] + above * cw_ref[2:3, :] + cb_ref[...]

    act = jax.nn.gelu(conv(wg_ref, cwg_ref, cbg_ref)) * conv(wv_ref, cwv_ref, cbv_ref)
    y = _dot(act.astype(BF16), wd_ref[...])

    @pl.when(f == 0)
    def _():
        o_ref[0] = x_ref[0] + y

    @pl.when(f > 0)
    def _():
        o_ref[0] += y


def _conv_ffn(x, g_ffn, w_up, conv_w, conv_b, w_down, *, tm, tf):
    B, S, D = x.shape
    d_ff = w_down.shape[0]
    nf = d_ff // tf
    halo_blocks = tm // CONV_HALO
    n_halo = S // CONV_HALO
    rows = tm + 2 * CONV_HALO
    est = (4 * tm * D * 4 + 2 * 3 * D * tf * 2 + rows * D * 2 + 8 * rows * tf * 4 + 2 * tm * D * 4 + (4 << 20))
    gate = lambda b, i, f: (0, f)
    val = lambda b, i, f: (0, nf + f)
    return pl.pallas_call(
        functools.partial(_conv_ffn_kernel, tm=tm),
        grid=(B, S // tm, nf),
        in_specs=[pl.BlockSpec((1, tm, D), lambda b, i, f: (b, i, 0)),
                  pl.BlockSpec((1, CONV_HALO, D), lambda b, i, f: (b, jnp.maximum(i * halo_blocks - 1, 0), 0)),
                  pl.BlockSpec((1, CONV_HALO, D),
                               lambda b, i, f: (b, jnp.minimum((i + 1) * halo_blocks, n_halo - 1), 0)),
                  pl.BlockSpec((1, D), lambda b, i, f: (0, 0)),
                  pl.BlockSpec((D, tf), gate),
                  pl.BlockSpec((D, tf), val),
                  pl.BlockSpec((conv_w.shape[0], tf), gate),
                  pl.BlockSpec((conv_w.shape[0], tf), val),
                  pl.BlockSpec((1, tf), gate),
                  pl.BlockSpec((1, tf), val),
                  pl.BlockSpec((tf, D), lambda b, i, f: (f, 0))],
        out_specs=pl.BlockSpec((1, tm, D), lambda b, i, f: (b, i, 0)),
        out_shape=jax.ShapeDtypeStruct((B, S, D), F32),
        scratch_shapes=[pltpu.VMEM((rows, D), BF16)],
        compiler_params=_params(("parallel", "parallel", "arbitrary"), est),
        name="conv_ffn",
    )(x, x, x, g_ffn, w_up, w_up, conv_w, conv_w, conv_b, conv_b, w_down)


def _rope_tables(seq, gain, scale):
    half = DIFF_QKDIM // 2
    inv = ROPE_THETA ** (-jnp.arange(half, dtype=F32) / half)
    ang = jnp.arange(seq, dtype=F32)[:, None] * inv[None, :]
    cos = jnp.cos(ang)
    sin = jnp.sin(ang)
    gain = gain.astype(F32)
    a = jnp.concatenate([gain[:half] * cos, gain[half:] * cos], axis=1) * scale
    b = jnp.concatenate([-gain[half:] * sin, gain[:half] * sin], axis=1) * scale
    reps = V7X_LANES // DIFF_QKDIM
    return jnp.tile(a, (1, reps)), jnp.tile(b, (1, reps))


def _segment_ones(width):
    seg = jnp.arange(width) // DIFF_QKDIM
    return (seg[:, None] == seg[None, :]).astype(BF16)


def _tile(n, target):
    t = min(n, target)
    assert n % t == 0, (n, t)
    return t


def _run_trunk(x, mem, layers):
    B, S, D = x.shape
    tm = _tile(S, 512)
    seg = _segment_ones(256)
    for l, p in enumerate(layers):
        lam_init = 0.8 - 0.6 * math.exp(-0.3 * l)
        aq, bq = _rope_tables(S, p["g_q"], DIFF_QKDIM ** -0.5)
        ak, bk = _rope_tables(S, p["g_k"], 1.0)
        q, k, v, u = _mix_in(x, p["g_mix"], p["w_in"], seg, aq, bq, ak, bk, tm=tm)
        a = _diff_attn(p["lam"], q, k, v, p["g_sub"], lam_init, tq=_tile(S, 256), tk=_tile(S, 512))
        x = _mix_out(a, u, x, p["w_pool"], p["pool_scale"], p["w_out"], tm=tm)
        k_mem = _mem_kv(mem, p["g_mem"], p["wc_kv"], p["gc_k"], normalise=True, col_block0=0)
        v_mem = _mem_kv(mem, p["g_mem"], p["wc_kv"], p["gc_k"], normalise=False, col_block0=N_CROSS_HEADS)
        x = _cross_attn(x, p["g_cross"], p["wc_q"], p["gc_q"], k_mem, v_mem, p["wc_o"], tm=tm)
        x = _conv_ffn(x, p["g_ffn"], p["w_up"], p["conv_w"], p["conv_b"], p["w_down"], tm=tm, tf=512)
    return x


def kernel(x_prompt, x_sample, mem_prompt, mem_sample, g_mix, w_in, g_q, g_k, lam_q1, lam_k1, lam_q2, lam_k2,
           g_sub, w_pool, pool_scale, w_out, g_cross, g_mem, wc_q, wc_kv, gc_q, gc_k, wc_o, g_ffn, w_up,
           conv_w, conv_b, w_down):
    depth = w_in.shape[0]
    layers = []
    for l in range(depth):
        lam_init = 0.8 - 0.6 * math.exp(-0.3 * l)
        lam = (jnp.exp(jnp.sum(lam_q1[l].astype(F32) * lam_k1[l].astype(F32)))
               - jnp.exp(jnp.sum(lam_q2[l].astype(F32) * lam_k2[l].astype(F32))) + lam_init)
        row = lambda t: t[l].reshape(1, -1).astype(F32)
        layers.append(dict(
            lam=lam.reshape(1).astype(F32),
            g_mix=row(g_mix), g_q=g_q[l], g_k=g_k[l], g_sub=row(g_sub), pool_scale=row(pool_scale),
            g_cross=row(g_cross), g_mem=row(g_mem), gc_q=row(gc_q), gc_k=row(gc_k), g_ffn=row(g_ffn),
            conv_w=conv_w[l].astype(F32), conv_b=row(conv_b),
            w_in=w_in[l].astype(BF16), w_pool=w_pool[l].astype(BF16), w_out=w_out[l].astype(BF16),
            wc_q=wc_q[l].astype(BF16), wc_kv=wc_kv[l].astype(BF16), wc_o=wc_o[l].astype(BF16),
            w_up=w_up[l].astype(BF16), w_down=w_down[l].astype(BF16)))
    return (_run_trunk(x_prompt, mem_prompt, layers), _run_trunk(x_sample, mem_sample, layers))
```

```python
import functools
import math

import jax
import jax.numpy as jnp
from jax import lax
from jax.experimental import pallas as pl
from jax.experimental.pallas import tpu as pltpu

F32 = jnp.float32
BF16 = jnp.bfloat16

N_DIFF_HEADS = 8
DIFF_QKDIM = 64
DIFF_VDIM = 128
HEAD_COLS = 2 * DIFF_QKDIM
POOL_WINDOWS = (2, 4, 8, 16)
POOL_GROUP_WIDTH = 256
POOL_HALO = 16
N_CROSS_HEADS = 4
CONV_HALO = 8
ROPE_THETA = 10000.0
EPS = 1e-6

V7X_VMEM_BYTES = 64 * 1024 * 1024
V7X_LANES = 128
VMEM_LIMIT_CAP = V7X_VMEM_BYTES - 6 * 1024 * 1024


def _vmem_limit(estimate_bytes):
    return int(min(VMEM_LIMIT_CAP, max(32 * 1024 * 1024, estimate_bytes)))


def _params(semantics, vmem_estimate):
    return pltpu.CompilerParams(dimension_semantics=semantics, vmem_limit_bytes=_vmem_limit(vmem_estimate))


def _resident(shape):
    return pl.BlockSpec(shape, lambda *_: (0,) * len(shape), pipeline_mode=pl.Buffered(1))


def _rms(x, gain):
    ms = jnp.mean(x * x, axis=-1, keepdims=True)
    return x * lax.rsqrt(ms + EPS) * gain


def _dot(a, b):
    return jnp.dot(a, b, preferred_element_type=F32)


def _dot_nt(a, b):
    return lax.dot_general(a, b, (((1,), (1,)), ((), ())), preferred_element_type=F32)


def _mix_in_kernel(x_ref, g_ref, w_ref, seg_ref, aq_ref, bq_ref, ak_ref, bk_ref,
                   q_ref, k_ref, v_ref, u_ref, *, qk_width, v_width):
    h = _rms(x_ref[0], g_ref[...]).astype(BF16)
    lane = lax.broadcasted_iota(jnp.int32, (1, V7X_LANES), 1)
    partner_is_above = (lane & (DIFF_QKDIM // 2)) == 0
    seg = seg_ref[...]
    chunk = seg.shape[0]
    for col0, a_ref, b_ref, o_ref in ((0, aq_ref, bq_ref, q_ref), (qk_width, ak_ref, bk_ref, k_ref)):
        a = a_ref[...]
        b = b_ref[...]
        zfull = _dot(h, w_ref[:, col0:col0 + qk_width])
        for c0 in range(0, qk_width, chunk):
            z = zfull[:, c0:c0 + chunk]
            ss = _dot((z * z).astype(BF16), seg)
            zn = z * lax.rsqrt(ss * (1.0 / DIFF_QKDIM) + EPS)
            for c in range(0, chunk, V7X_LANES):
                zc = zn[:, c:c + V7X_LANES]
                partner = jnp.where(partner_is_above,
                                    pltpu.roll(zc, V7X_LANES - DIFF_QKDIM // 2, 1),
                                    pltpu.roll(zc, DIFF_QKDIM // 2, 1))
                o_ref[0, :, c0 + c:c0 + c + V7X_LANES] = (zc * a + partner * b).astype(o_ref.dtype)
    v0 = 2 * qk_width
    v_ref[0] = _dot(h, w_ref[:, v0:v0 + v_width]).astype(v_ref.dtype)
    u_ref[0] = _dot(h, w_ref[:, v0 + v_width:])


def _mix_in(x, g, w_in, seg, aq, bq, ak, bk, *, tm):
    B, S, D = x.shape
    qk_width = N_DIFF_HEADS * HEAD_COLS
    v_width = N_DIFF_HEADS * DIFF_VDIM
    u_width = w_in.shape[1] - 2 * qk_width - v_width
    row = lambda b, i: (b, i, 0)
    tab = pl.BlockSpec((tm, V7X_LANES), lambda b, i: (i, 0))
    est = (w_in.size * 2 + 2 * tm * D * 4 + 2 * tm * (2 * qk_width + v_width) * 2 + 2 * tm * u_width * 4
           + tm * D * 2 + 8 * tm * 1024 * 4 + 8 * tm * V7X_LANES * 4 + (4 << 20))
    return pl.pallas_call(
        functools.partial(_mix_in_kernel, qk_width=qk_width, v_width=v_width),
        grid=(B, S // tm),
        in_specs=[pl.BlockSpec((1, tm, D), row),
                  pl.BlockSpec((1, D), lambda b, i: (0, 0)),
                  _resident(w_in.shape),
                  pl.BlockSpec(seg.shape, lambda b, i: (0, 0)),
                  tab, tab, tab, tab],
        out_specs=[pl.BlockSpec((1, tm, qk_width), row),
                   pl.BlockSpec((1, tm, qk_width), row),
                   pl.BlockSpec((1, tm, v_width), row),
                   pl.BlockSpec((1, tm, u_width), row)],
        out_shape=[jax.ShapeDtypeStruct((B, S, qk_width), BF16),
                   jax.ShapeDtypeStruct((B, S, qk_width), BF16),
                   jax.ShapeDtypeStruct((B, S, v_width), BF16),
                   jax.ShapeDtypeStruct((B, S, u_width), F32)],
        compiler_params=_params(("parallel", "parallel"), est),
        name="mix_in",
    )(x, g, w_in, seg, aq, bq, ak, bk)


def _dot_tn(a, b):
    return lax.dot_general(a, b, (((0,), (0,)), ((), ())), preferred_element_type=F32)


def _diff_attn_kernel(lam_ref, q_ref, k_ref, v_ref, gsub_ref, o_ref, qbd_ref, s0_ref, s1_ref, m_ref, l_ref, acc_ref,
                      *, tq, tk, blocks_per_trip, out_scale):
    qt = q_ref[0].astype(F32).T
    row = lax.broadcasted_iota(jnp.int32, (HEAD_COLS, 1), 0)
    zero = jnp.zeros_like(qt)
    qbd_ref[:, 0:tq] = jnp.where(row < DIFF_QKDIM, qt, zero).astype(BF16)
    qbd_ref[:, tq:2 * tq] = jnp.where(row >= DIFF_QKDIM, qt, zero).astype(BF16)
    m_ref[...] = jnp.full(m_ref.shape, -jnp.inf, F32)
    l_ref[...] = jnp.zeros(l_ref.shape, F32)
    acc_ref[...] = jnp.zeros(acc_ref.shape, F32)

    def scores(j, s_ref):
        k0 = pl.multiple_of(j * tk, tk)
        s_ref[...] = _dot(k_ref[0, pl.ds(k0, tk), :], qbd_ref[...])

    def absorb(j, s_ref):
        k0 = pl.multiple_of(j * tk, tk)
        vb = v_ref[0, pl.ds(k0, tk), :]
        s = s_ref[...]
        m_prev = m_ref[...]
        m_new = jnp.maximum(m_prev, jnp.max(s, axis=0, keepdims=True))
        alpha = jnp.exp2(m_prev - m_new)
        p = jnp.exp2(s - m_new)
        l_ref[...] = alpha * l_ref[...] + jnp.sum(p, axis=0, keepdims=True)
        m_ref[...] = m_new
        pb = p.astype(BF16)
        for c in range(2):
            cols = slice(c * tq, (c + 1) * tq)
            acc_ref[c] = alpha[:, cols] * acc_ref[c] + _dot_tn(vb, pb[:, cols])

    bufs = (s0_ref, s1_ref)
    n_trips = k_ref.shape[1] // (blocks_per_trip * tk)
    scores(0, s0_ref)

    def trip(i, carry):
        j = blocks_per_trip * i
        for u in range(blocks_per_trip):
            scores(j + u + 1, bufs[(u + 1) % 2])
            absorb(j + u, bufs[u % 2])
        return carry

    lax.fori_loop(0, n_trips - 1, trip, 0)
    j_last = blocks_per_trip * (n_trips - 1)
    for u in range(blocks_per_trip):
        if u + 1 < blocks_per_trip:
            scores(j_last + u + 1, bufs[(u + 1) % 2])
        absorb(j_last + u, bufs[u % 2])

    inv = 1.0 / l_ref[...]
    o = acc_ref[0] * inv[:, 0:tq] - lam_ref[0] * (acc_ref[1] * inv[:, tq:2 * tq])
    ms = jnp.mean(o * o, axis=0, keepdims=True)
    o = o * lax.rsqrt(ms + EPS) * gsub_ref[...] * out_scale
    o_ref[0] = o.T.astype(o_ref.dtype)


def _diff_attn(lam, q, k, v, g_sub_col, lam_init, *, tq, tk, blocks_per_trip):
    B, S, _ = q.shape
    assert blocks_per_trip % 2 == 0 and S % (blocks_per_trip * tk) == 0, (S, tk, blocks_per_trip)
    est = (2 * 2 * S * (HEAD_COLS + DIFF_VDIM) * 2 + 4 * tq * HEAD_COLS * 2 + 2 * tq * HEAD_COLS * 2
           + 2 * tq * DIFF_VDIM * 4 + 8 * tk * 2 * tq * 4 + (4 << 20))
    return pl.pallas_call(
        functools.partial(_diff_attn_kernel, tq=tq, tk=tk, blocks_per_trip=blocks_per_trip,
                          out_scale=1.0 - lam_init),
        grid=(B, N_DIFF_HEADS, S // tq),
        in_specs=[pl.BlockSpec(memory_space=pltpu.SMEM),
                  pl.BlockSpec((1, tq, HEAD_COLS), lambda b, h, i: (b, i, h)),
                  pl.BlockSpec((1, S, HEAD_COLS), lambda b, h, i: (b, 0, h)),
                  pl.BlockSpec((1, S, DIFF_VDIM), lambda b, h, i: (b, 0, h)),
                  pl.BlockSpec((DIFF_VDIM, 1), lambda b, h, i: (0, 0))],
        out_specs=pl.BlockSpec((1, tq, DIFF_VDIM), lambda b, h, i: (b, i, h)),
        out_shape=jax.ShapeDtypeStruct((B, S, N_DIFF_HEADS * DIFF_VDIM), BF16),
        scratch_shapes=[pltpu.VMEM((HEAD_COLS, 2 * tq), BF16),
                        pltpu.VMEM((tk, 2 * tq), F32),
                        pltpu.VMEM((tk, 2 * tq), F32),
                        pltpu.VMEM((1, 2 * tq), F32),
                        pltpu.VMEM((1, 2 * tq), F32),
                        pltpu.VMEM((2, DIFF_VDIM, tq), F32)],
        compiler_params=_params(("parallel", "parallel", "parallel"), est),
        name="diff_attn",
    )(lam, q, k, v, g_sub_col)


def _mix_out_kernel(a_ref, u_ref, up_ref, un_ref, x_ref, wp_ref, ps_ref, wo_ref, o_ref, *, tm, seq):
    i = pl.program_id(1)
    u_prev = jnp.where(i > 0, up_ref[0], 0.0)
    u_next = jnp.where(i < pl.num_programs(1) - 1, un_ref[0], 0.0)
    ue = jnp.concatenate([u_prev, u_ref[0], u_next], axis=0)
    pos = i * tm + lax.broadcasted_iota(jnp.int32, (tm, 1), 0)
    a_width = a_ref.shape[2]
    acc = x_ref[0] + _dot(a_ref[0], wo_ref[0:a_width, :])
    for g, w in enumerate(POOL_WINDOWS):
        c0 = g * POOL_GROUP_WIDTH
        ug = ue[:, c0:c0 + POOL_GROUP_WIDTH]
        win = ug[POOL_HALO - w // 2:POOL_HALO - w // 2 + tm]
        for j in range(-(w // 2) + 1, w // 2):
            win = win + ug[POOL_HALO + j:POOL_HALO + j + tm]
        cnt = jnp.minimum(pos + w // 2, seq) - jnp.maximum(pos - w // 2, 0)
        z = win / cnt.astype(F32) - ug[POOL_HALO:POOL_HALO + tm]
        pg = _dot(z.astype(BF16), wp_ref[g]) * ps_ref[:, c0:c0 + POOL_GROUP_WIDTH]
        acc = acc + _dot(pg.astype(BF16), wo_ref[a_width + c0:a_width + c0 + POOL_GROUP_WIDTH, :])
    o_ref[0] = acc


def _mix_out(a, u, x, w_pool, pool_scale, w_out, *, tm):
    B, S, D = x.shape
    a_width, u_width = a.shape[2], u.shape[2]
    halo_blocks = tm // POOL_HALO
    n_halo = S // POOL_HALO
    row = lambda b, i: (b, i, 0)
    est = (w_out.size * 2 + w_pool.size * 2 * 2 + 4 * tm * D * 4 + 2 * tm * a_width * 2 + 2 * tm * u_width * 4
           + 6 * tm * u_width * 4 + 2 * tm * D * 4 + (4 << 20))
    return pl.pallas_call(
        functools.partial(_mix_out_kernel, tm=tm, seq=S),
        grid=(B, S // tm),
        in_specs=[pl.BlockSpec((1, tm, a_width), row),
                  pl.BlockSpec((1, tm, u_width), row),
                  pl.BlockSpec((1, POOL_HALO, u_width),
                               lambda b, i: (b, jnp.maximum(i * halo_blocks - 1, 0), 0)),
                  pl.BlockSpec((1, POOL_HALO, u_width),
                               lambda b, i: (b, jnp.minimum((i + 1) * halo_blocks, n_halo - 1), 0)),
                  pl.BlockSpec((1, tm, D), row),
                  pl.BlockSpec(w_pool.shape, lambda b, i: (0, 0, 0)),
                  pl.BlockSpec((1, u_width), lambda b, i: (0, 0)),
                  _resident(w_out.shape)],
        out_specs=pl.BlockSpec((1, tm, D), row),
        out_shape=jax.ShapeDtypeStruct((B, S, D), F32),
        compiler_params=_params(("parallel", "parallel"), est),
        name="mix_out",
    )(a, u, u, u, x, w_pool, pool_scale, w_out)


def _mem_kv_kernel(mem_ref, g_ref, w_ref, gk_ref, o_ref, *, normalise):
    m = _rms(mem_ref[0], g_ref[...]).astype(BF16)
    kv = _dot(m, w_ref[...])
    if normalise:
        kv = _rms(kv, gk_ref[...])
    o_ref[0] = kv.astype(o_ref.dtype)


def _mem_kv(mem, g_mem, wc_kv, gc_k, *, normalise, col_block0):
    B, n_mem, D = mem.shape
    hd = D // N_CROSS_HEADS
    est = 2 * n_mem * D * 4 + 2 * D * hd * 2 + 4 * n_mem * hd * 4 + n_mem * D * 4 + (4 << 20)
    return pl.pallas_call(
        functools.partial(_mem_kv_kernel, normalise=normalise),
        grid=(N_CROSS_HEADS, B),
        in_specs=[pl.BlockSpec((1, n_mem, D), lambda h, b: (b, 0, 0)),
                  pl.BlockSpec((1, D), lambda h, b: (0, 0)),
                  pl.BlockSpec((D, hd), lambda h, b: (0, col_block0 + h)),
                  pl.BlockSpec((1, hd), lambda h, b: (0, 0))],
        out_specs=pl.BlockSpec((1, n_mem, hd), lambda h, b: (b, 0, h)),
        out_shape=jax.ShapeDtypeStruct((B, n_mem, D), BF16),
        compiler_params=_params(("parallel", "parallel"), est),
        name="mem_k" if normalise else "mem_v",
    )(mem, g_mem, wc_kv, gc_k)


def _cross_attn_kernel(x_ref, g_ref, wq_ref, gq_ref, k_ref, v_ref, wo_ref, o_ref):
    x = x_ref[0]
    h = _rms(x, g_ref[...]).astype(BF16)
    q = _dot(h, wq_ref[...])
    hd = gq_ref.shape[1]
    scale = hd ** -0.5
    acc = x
    for c0 in range(0, q.shape[1], hd):
        qn = _rms(q[:, c0:c0 + hd], gq_ref[...]).astype(BF16)
        s = _dot_nt(qn, k_ref[0, :, c0:c0 + hd]) * scale
        p = jnp.exp(s - jnp.max(s, axis=-1, keepdims=True))
        p = p / jnp.sum(p, axis=-1, keepdims=True)
        oh = _dot(p.astype(BF16), v_ref[0, :, c0:c0 + hd])
        acc = acc + _dot(oh.astype(BF16), wo_ref[c0:c0 + hd, :])
    o_ref[0] = acc


def _cross_attn(x, g_cross, wc_q, gc_q, k_mem, v_mem, wc_o, *, tm):
    B, S, D = x.shape
    n_mem = k_mem.shape[1]
    row = lambda b, i: (b, i, 0)
    est = (wc_q.size * 2 + wc_o.size * 2 + 4 * n_mem * D * 2 + 4 * tm * D * 4 + 5 * tm * D * 4 + (4 << 20))
    return pl.pallas_call(
        _cross_attn_kernel,
        grid=(B, S // tm),
        in_specs=[pl.BlockSpec((1, tm, D), row),
                  pl.BlockSpec((1, D), lambda b, i: (0, 0)),
                  _resident(wc_q.shape),
                  pl.BlockSpec(gc_q.shape, lambda b, i: (0, 0)),
                  pl.BlockSpec((1, n_mem, D), lambda b, i: (b, 0, 0)),
                  pl.BlockSpec((1, n_mem, D), lambda b, i: (b, 0, 0)),
                  _resident(wc_o.shape)],
        out_specs=pl.BlockSpec((1, tm, D), row),
        out_shape=jax.ShapeDtypeStruct((B, S, D), F32),
        compiler_params=_params(("parallel", "parallel"), est),
        name="cross_attn",
    )(x, g_cross, wc_q, gc_q, k_mem, v_mem, wc_o)


def _conv_ffn_kernel(x_ref, xp_ref, xn_ref, g_ref, wg_ref, wv_ref, cwg_ref, cwv_ref, cbg_ref, cbv_ref, wd_ref,
                     o_ref, h_ref, *, tm):
    i = pl.program_id(1)
    f = pl.program_id(2)
    rows = tm + 2 * CONV_HALO

    @pl.when(f == 0)
    def _():
        g = g_ref[...]
        h_prev = jnp.where(i > 0, _rms(xp_ref[0], g), 0.0)
        h_next = jnp.where(i < pl.num_programs(1) - 1, _rms(xn_ref[0], g), 0.0)
        h_ref[0:CONV_HALO, :] = h_prev.astype(BF16)
        h_ref[CONV_HALO:CONV_HALO + tm, :] = _rms(x_ref[0], g).astype(BF16)
        h_ref[CONV_HALO + tm:rows, :] = h_next.astype(BF16)
        o_ref[0] = x_ref[0]

    h = h_ref[...]

    def conv(w_ref, cw_ref, cb_ref):
        u = _dot(h, w_ref[...])
        below = pltpu.roll(u, 1, 0)[CONV_HALO:CONV_HALO + tm]
        above = pltpu.roll(u, rows - 1, 0)[CONV_HALO:CONV_HALO + tm]
        mid = u[CONV_HALO:CONV_HALO + tm]
        return below * cw_ref[0:1, :] + mid * cw_ref[1:2, :] + above * cw_ref[2:3, :] + cb_ref[...]

    act = jax.nn.gelu(conv(wg_ref, cwg_ref, cbg_ref)) * conv(wv_ref, cwv_ref, cbv_ref)
    o_ref[0] += _dot(act.astype(BF16), wd_ref[...])


def _conv_ffn(x, g_ffn, w_up, conv_w, conv_b, w_down, *, tm, tf):
    B, S, D = x.shape
    d_ff = w_down.shape[0]
    nf = d_ff // tf
    halo_blocks = tm // CONV_HALO
    n_halo = S // CONV_HALO
    rows = tm + 2 * CONV_HALO
    est = (4 * tm * D * 4 + 2 * 3 * D * tf * 2 + rows * D * 2 + 8 * rows * tf * 4 + 2 * tm * D * 4 + (4 << 20))
    gate = lambda b, i, f: (0, f)
    val = lambda b, i, f: (0, nf + f)
    return pl.pallas_call(
        functools.partial(_conv_ffn_kernel, tm=tm),
        grid=(B, S // tm, nf),
        in_specs=[pl.BlockSpec((1, tm, D), lambda b, i, f: (b, i, 0)),
                  pl.BlockSpec((1, CONV_HALO, D), lambda b, i, f: (b, jnp.maximum(i * halo_blocks - 1, 0), 0)),
                  pl.BlockSpec((1, CONV_HALO, D),
                               lambda b, i, f: (b, jnp.minimum((i + 1) * halo_blocks, n_halo - 1), 0)),
                  pl.BlockSpec((1, D), lambda b, i, f: (0, 0)),
                  pl.BlockSpec((D, tf), gate),
                  pl.BlockSpec((D, tf), val),
                  pl.BlockSpec((conv_w.shape[0], tf), gate),
                  pl.BlockSpec((conv_w.shape[0], tf), val),
                  pl.BlockSpec((1, tf), gate),
                  pl.BlockSpec((1, tf), val),
                  pl.BlockSpec((tf, D), lambda b, i, f: (f, 0))],
        out_specs=pl.BlockSpec((1, tm, D), lambda b, i, f: (b, i, 0)),
        out_shape=jax.ShapeDtypeStruct((B, S, D), F32),
        scratch_shapes=[pltpu.VMEM((rows, D), BF16)],
        compiler_params=_params(("parallel", "parallel", "arbitrary"), est),
        name="conv_ffn",
    )(x, x, x, g_ffn, w_up, w_up, conv_w, conv_w, conv_b, conv_b, w_down)


def _rope_tables(seq, gain, scale):
    half = DIFF_QKDIM // 2
    inv = ROPE_THETA ** (-jnp.arange(half, dtype=F32) / half)
    ang = jnp.arange(seq, dtype=F32)[:, None] * inv[None, :]
    cos = jnp.cos(ang)
    sin = jnp.sin(ang)
    gain = gain.astype(F32)
    a = jnp.concatenate([gain[:half] * cos, gain[half:] * cos], axis=1) * scale
    b = jnp.concatenate([-gain[half:] * sin, gain[:half] * sin], axis=1) * scale
    reps = V7X_LANES // DIFF_QKDIM
    return jnp.tile(a, (1, reps)), jnp.tile(b, (1, reps))


def _segment_ones(width):
    seg = jnp.arange(width) // DIFF_QKDIM
    return (seg[:, None] == seg[None, :]).astype(BF16)


def _tile(n, target):
    t = min(n, target)
    assert n % t == 0, (n, t)
    return t


def _run_trunk(x, mem, layers):
    B, S, D = x.shape
    tm = _tile(S, 512)
    seg = _segment_ones(256)
    for l, p in enumerate(layers):
        lam_init = 0.8 - 0.6 * math.exp(-0.3 * l)
        aq, bq = _rope_tables(S, p["g_q"], DIFF_QKDIM ** -0.5 * math.log2(math.e))
        ak, bk = _rope_tables(S, p["g_k"], 1.0)
        q, k, v, u = _mix_in(x, p["g_mix"], p["w_in"], seg, aq, bq, ak, bk, tm=tm)
        a = _diff_attn(p["lam"], q, k, v, p["g_sub"].reshape(-1, 1), lam_init, tq=_tile(S, 512), tk=_tile(S // 4, 512),
                       blocks_per_trip=4)
        x = _mix_out(a, u, x, p["w_pool"], p["pool_scale"], p["w_out"], tm=tm)
        k_mem = _mem_kv(mem, p["g_mem"], p["wc_kv"], p["gc_k"], normalise=True, col_block0=0)
        v_mem = _mem_kv(mem, p["g_mem"], p["wc_kv"], p["gc_k"], normalise=False, col_block0=N_CROSS_HEADS)
        x = _cross_attn(x, p["g_cross"], p["wc_q"], p["gc_q"], k_mem, v_mem, p["wc_o"], tm=tm)
        x = _conv_ffn(x, p["g_ffn"], p["w_up"], p["conv_w"], p["conv_b"], p["w_down"], tm=tm, tf=512)
    return x


def kernel(x_prompt, x_sample, mem_prompt, mem_sample, g_mix, w_in, g_q, g_k, lam_q1, lam_k1, lam_q2, lam_k2,
           g_sub, w_pool, pool_scale, w_out, g_cross, g_mem, wc_q, wc_kv, gc_q, gc_k, wc_o, g_ffn, w_up,
           conv_w, conv_b, w_down):
    depth = w_in.shape[0]
    layers = []
    for l in range(depth):
        lam_init = 0.8 - 0.6 * math.exp(-0.3 * l)
        lam = (jnp.exp(jnp.sum(lam_q1[l].astype(F32) * lam_k1[l].astype(F32)))
               - jnp.exp(jnp.sum(lam_q2[l].astype(F32) * lam_k2[l].astype(F32))) + lam_init)
        row = lambda t: t[l].reshape(1, -1).astype(F32)
        layers.append(dict(
            lam=lam.reshape(1).astype(F32),
            g_mix=row(g_mix), g_q=g_q[l], g_k=g_k[l], g_sub=row(g_sub), pool_scale=row(pool_scale),
            g_cross=row(g_cross), g_mem=row(g_mem), gc_q=row(gc_q), gc_k=row(gc_k), g_ffn=row(g_ffn),
            conv_w=conv_w[l].astype(F32), conv_b=row(conv_b),
            w_in=w_in[l].astype(BF16), w_pool=w_pool[l].astype(BF16), w_out=w_out[l].astype(BF16),
            wc_q=wc_q[l].astype(BF16), wc_kv=wc_kv[l].astype(BF16), wc_o=wc_o[l].astype(BF16),
            w_up=w_up[l].astype(BF16), w_down=w_down[l].astype(BF16)))
    return (_run_trunk(x_prompt, mem_prompt, layers), _run_trunk(x_sample, mem_sample, layers))
```

```python
import functools
import math

import jax
import jax.numpy as jnp
from jax import lax
from jax.experimental import pallas as pl
from jax.experimental.pallas import tpu as pltpu

F32 = jnp.float32
BF16 = jnp.bfloat16

N_DIFF_HEADS = 8
DIFF_QKDIM = 64
DIFF_VDIM = 128
HEAD_COLS = 2 * DIFF_QKDIM
POOL_WINDOWS = (2, 4, 8, 16)
POOL_GROUP_WIDTH = 256
POOL_HALO = 16
N_CROSS_HEADS = 4
CONV_HALO = 8
ROPE_THETA = 10000.0
EPS = 1e-6
SOFTMAX_SHIFT_LIMIT = 60.0

V7X_VMEM_BYTES = 64 * 1024 * 1024
V7X_LANES = 128
VMEM_LIMIT_CAP = V7X_VMEM_BYTES - 6 * 1024 * 1024


def _vmem_limit(estimate_bytes):
    return int(min(VMEM_LIMIT_CAP, max(32 * 1024 * 1024, estimate_bytes)))


def _params(semantics, vmem_estimate):
    return pltpu.CompilerParams(dimension_semantics=semantics, vmem_limit_bytes=_vmem_limit(vmem_estimate))


def _resident(shape):
    return pl.BlockSpec(shape, lambda *_: (0,) * len(shape), pipeline_mode=pl.Buffered(1))


def _rms(x, gain):
    ms = jnp.mean(x * x, axis=-1, keepdims=True)
    return x * lax.rsqrt(ms + EPS) * gain


def _dot(a, b):
    return jnp.dot(a, b, preferred_element_type=F32)


def _dot_nt(a, b):
    return lax.dot_general(a, b, (((1,), (1,)), ((), ())), preferred_element_type=F32)


def _mix_in_kernel(x_ref, g_ref, w_ref, seg_ref, aq_ref, bq_ref, ak_ref, bk_ref,
                   q_ref, k_ref, v_ref, u_ref, *, qk_width, v_width):
    h = _rms(x_ref[0], g_ref[...]).astype(BF16)
    lane = lax.broadcasted_iota(jnp.int32, (1, V7X_LANES), 1)
    partner_is_above = (lane & (DIFF_QKDIM // 2)) == 0
    seg = seg_ref[...]
    chunk = seg.shape[0]
    for col0, a_ref, b_ref, o_ref in ((0, aq_ref, bq_ref, q_ref), (qk_width, ak_ref, bk_ref, k_ref)):
        a = a_ref[...]
        b = b_ref[...]
        zfull = _dot(h, w_ref[:, col0:col0 + qk_width])
        for c0 in range(0, qk_width, chunk):
            z = zfull[:, c0:c0 + chunk]
            ss = _dot((z * z).astype(BF16), seg)
            zn = z * lax.rsqrt(ss * (1.0 / DIFF_QKDIM) + EPS)
            for c in range(0, chunk, V7X_LANES):
                zc = zn[:, c:c + V7X_LANES]
                partner = jnp.where(partner_is_above,
                                    pltpu.roll(zc, V7X_LANES - DIFF_QKDIM // 2, 1),
                                    pltpu.roll(zc, DIFF_QKDIM // 2, 1))
                o_ref[0, :, c0 + c:c0 + c + V7X_LANES] = (zc * a + partner * b).astype(o_ref.dtype)
    v0 = 2 * qk_width
    v_ref[0] = _dot(h, w_ref[:, v0:v0 + v_width]).astype(v_ref.dtype)
    u_ref[0] = _dot(h, w_ref[:, v0 + v_width:])


def _mix_in(x, g, w_in, seg, aq, bq, ak, bk, *, tm):
    B, S, D = x.shape
    qk_width = N_DIFF_HEADS * HEAD_COLS
    v_width = N_DIFF_HEADS * DIFF_VDIM
    u_width = w_in.shape[1] - 2 * qk_width - v_width
    row = lambda b, i: (b, i, 0)
    tab = pl.BlockSpec((tm, V7X_LANES), lambda b, i: (i, 0))
    est = (w_in.size * 2 + 2 * tm * D * 4 + 2 * tm * (2 * qk_width + v_width) * 2 + 2 * tm * u_width * 4
           + tm * D * 2 + 8 * tm * 1024 * 4 + 8 * tm * V7X_LANES * 4 + (4 << 20))
    return pl.pallas_call(
        functools.partial(_mix_in_kernel, qk_width=qk_width, v_width=v_width),
        grid=(B, S // tm),
        in_specs=[pl.BlockSpec((1, tm, D), row),
                  pl.BlockSpec((1, D), lambda b, i: (0, 0)),
                  _resident(w_in.shape),
                  pl.BlockSpec(seg.shape, lambda b, i: (0, 0)),
                  tab, tab, tab, tab],
        out_specs=[pl.BlockSpec((1, tm, qk_width), row),
                   pl.BlockSpec((1, tm, qk_width), row),
                   pl.BlockSpec((1, tm, v_width), row),
                   pl.BlockSpec((1, tm, u_width), row)],
        out_shape=[jax.ShapeDtypeStruct((B, S, qk_width), BF16),
                   jax.ShapeDtypeStruct((B, S, qk_width), BF16),
                   jax.ShapeDtypeStruct((B, S, v_width), BF16),
                   jax.ShapeDtypeStruct((B, S, u_width), F32)],
        compiler_params=_params(("parallel", "parallel"), est),
        name="mix_in",
    )(x, g, w_in, seg, aq, bq, ak, bk)


def _dot_tn(a, b):
    return lax.dot_general(a, b, (((0,), (0,)), ((), ())), preferred_element_type=F32)


def _diff_attn_kernel(lam_ref, q_ref, k_ref, v_ref, gsub_ref, o_ref, qbd_ref, s0_ref, s1_ref, m_ref, l_ref, acc_ref,
                      *, tq, tk, blocks_per_trip, out_scale):
    qt = q_ref[0].astype(F32).T
    row = lax.broadcasted_iota(jnp.int32, (HEAD_COLS, 1), 0)
    zero = jnp.zeros_like(qt)
    qbd_ref[:, 0:tq] = jnp.where(row < DIFF_QKDIM, qt, zero).astype(BF16)
    qbd_ref[:, tq:2 * tq] = jnp.where(row >= DIFF_QKDIM, qt, zero).astype(BF16)
    m_ref[...] = jnp.full(m_ref.shape, -jnp.inf, F32)
    l_ref[...] = jnp.zeros(l_ref.shape, F32)
    acc_ref[...] = jnp.zeros(acc_ref.shape, F32)

    def scores(j, s_ref):
        k0 = pl.multiple_of(j * tk, tk)
        s_ref[...] = _dot(k_ref[0, pl.ds(k0, tk), :], qbd_ref[...])

    def absorb(j, s_ref):
        k0 = pl.multiple_of(j * tk, tk)
        vb = v_ref[0, pl.ds(k0, tk), :]
        s = s_ref[...]
        m_prev = m_ref[...]
        m_new = jnp.maximum(m_prev, jnp.max(s, axis=0, keepdims=True))
        alpha = jnp.exp2(m_prev - m_new)
        p = jnp.exp2(s - m_new)
        l_ref[...] = alpha * l_ref[...] + jnp.sum(p, axis=0, keepdims=True)
        m_ref[...] = m_new
        pb = p.astype(BF16)
        for c in range(2):
            cols = slice(c * tq, (c + 1) * tq)
            acc_ref[c] = alpha[:, cols] * acc_ref[c] + _dot_tn(vb, pb[:, cols])

    bufs = (s0_ref, s1_ref)
    n_trips = k_ref.shape[1] // (blocks_per_trip * tk)
    scores(0, s0_ref)

    def trip(i, carry):
        j = blocks_per_trip * i
        for u in range(blocks_per_trip):
            scores(j + u + 1, bufs[(u + 1) % 2])
            absorb(j + u, bufs[u % 2])
        return carry

    lax.fori_loop(0, n_trips - 1, trip, 0)
    j_last = blocks_per_trip * (n_trips - 1)
    for u in range(blocks_per_trip):
        if u + 1 < blocks_per_trip:
            scores(j_last + u + 1, bufs[(u + 1) % 2])
        absorb(j_last + u, bufs[u % 2])

    inv = 1.0 / l_ref[...]
    o = acc_ref[0] * inv[:, 0:tq] - lam_ref[0] * (acc_ref[1] * inv[:, tq:2 * tq])
    ms = jnp.mean(o * o, axis=0, keepdims=True)
    o = o * lax.rsqrt(ms + EPS) * gsub_ref[...] * out_scale
    o_ref[0] = o.T.astype(o_ref.dtype)


def _diff_attn(lam, q, k, v, g_sub_col, lam_init, *, tq, tk, blocks_per_trip):
    B, S, _ = q.shape
    assert blocks_per_trip % 2 == 0 and S % (blocks_per_trip * tk) == 0, (S, tk, blocks_per_trip)
    est = (2 * 2 * S * (HEAD_COLS + DIFF_VDIM) * 2 + 4 * tq * HEAD_COLS * 2 + 2 * tq * HEAD_COLS * 2
           + 2 * tq * DIFF_VDIM * 4 + 8 * tk * 2 * tq * 4 + (4 << 20))
    return pl.pallas_call(
        functools.partial(_diff_attn_kernel, tq=tq, tk=tk, blocks_per_trip=blocks_per_trip,
                          out_scale=1.0 - lam_init),
        grid=(B, N_DIFF_HEADS, S // tq),
        in_specs=[pl.BlockSpec(memory_space=pltpu.SMEM),
                  pl.BlockSpec((1, tq, HEAD_COLS), lambda b, h, i: (b, i, h)),
                  pl.BlockSpec((1, S, HEAD_COLS), lambda b, h, i: (b, 0, h)),
                  pl.BlockSpec((1, S, DIFF_VDIM), lambda b, h, i: (b, 0, h)),
                  pl.BlockSpec((DIFF_VDIM, 1), lambda b, h, i: (0, 0))],
        out_specs=pl.BlockSpec((1, tq, DIFF_VDIM), lambda b, h, i: (b, i, h)),
        out_shape=jax.ShapeDtypeStruct((B, S, N_DIFF_HEADS * DIFF_VDIM), BF16),
        scratch_shapes=[pltpu.VMEM((HEAD_COLS, 2 * tq), BF16),
                        pltpu.VMEM((tk, 2 * tq), F32),
                        pltpu.VMEM((tk, 2 * tq), F32),
                        pltpu.VMEM((1, 2 * tq), F32),
                        pltpu.VMEM((1, 2 * tq), F32),
                        pltpu.VMEM((2, DIFF_VDIM, tq), F32)],
        compiler_params=_params(("parallel", "parallel", "parallel"), est),
        name="diff_attn",
    )(lam, q, k, v, g_sub_col)


def _diff_attn_bounded_kernel(lam_ref, bound_ref, q_ref, k_ref, v_ref, gsub_ref, o_ref, qbd_ref, l_ref, acc_ref,
                              *, tq, tk, blocks_per_trip, out_scale):
    qt = q_ref[0].astype(F32).T
    row = lax.broadcasted_iota(jnp.int32, (HEAD_COLS, 1), 0)
    zero = jnp.zeros_like(qt)
    qbd_ref[:, 0:tq] = jnp.where(row < DIFF_QKDIM, qt, zero).astype(BF16)
    qbd_ref[:, tq:2 * tq] = jnp.where(row >= DIFF_QKDIM, qt, zero).astype(BF16)
    l_ref[...] = jnp.zeros(l_ref.shape, F32)
    acc_ref[...] = jnp.zeros(acc_ref.shape, F32)
    shift = bound_ref[0]

    def trip(i, carry):
        for u in range(blocks_per_trip):
            k0 = pl.multiple_of((blocks_per_trip * i + u) * tk, tk)
            vb = v_ref[0, pl.ds(k0, tk), :]
            p = jnp.exp2(_dot(k_ref[0, pl.ds(k0, tk), :], qbd_ref[...]) - shift)
            l_ref[...] += jnp.sum(p, axis=0, keepdims=True)
            pb = p.astype(BF16)
            for c in range(2):
                acc_ref[c] += _dot_tn(vb, pb[:, c * tq:(c + 1) * tq])
        return carry

    lax.fori_loop(0, k_ref.shape[1] // (blocks_per_trip * tk), trip, 0)

    inv = 1.0 / l_ref[...]
    o = acc_ref[0] * inv[:, 0:tq] - lam_ref[0] * (acc_ref[1] * inv[:, tq:2 * tq])
    ms = jnp.mean(o * o, axis=0, keepdims=True)
    o = o * lax.rsqrt(ms + EPS) * gsub_ref[...] * out_scale
    o_ref[0] = o.T.astype(o_ref.dtype)


def _diff_attn_bounded(lam, bound, q, k, v, g_sub_col, lam_init, *, tq, tk, blocks_per_trip):
    B, S, _ = q.shape
    assert S % (blocks_per_trip * tk) == 0, (S, tk, blocks_per_trip)
    est = (2 * 2 * S * (HEAD_COLS + DIFF_VDIM) * 2 + 4 * tq * HEAD_COLS * 2 + 2 * tq * HEAD_COLS * 2
           + 2 * tq * DIFF_VDIM * 4 + 8 * tk * 2 * tq * 4 + (4 << 20))
    return pl.pallas_call(
        functools.partial(_diff_attn_bounded_kernel, tq=tq, tk=tk, blocks_per_trip=blocks_per_trip,
                          out_scale=1.0 - lam_init),
        grid=(B, N_DIFF_HEADS, S // tq),
        in_specs=[pl.BlockSpec(memory_space=pltpu.SMEM),
                  pl.BlockSpec(memory_space=pltpu.SMEM),
                  pl.BlockSpec((1, tq, HEAD_COLS), lambda b, h, i: (b, i, h)),
                  pl.BlockSpec((1, S, HEAD_COLS), lambda b, h, i: (b, 0, h)),
                  pl.BlockSpec((1, S, DIFF_VDIM), lambda b, h, i: (b, 0, h)),
                  pl.BlockSpec((DIFF_VDIM, 1), lambda b, h, i: (0, 0))],
        out_specs=pl.BlockSpec((1, tq, DIFF_VDIM), lambda b, h, i: (b, i, h)),
        out_shape=jax.ShapeDtypeStruct((B, S, N_DIFF_HEADS * DIFF_VDIM), BF16),
        scratch_shapes=[pltpu.VMEM((HEAD_COLS, 2 * tq), BF16),
                        pltpu.VMEM((1, 2 * tq), F32),
                        pltpu.VMEM((2, DIFF_VDIM, tq), F32)],
        compiler_params=_params(("parallel", "parallel", "parallel"), est),
        name="diff_attn_bounded",
    )(lam, bound, q, k, v, g_sub_col)


def _mix_out_kernel(a_ref, u_ref, up_ref, un_ref, x_ref, wp_ref, ps_ref, wo_ref, o_ref, *, tm, seq):
    i = pl.program_id(1)
    u_prev = jnp.where(i > 0, up_ref[0], 0.0)
    u_next = jnp.where(i < pl.num_programs(1) - 1, un_ref[0], 0.0)
    ue = jnp.concatenate([u_prev, u_ref[0], u_next], axis=0)
    pos = i * tm + lax.broadcasted_iota(jnp.int32, (tm, 1), 0)
    a_width = a_ref.shape[2]
    acc = x_ref[0] + _dot(a_ref[0], wo_ref[0:a_width, :])
    for g, w in enumerate(POOL_WINDOWS):
        c0 = g * POOL_GROUP_WIDTH
        ug = ue[:, c0:c0 + POOL_GROUP_WIDTH]
        win = ug[POOL_HALO - w // 2:POOL_HALO - w // 2 + tm]
        for j in range(-(w // 2) + 1, w // 2):
            win = win + ug[POOL_HALO + j:POOL_HALO + j + tm]
        cnt = jnp.minimum(pos + w // 2, seq) - jnp.maximum(pos - w // 2, 0)
        z = win / cnt.astype(F32) - ug[POOL_HALO:POOL_HALO + tm]
        pg = _dot(z.astype(BF16), wp_ref[g]) * ps_ref[:, c0:c0 + POOL_GROUP_WIDTH]
        acc = acc + _dot(pg.astype(BF16), wo_ref[a_width + c0:a_width + c0 + POOL_GROUP_WIDTH, :])
    o_ref[0] = acc


def _mix_out(a, u, x, w_pool, pool_scale, w_out, *, tm):
    B, S, D = x.shape
    a_width, u_width = a.shape[2], u.shape[2]
    halo_blocks = tm // POOL_HALO
    n_halo = S // POOL_HALO
    row = lambda b, i: (b, i, 0)
    est = (w_out.size * 2 + w_pool.size * 2 * 2 + 4 * tm * D * 4 + 2 * tm * a_width * 2 + 2 * tm * u_width * 4
           + 6 * tm * u_width * 4 + 2 * tm * D * 4 + (4 << 20))
    return pl.pallas_call(
        functools.partial(_mix_out_kernel, tm=tm, seq=S),
        grid=(B, S // tm),
        in_specs=[pl.BlockSpec((1, tm, a_width), row),
                  pl.BlockSpec((1, tm, u_width), row),
                  pl.BlockSpec((1, POOL_HALO, u_width),
                               lambda b, i: (b, jnp.maximum(i * halo_blocks - 1, 0), 0)),
                  pl.BlockSpec((1, POOL_HALO, u_width),
                               lambda b, i: (b, jnp.minimum((i + 1) * halo_blocks, n_halo - 1), 0)),
                  pl.BlockSpec((1, tm, D), row),
                  pl.BlockSpec(w_pool.shape, lambda b, i: (0, 0, 0)),
                  pl.BlockSpec((1, u_width), lambda b, i: (0, 0)),
                  _resident(w_out.shape)],
        out_specs=pl.BlockSpec((1, tm, D), row),
        out_shape=jax.ShapeDtypeStruct((B, S, D), F32),
        compiler_params=_params(("parallel", "parallel"), est),
        name="mix_out",
    )(a, u, u, u, x, w_pool, pool_scale, w_out)


def _mem_kv_kernel(mem_ref, g_ref, w_ref, gk_ref, o_ref, *, normalise):
    m = _rms(mem_ref[0], g_ref[...]).astype(BF16)
    kv = _dot(m, w_ref[...])
    if normalise:
        kv = _rms(kv, gk_ref[...])
    o_ref[0] = kv.astype(o_ref.dtype)


def _mem_kv(mem, g_mem, wc_kv, gc_k, *, normalise, col_block0):
    B, n_mem, D = mem.shape
    hd = D // N_CROSS_HEADS
    est = 2 * n_mem * D * 4 + 2 * D * hd * 2 + 4 * n_mem * hd * 4 + n_mem * D * 4 + (4 << 20)
    return pl.pallas_call(
        functools.partial(_mem_kv_kernel, normalise=normalise),
        grid=(N_CROSS_HEADS, B),
        in_specs=[pl.BlockSpec((1, n_mem, D), lambda h, b: (b, 0, 0)),
                  pl.BlockSpec((1, D), lambda h, b: (0, 0)),
                  pl.BlockSpec((D, hd), lambda h, b: (0, col_block0 + h)),
                  pl.BlockSpec((1, hd), lambda h, b: (0, 0))],
        out_specs=pl.BlockSpec((1, n_mem, hd), lambda h, b: (b, 0, h)),
        out_shape=jax.ShapeDtypeStruct((B, n_mem, D), BF16),
        compiler_params=_params(("parallel", "parallel"), est),
        name="mem_k" if normalise else "mem_v",
    )(mem, g_mem, wc_kv, gc_k)


def _cross_attn_kernel(x_ref, g_ref, wq_ref, gq_ref, k_ref, v_ref, wo_ref, o_ref):
    x = x_ref[0]
    h = _rms(x, g_ref[...]).astype(BF16)
    q = _dot(h, wq_ref[...])
    hd = gq_ref.shape[1]
    scale = hd ** -0.5
    acc = x
    for c0 in range(0, q.shape[1], hd):
        qn = _rms(q[:, c0:c0 + hd], gq_ref[...]).astype(BF16)
        s = _dot_nt(qn, k_ref[0, :, c0:c0 + hd]) * scale
        p = jnp.exp(s - jnp.max(s, axis=-1, keepdims=True))
        p = p / jnp.sum(p, axis=-1, keepdims=True)
        oh = _dot(p.astype(BF16), v_ref[0, :, c0:c0 + hd])
        acc = acc + _dot(oh.astype(BF16), wo_ref[c0:c0 + hd, :])
    o_ref[0] = acc


def _cross_attn(x, g_cross, wc_q, gc_q, k_mem, v_mem, wc_o, *, tm):
    B, S, D = x.shape
    n_mem = k_mem.shape[1]
    row = lambda b, i: (b, i, 0)
    est = (wc_q.size * 2 + wc_o.size * 2 + 4 * n_mem * D * 2 + 4 * tm * D * 4 + 5 * tm * D * 4 + (4 << 20))
    return pl.pallas_call(
        _cross_attn_kernel,
        grid=(B, S // tm),
        in_specs=[pl.BlockSpec((1, tm, D), row),
                  pl.BlockSpec((1, D), lambda b, i: (0, 0)),
                  _resident(wc_q.shape),
                  pl.BlockSpec(gc_q.shape, lambda b, i: (0, 0)),
                  pl.BlockSpec((1, n_mem, D), lambda b, i: (b, 0, 0)),
                  pl.BlockSpec((1, n_mem, D), lambda b, i: (b, 0, 0)),
                  _resident(wc_o.shape)],
        out_specs=pl.BlockSpec((1, tm, D), row),
        out_shape=jax.ShapeDtypeStruct((B, S, D), F32),
        compiler_params=_params(("parallel", "parallel"), est),
        name="cross_attn",
    )(x, g_cross, wc_q, gc_q, k_mem, v_mem, wc_o)


def _conv_ffn_kernel(x_ref, xp_ref, xn_ref, g_ref, wg_ref, wv_ref, cwg_ref, cwv_ref, cbg_ref, cbv_ref, wd_ref,
                     o_ref, h_ref, *, tm, sub):
    i = pl.program_id(1)
    f = pl.program_id(2)
    rows = tm + 2 * CONV_HALO

    @pl.when(f == 0)
    def _():
        g = g_ref[...]
        h_prev = jnp.where(i > 0, _rms(xp_ref[0], g), 0.0)
        h_next = jnp.where(i < pl.num_programs(1) - 1, _rms(xn_ref[0], g), 0.0)
        h_ref[0:CONV_HALO, :] = h_prev.astype(BF16)
        h_ref[CONV_HALO:CONV_HALO + tm, :] = _rms(x_ref[0], g).astype(BF16)
        h_ref[CONV_HALO + tm:rows, :] = h_next.astype(BF16)
        o_ref[0] = x_ref[0]

    h = h_ref[...]

    def conv(w_ref, cw_ref, cb_ref, cols):
        u = _dot(h, w_ref[:, cols])
        below = pltpu.roll(u, 1, 0)[CONV_HALO:CONV_HALO + tm]
        above = pltpu.roll(u, rows - 1, 0)[CONV_HALO:CONV_HALO + tm]
        mid = u[CONV_HALO:CONV_HALO + tm]
        return below * cw_ref[0:1, cols] + mid * cw_ref[1:2, cols] + above * cw_ref[2:3, cols] + cb_ref[:, cols]

    tf = wd_ref.shape[0]
    y = None
    for c0 in range(0, tf, sub):
        cols = slice(c0, c0 + sub)
        act = jax.nn.gelu(conv(wg_ref, cwg_ref, cbg_ref, cols)) * conv(wv_ref, cwv_ref, cbv_ref, cols)
        part = _dot(act.astype(BF16), wd_ref[cols, :])
        y = part if y is None else y + part
    o_ref[0] += y


def _conv_ffn(x, g_ffn, w_up, conv_w, conv_b, w_down, *, tm, tf):
    B, S, D = x.shape
    d_ff = w_down.shape[0]
    nf = d_ff // tf
    halo_blocks = tm // CONV_HALO
    n_halo = S // CONV_HALO
    rows = tm + 2 * CONV_HALO
    est = (4 * tm * D * 4 + 2 * 3 * D * tf * 2 + rows * D * 2 + 8 * rows * tf * 4 + 2 * tm * D * 4 + (4 << 20))
    gate = lambda b, i, f: (0, f)
    val = lambda b, i, f: (0, nf + f)
    return pl.pallas_call(
        functools.partial(_conv_ffn_kernel, tm=tm, sub=tf),
        grid=(B, S // tm, nf),
        in_specs=[pl.BlockSpec((1, tm, D), lambda b, i, f: (b, i, 0)),
                  pl.BlockSpec((1, CONV_HALO, D), lambda b, i, f: (b, jnp.maximum(i * halo_blocks - 1, 0), 0)),
                  pl.BlockSpec((1, CONV_HALO, D),
                               lambda b, i, f: (b, jnp.minimum((i + 1) * halo_blocks, n_halo - 1), 0)),
                  pl.BlockSpec((1, D), lambda b, i, f: (0, 0)),
                  pl.BlockSpec((D, tf), gate),
                  pl.BlockSpec((D, tf), val),
                  pl.BlockSpec((conv_w.shape[0], tf), gate),
                  pl.BlockSpec((conv_w.shape[0], tf), val),
                  pl.BlockSpec((1, tf), gate),
                  pl.BlockSpec((1, tf), val),
                  pl.BlockSpec((tf, D), lambda b, i, f: (f, 0))],
        out_specs=pl.BlockSpec((1, tm, D), lambda b, i, f: (b, i, 0)),
        out_shape=jax.ShapeDtypeStruct((B, S, D), F32),
        scratch_shapes=[pltpu.VMEM((rows, D), BF16)],
        compiler_params=_params(("parallel", "parallel", "arbitrary"), est),
        name="conv_ffn",
    )(x, x, x, g_ffn, w_up, w_up, conv_w, conv_w, conv_b, conv_b, w_down)


def _rope_tables(seq, gain, scale):
    half = DIFF_QKDIM // 2
    inv = ROPE_THETA ** (-jnp.arange(half, dtype=F32) / half)
    ang = jnp.arange(seq, dtype=F32)[:, None] * inv[None, :]
    cos = jnp.cos(ang)
    sin = jnp.sin(ang)
    gain = gain.astype(F32)
    a = jnp.concatenate([gain[:half] * cos, gain[half:] * cos], axis=1) * scale
    b = jnp.concatenate([-gain[half:] * sin, gain[:half] * sin], axis=1) * scale
    reps = V7X_LANES // DIFF_QKDIM
    return jnp.tile(a, (1, reps)), jnp.tile(b, (1, reps))


def _segment_ones(width):
    seg = jnp.arange(width) // DIFF_QKDIM
    return (seg[:, None] == seg[None, :]).astype(BF16)


def _tile(n, target):
    t = min(n, target)
    assert n % t == 0, (n, t)
    return t


def _run_trunk(x, mem, layers):
    B, S, D = x.shape
    tm = _tile(S, 512)
    seg = _segment_ones(256)
    for l, p in enumerate(layers):
        lam_init = 0.8 - 0.6 * math.exp(-0.3 * l)
        q_scale = DIFF_QKDIM ** -0.5 * math.log2(math.e)
        aq, bq = _rope_tables(S, p["g_q"], q_scale)
        ak, bk = _rope_tables(S, p["g_k"], 1.0)
        q, k, v, u = _mix_in(x, p["g_mix"], p["w_in"], seg, aq, bq, ak, bk, tm=tm)
        bound = (DIFF_QKDIM * q_scale * 1.01 * jnp.max(jnp.abs(p["g_q"])) * jnp.max(jnp.abs(p["g_k"]))).astype(F32)
        attn = functools.partial(_diff_attn, p["lam"], q, k, v, p["g_sub"].reshape(-1, 1), lam_init,
                                 tq=_tile(S, 512), tk=_tile(S // 4, 512), blocks_per_trip=4)
        attn_bounded = functools.partial(_diff_attn_bounded, p["lam"], bound.reshape(1), q, k, v,
                                         p["g_sub"].reshape(-1, 1), lam_init,
                                         tq=_tile(S, 512), tk=_tile(S // 8, 512), blocks_per_trip=8)
        a = lax.cond(bound <= SOFTMAX_SHIFT_LIMIT, attn_bounded, attn)
        x = _mix_out(a, u, x, p["w_pool"], p["pool_scale"], p["w_out"], tm=tm)
        k_mem = _mem_kv(mem, p["g_mem"], p["wc_kv"], p["gc_k"], normalise=True, col_block0=0)
        v_mem = _mem_kv(mem, p["g_mem"], p["wc_kv"], p["gc_k"], normalise=False, col_block0=N_CROSS_HEADS)
        x = _cross_attn(x, p["g_cross"], p["wc_q"], p["gc_q"], k_mem, v_mem, p["wc_o"], tm=tm)
        x = _conv_ffn(x, p["g_ffn"], p["w_up"], p["conv_w"], p["conv_b"], p["w_down"], tm=tm, tf=512)
    return x


def kernel(x_prompt, x_sample, mem_prompt, mem_sample, g_mix, w_in, g_q, g_k, lam_q1, lam_k1, lam_q2, lam_k2,
           g_sub, w_pool, pool_scale, w_out, g_cross, g_mem, wc_q, wc_kv, gc_q, gc_k, wc_o, g_ffn, w_up,
           conv_w, conv_b, w_down):
    depth = w_in.shape[0]
    layers = []
    for l in range(depth):
        lam_init = 0.8 - 0.6 * math.exp(-0.3 * l)
        lam = (jnp.exp(jnp.sum(lam_q1[l].astype(F32) * lam_k1[l].astype(F32)))
               - jnp.exp(jnp.sum(lam_q2[l].astype(F32) * lam_k2[l].astype(F32))) + lam_init)
        row = lambda t: t[l].reshape(1, -1).astype(F32)
        layers.append(dict(
            lam=lam.reshape(1).astype(F32),
            g_mix=row(g_mix), g_q=g_q[l], g_k=g_k[l], g_sub=row(g_sub), pool_scale=row(pool_scale),
            g_cross=row(g_cross), g_mem=row(g_mem), gc_q=row(gc_q), gc_k=row(gc_k), g_ffn=row(g_ffn),
            conv_w=conv_w[l].astype(F32), conv_b=row(conv_b),
            w_in=w_in[l].astype(BF16), w_pool=w_pool[l].astype(BF16), w_out=w_out[l].astype(BF16),
            wc_q=wc_q[l].astype(BF16), wc_kv=wc_kv[l].astype(BF16), wc_o=wc_o[l].astype(BF16),
            w_up=w_up[l].astype(BF16), w_down=w_down[l].astype(BF16)))
    return (_run_trunk(x_prompt, mem_prompt, layers), _run_trunk(x_sample, mem_sample, layers))
```

```python
import functools
import math

import jax
import jax.numpy as jnp
from jax import lax
from jax.experimental import pallas as pl
from jax.experimental.pallas import tpu as pltpu

F32 = jnp.float32
BF16 = jnp.bfloat16

N_DIFF_HEADS = 8
DIFF_QKDIM = 64
DIFF_VDIM = 128
HEAD_COLS = 2 * DIFF_QKDIM
POOL_WINDOWS = (2, 4, 8, 16)
POOL_GROUP_WIDTH = 256
POOL_HALO = 16
N_CROSS_HEADS = 4
CONV_HALO = 8
ROPE_THETA = 10000.0
EPS = 1e-6
SOFTMAX_SHIFT_LIMIT = 60.0

V7X_VMEM_BYTES = 64 * 1024 * 1024
V7X_LANES = 128
VMEM_LIMIT_CAP = V7X_VMEM_BYTES - 6 * 1024 * 1024


def _vmem_limit(estimate_bytes):
    return int(min(VMEM_LIMIT_CAP, max(32 * 1024 * 1024, estimate_bytes)))


def _params(semantics, vmem_estimate):
    return pltpu.CompilerParams(dimension_semantics=semantics, vmem_limit_bytes=_vmem_limit(vmem_estimate))


def _resident(shape):
    return pl.BlockSpec(shape, lambda *_: (0,) * len(shape), pipeline_mode=pl.Buffered(1))


def _rms(x, gain):
    ms = jnp.mean(x * x, axis=-1, keepdims=True)
    return x * lax.rsqrt(ms + EPS) * gain


def _dot(a, b):
    return jnp.dot(a, b, preferred_element_type=F32)


def _dot_nt(a, b):
    return lax.dot_general(a, b, (((1,), (1,)), ((), ())), preferred_element_type=F32)


def _mix_in_kernel(x_ref, g_ref, w_ref, seg_ref, aq_ref, bq_ref, ak_ref, bk_ref,
                   q_ref, k_ref, v_ref, u_ref, *, qk_width, v_width):
    h = _rms(x_ref[0], g_ref[...]).astype(BF16)
    lane = lax.broadcasted_iota(jnp.int32, (1, V7X_LANES), 1)
    partner_is_above = (lane & (DIFF_QKDIM // 2)) == 0
    seg = seg_ref[...]
    chunk = seg.shape[0]
    for col0, a_ref, b_ref, o_ref in ((0, aq_ref, bq_ref, q_ref), (qk_width, ak_ref, bk_ref, k_ref)):
        a = a_ref[...]
        b = b_ref[...]
        zfull = _dot(h, w_ref[:, col0:col0 + qk_width])
        for c0 in range(0, qk_width, chunk):
            z = zfull[:, c0:c0 + chunk]
            ss = _dot((z * z).astype(BF16), seg)
            zn = z * lax.rsqrt(ss * (1.0 / DIFF_QKDIM) + EPS)
            for c in range(0, chunk, V7X_LANES):
                zc = zn[:, c:c + V7X_LANES]
                partner = jnp.where(partner_is_above,
                                    pltpu.roll(zc, V7X_LANES - DIFF_QKDIM // 2, 1),
                                    pltpu.roll(zc, DIFF_QKDIM // 2, 1))
                o_ref[0, :, c0 + c:c0 + c + V7X_LANES] = (zc * a + partner * b).astype(o_ref.dtype)
    v0 = 2 * qk_width
    v_ref[0] = _dot(h, w_ref[:, v0:v0 + v_width]).astype(v_ref.dtype)
    u_ref[0] = _dot(h, w_ref[:, v0 + v_width:])


def _mix_in(x, g, w_in, seg, aq, bq, ak, bk, *, tm):
    B, S, D = x.shape
    qk_width = N_DIFF_HEADS * HEAD_COLS
    v_width = N_DIFF_HEADS * DIFF_VDIM
    u_width = w_in.shape[1] - 2 * qk_width - v_width
    row = lambda b, i: (b, i, 0)
    tab = pl.BlockSpec((tm, V7X_LANES), lambda b, i: (i, 0))
    est = (w_in.size * 2 + 2 * tm * D * 4 + 2 * tm * (2 * qk_width + v_width) * 2 + 2 * tm * u_width * 4
           + tm * D * 2 + 8 * tm * 1024 * 4 + 8 * tm * V7X_LANES * 4 + (4 << 20))
    return pl.pallas_call(
        functools.partial(_mix_in_kernel, qk_width=qk_width, v_width=v_width),
        grid=(B, S // tm),
        in_specs=[pl.BlockSpec((1, tm, D), row),
                  pl.BlockSpec((1, D), lambda b, i: (0, 0)),
                  _resident(w_in.shape),
                  pl.BlockSpec(seg.shape, lambda b, i: (0, 0)),
                  tab, tab, tab, tab],
        out_specs=[pl.BlockSpec((1, tm, qk_width), row),
                   pl.BlockSpec((1, tm, qk_width), row),
                   pl.BlockSpec((1, tm, v_width), row),
                   pl.BlockSpec((1, tm, u_width), row)],
        out_shape=[jax.ShapeDtypeStruct((B, S, qk_width), BF16),
                   jax.ShapeDtypeStruct((B, S, qk_width), BF16),
                   jax.ShapeDtypeStruct((B, S, v_width), BF16),
                   jax.ShapeDtypeStruct((B, S, u_width), F32)],
        compiler_params=_params(("parallel", "parallel"), est),
        name="mix_in",
    )(x, g, w_in, seg, aq, bq, ak, bk)


def _dot_tn(a, b):
    return lax.dot_general(a, b, (((0,), (0,)), ((), ())), preferred_element_type=F32)


def _diff_attn_kernel(lam_ref, q_ref, k_ref, v_ref, gsub_ref, o_ref, qbd_ref, s0_ref, s1_ref, m_ref, l_ref, acc_ref,
                      *, tq, tk, blocks_per_trip, out_scale):
    qt = q_ref[0].astype(F32).T
    row = lax.broadcasted_iota(jnp.int32, (HEAD_COLS, 1), 0)
    zero = jnp.zeros_like(qt)
    qbd_ref[:, 0:tq] = jnp.where(row < DIFF_QKDIM, qt, zero).astype(BF16)
    qbd_ref[:, tq:2 * tq] = jnp.where(row >= DIFF_QKDIM, qt, zero).astype(BF16)
    m_ref[...] = jnp.full(m_ref.shape, -jnp.inf, F32)
    l_ref[...] = jnp.zeros(l_ref.shape, F32)
    acc_ref[...] = jnp.zeros(acc_ref.shape, F32)

    def scores(j, s_ref):
        k0 = pl.multiple_of(j * tk, tk)
        s_ref[...] = _dot(k_ref[0, pl.ds(k0, tk), :], qbd_ref[...])

    def absorb(j, s_ref):
        k0 = pl.multiple_of(j * tk, tk)
        vb = v_ref[0, pl.ds(k0, tk), :]
        s = s_ref[...]
        m_prev = m_ref[...]
        m_new = jnp.maximum(m_prev, jnp.max(s, axis=0, keepdims=True))
        alpha = jnp.exp2(m_prev - m_new)
        p = jnp.exp2(s - m_new)
        l_ref[...] = alpha * l_ref[...] + jnp.sum(p, axis=0, keepdims=True)
        m_ref[...] = m_new
        pb = p.astype(BF16)
        for c in range(2):
            cols = slice(c * tq, (c + 1) * tq)
            acc_ref[c] = alpha[:, cols] * acc_ref[c] + _dot_tn(vb, pb[:, cols])

    bufs = (s0_ref, s1_ref)
    n_trips = k_ref.shape[1] // (blocks_per_trip * tk)
    scores(0, s0_ref)

    def trip(i, carry):
        j = blocks_per_trip * i
        for u in range(blocks_per_trip):
            scores(j + u + 1, bufs[(u + 1) % 2])
            absorb(j + u, bufs[u % 2])
        return carry

    lax.fori_loop(0, n_trips - 1, trip, 0)
    j_last = blocks_per_trip * (n_trips - 1)
    for u in range(blocks_per_trip):
        if u + 1 < blocks_per_trip:
            scores(j_last + u + 1, bufs[(u + 1) % 2])
        absorb(j_last + u, bufs[u % 2])

    inv = 1.0 / l_ref[...]
    o = acc_ref[0] * inv[:, 0:tq] - lam_ref[0] * (acc_ref[1] * inv[:, tq:2 * tq])
    ms = jnp.mean(o * o, axis=0, keepdims=True)
    o = o * lax.rsqrt(ms + EPS) * gsub_ref[...] * out_scale
    o_ref[0] = o.T.astype(o_ref.dtype)


def _diff_attn(lam, q, k, v, g_sub_col, lam_init, *, tq, tk, blocks_per_trip):
    B, S, _ = q.shape
    assert blocks_per_trip % 2 == 0 and S % (blocks_per_trip * tk) == 0, (S, tk, blocks_per_trip)
    est = (2 * 2 * S * (HEAD_COLS + DIFF_VDIM) * 2 + 4 * tq * HEAD_COLS * 2 + 2 * tq * HEAD_COLS * 2
           + 2 * tq * DIFF_VDIM * 4 + 8 * tk * 2 * tq * 4 + (4 << 20))
    return pl.pallas_call(
        functools.partial(_diff_attn_kernel, tq=tq, tk=tk, blocks_per_trip=blocks_per_trip,
                          out_scale=1.0 - lam_init),
        grid=(B, N_DIFF_HEADS, S // tq),
        in_specs=[pl.BlockSpec(memory_space=pltpu.SMEM),
                  pl.BlockSpec((1, tq, HEAD_COLS), lambda b, h, i: (b, i, h)),
                  pl.BlockSpec((1, S, HEAD_COLS), lambda b, h, i: (b, 0, h)),
                  pl.BlockSpec((1, S, DIFF_VDIM), lambda b, h, i: (b, 0, h)),
                  pl.BlockSpec((DIFF_VDIM, 1), lambda b, h, i: (0, 0))],
        out_specs=pl.BlockSpec((1, tq, DIFF_VDIM), lambda b, h, i: (b, i, h)),
        out_shape=jax.ShapeDtypeStruct((B, S, N_DIFF_HEADS * DIFF_VDIM), BF16),
        scratch_shapes=[pltpu.VMEM((HEAD_COLS, 2 * tq), BF16),
                        pltpu.VMEM((tk, 2 * tq), F32),
                        pltpu.VMEM((tk, 2 * tq), F32),
                        pltpu.VMEM((1, 2 * tq), F32),
                        pltpu.VMEM((1, 2 * tq), F32),
                        pltpu.VMEM((2, DIFF_VDIM, tq), F32)],
        compiler_params=_params(("parallel", "parallel", "parallel"), est),
        name="diff_attn",
    )(lam, q, k, v, g_sub_col)


def _diff_attn_bounded_kernel(lam_ref, bound_ref, q_ref, k_ref, v_ref, gsub_ref, o_ref, qbd_ref, l_ref, acc_ref,
                              *, tq, tk, blocks_per_trip, out_scale):
    qt = q_ref[0].astype(F32).T
    row = lax.broadcasted_iota(jnp.int32, (HEAD_COLS, 1), 0)
    zero = jnp.zeros_like(qt)
    qbd_ref[:, 0:tq] = jnp.where(row < DIFF_QKDIM, qt, zero).astype(BF16)
    qbd_ref[:, tq:2 * tq] = jnp.where(row >= DIFF_QKDIM, qt, zero).astype(BF16)
    l_ref[...] = jnp.zeros(l_ref.shape, F32)
    acc_ref[...] = jnp.zeros(acc_ref.shape, F32)
    shift = bound_ref[0]

    def trip(i, carry):
        for u in range(blocks_per_trip):
            k0 = pl.multiple_of((blocks_per_trip * i + u) * tk, tk)
            vb = v_ref[0, pl.ds(k0, tk), :]
            p = jnp.exp2(_dot(k_ref[0, pl.ds(k0, tk), :], qbd_ref[...]) - shift)
            l_ref[...] += jnp.sum(p, axis=0, keepdims=True)
            pb = p.astype(BF16)
            for c in range(2):
                acc_ref[c] += _dot_tn(vb, pb[:, c * tq:(c + 1) * tq])
        return carry

    lax.fori_loop(0, k_ref.shape[1] // (blocks_per_trip * tk), trip, 0)

    inv = 1.0 / l_ref[...]
    o = acc_ref[0] * inv[:, 0:tq] - lam_ref[0] * (acc_ref[1] * inv[:, tq:2 * tq])
    ms = jnp.mean(o * o, axis=0, keepdims=True)
    o = o * lax.rsqrt(ms + EPS) * gsub_ref[...] * out_scale
    o_ref[0] = o.T.astype(o_ref.dtype)


def _diff_attn_bounded(lam, bound, q, k, v, g_sub_col, lam_init, *, tq, tk, blocks_per_trip):
    B, S, _ = q.shape
    assert S % (blocks_per_trip * tk) == 0, (S, tk, blocks_per_trip)
    est = (2 * 2 * S * (HEAD_COLS + DIFF_VDIM) * 2 + 4 * tq * HEAD_COLS * 2 + 2 * tq * HEAD_COLS * 2
           + 2 * tq * DIFF_VDIM * 4 + 8 * tk * 2 * tq * 4 + (4 << 20))
    return pl.pallas_call(
        functools.partial(_diff_attn_bounded_kernel, tq=tq, tk=tk, blocks_per_trip=blocks_per_trip,
                          out_scale=1.0 - lam_init),
        grid=(B, N_DIFF_HEADS, S // tq),
        in_specs=[pl.BlockSpec(memory_space=pltpu.SMEM),
                  pl.BlockSpec(memory_space=pltpu.SMEM),
                  pl.BlockSpec((1, tq, HEAD_COLS), lambda b, h, i: (b, i, h)),
                  pl.BlockSpec((1, S, HEAD_COLS), lambda b, h, i: (b, 0, h)),
                  pl.BlockSpec((1, S, DIFF_VDIM), lambda b, h, i: (b, 0, h)),
                  pl.BlockSpec((DIFF_VDIM, 1), lambda b, h, i: (0, 0))],
        out_specs=pl.BlockSpec((1, tq, DIFF_VDIM), lambda b, h, i: (b, i, h)),
        out_shape=jax.ShapeDtypeStruct((B, S, N_DIFF_HEADS * DIFF_VDIM), BF16),
        scratch_shapes=[pltpu.VMEM((HEAD_COLS, 2 * tq), BF16),
                        pltpu.VMEM((1, 2 * tq), F32),
                        pltpu.VMEM((2, DIFF_VDIM, tq), F32)],
        compiler_params=_params(("parallel", "parallel", "parallel"), est),
        name="diff_attn_bounded",
    )(lam, bound, q, k, v, g_sub_col)


def _mix_out_kernel(a_ref, u_ref, up_ref, un_ref, x_ref, wp_ref, ps_ref, wo_ref, o_ref, *, tm, seq):
    i = pl.program_id(1)
    u_prev = jnp.where(i > 0, up_ref[0], 0.0)
    u_next = jnp.where(i < pl.num_programs(1) - 1, un_ref[0], 0.0)
    ue = jnp.concatenate([u_prev, u_ref[0], u_next], axis=0)
    rows = tm + 2 * POOL_HALO
    pos = i * tm + lax.broadcasted_iota(jnp.int32, (tm, 1), 0)
    a_width = a_ref.shape[2]
    acc = x_ref[0] + _dot(a_ref[0], wo_ref[0:a_width, :])
    for g, w in enumerate(POOL_WINDOWS):
        c0 = g * POOL_GROUP_WIDTH
        ug = ue[:, c0:c0 + POOL_GROUP_WIDTH]
        win = ug + pltpu.roll(ug, 1, 0)
        shift = 1
        while 2 * shift < w:
            win = pltpu.roll(win, shift, 0) + pltpu.roll(win, rows - shift, 0)
            shift *= 2
        win = win[POOL_HALO:POOL_HALO + tm]
        cnt = jnp.minimum(pos + w // 2, seq) - jnp.maximum(pos - w // 2, 0)
        z = win / cnt.astype(F32) - ug[POOL_HALO:POOL_HALO + tm]
        pg = _dot(z.astype(BF16), wp_ref[g]) * ps_ref[:, c0:c0 + POOL_GROUP_WIDTH]
        acc = acc + _dot(pg.astype(BF16), wo_ref[a_width + c0:a_width + c0 + POOL_GROUP_WIDTH, :])
    o_ref[0] = acc


def _mix_out(a, u, x, w_pool, pool_scale, w_out, *, tm):
    B, S, D = x.shape
    a_width, u_width = a.shape[2], u.shape[2]
    halo_blocks = tm // POOL_HALO
    n_halo = S // POOL_HALO
    row = lambda b, i: (b, i, 0)
    est = (w_out.size * 2 + w_pool.size * 2 * 2 + 4 * tm * D * 4 + 2 * tm * a_width * 2 + 2 * tm * u_width * 4
           + 6 * tm * u_width * 4 + 2 * tm * D * 4 + (4 << 20))
    return pl.pallas_call(
        functools.partial(_mix_out_kernel, tm=tm, seq=S),
        grid=(B, S // tm),
        in_specs=[pl.BlockSpec((1, tm, a_width), row),
                  pl.BlockSpec((1, tm, u_width), row),
                  pl.BlockSpec((1, POOL_HALO, u_width),
                               lambda b, i: (b, jnp.maximum(i * halo_blocks - 1, 0), 0)),
                  pl.BlockSpec((1, POOL_HALO, u_width),
                               lambda b, i: (b, jnp.minimum((i + 1) * halo_blocks, n_halo - 1), 0)),
                  pl.BlockSpec((1, tm, D), row),
                  pl.BlockSpec(w_pool.shape, lambda b, i: (0, 0, 0)),
                  pl.BlockSpec((1, u_width), lambda b, i: (0, 0)),
                  _resident(w_out.shape)],
        out_specs=pl.BlockSpec((1, tm, D), row),
        out_shape=jax.ShapeDtypeStruct((B, S, D), F32),
        compiler_params=_params(("parallel", "parallel"), est),
        name="mix_out",
    )(a, u, u, u, x, w_pool, pool_scale, w_out)


def _mem_kv_kernel(mem_ref, g_ref, w_ref, gk_ref, o_ref, *, normalise):
    m = _rms(mem_ref[0], g_ref[...]).astype(BF16)
    kv = _dot(m, w_ref[...])
    if normalise:
        kv = _rms(kv, gk_ref[...])
    o_ref[0] = kv.astype(o_ref.dtype)


def _mem_kv(mem, g_mem, wc_kv, gc_k, *, normalise, col_block0):
    B, n_mem, D = mem.shape
    hd = D // N_CROSS_HEADS
    est = 2 * n_mem * D * 4 + 2 * D * hd * 2 + 4 * n_mem * hd * 4 + n_mem * D * 4 + (4 << 20)
    return pl.pallas_call(
        functools.partial(_mem_kv_kernel, normalise=normalise),
        grid=(N_CROSS_HEADS, B),
        in_specs=[pl.BlockSpec((1, n_mem, D), lambda h, b: (b, 0, 0)),
                  pl.BlockSpec((1, D), lambda h, b: (0, 0)),
                  pl.BlockSpec((D, hd), lambda h, b: (0, col_block0 + h)),
                  pl.BlockSpec((1, hd), lambda h, b: (0, 0))],
        out_specs=pl.BlockSpec((1, n_mem, hd), lambda h, b: (b, 0, h)),
        out_shape=jax.ShapeDtypeStruct((B, n_mem, D), BF16),
        compiler_params=_params(("parallel", "parallel"), est),
        name="mem_k" if normalise else "mem_v",
    )(mem, g_mem, wc_kv, gc_k)


def _cross_attn_kernel(x_ref, g_ref, wq_ref, gq_ref, k_ref, v_ref, wo_ref, o_ref):
    x = x_ref[0]
    h = _rms(x, g_ref[...]).astype(BF16)
    q = _dot(h, wq_ref[...])
    hd = gq_ref.shape[1]
    scale = hd ** -0.5
    acc = x
    for c0 in range(0, q.shape[1], hd):
        qn = _rms(q[:, c0:c0 + hd], gq_ref[...]).astype(BF16)
        s = _dot_nt(qn, k_ref[0, :, c0:c0 + hd]) * scale
        p = jnp.exp(s - jnp.max(s, axis=-1, keepdims=True))
        p = p / jnp.sum(p, axis=-1, keepdims=True)
        oh = _dot(p.astype(BF16), v_ref[0, :, c0:c0 + hd])
        acc = acc + _dot(oh.astype(BF16), wo_ref[c0:c0 + hd, :])
    o_ref[0] = acc


def _cross_attn(x, g_cross, wc_q, gc_q, k_mem, v_mem, wc_o, *, tm):
    B, S, D = x.shape
    n_mem = k_mem.shape[1]
    row = lambda b, i: (b, i, 0)
    est = (wc_q.size * 2 + wc_o.size * 2 + 4 * n_mem * D * 2 + 4 * tm * D * 4 + 5 * tm * D * 4 + (4 << 20))
    return pl.pallas_call(
        _cross_attn_kernel,
        grid=(B, S // tm),
        in_specs=[pl.BlockSpec((1, tm, D), row),
                  pl.BlockSpec((1, D), lambda b, i: (0, 0)),
                  _resident(wc_q.shape),
                  pl.BlockSpec(gc_q.shape, lambda b, i: (0, 0)),
                  pl.BlockSpec((1, n_mem, D), lambda b, i: (b, 0, 0)),
                  pl.BlockSpec((1, n_mem, D), lambda b, i: (b, 0, 0)),
                  _resident(wc_o.shape)],
        out_specs=pl.BlockSpec((1, tm, D), row),
        out_shape=jax.ShapeDtypeStruct((B, S, D), F32),
        compiler_params=_params(("parallel", "parallel"), est),
        name="cross_attn",
    )(x, g_cross, wc_q, gc_q, k_mem, v_mem, wc_o)


def _conv_ffn_kernel(x_ref, xp_ref, xn_ref, g_ref, wg_ref, wv_ref, cwg_ref, cwv_ref, cbg_ref, cbv_ref, wd_ref,
                     o_ref, h_ref, act_ref, *, tm):
    i = pl.program_id(1)
    f = pl.program_id(2)
    last = pl.num_programs(2) - 1
    rows = tm + 2 * CONV_HALO

    def conv(w_ref, cw_ref, cb_ref):
        u = _dot(h_ref[...], w_ref[...])
        below = pltpu.roll(u, 1, 0)[CONV_HALO:CONV_HALO + tm]
        above = pltpu.roll(u, rows - 1, 0)[CONV_HALO:CONV_HALO + tm]
        mid = u[CONV_HALO:CONV_HALO + tm]
        return below * cw_ref[0:1, :] + mid * cw_ref[1:2, :] + above * cw_ref[2:3, :] + cb_ref[...]

    def gated_chunk():
        return (jax.nn.gelu(conv(wg_ref, cwg_ref, cbg_ref)) * conv(wv_ref, cwv_ref, cbv_ref)).astype(BF16)

    @pl.when(f == 0)
    def _():
        g = g_ref[...]
        h_prev = jnp.where(i > 0, _rms(xp_ref[0], g), 0.0)
        h_next = jnp.where(i < pl.num_programs(1) - 1, _rms(xn_ref[0], g), 0.0)
        h_ref[0:CONV_HALO, :] = h_prev.astype(BF16)
        h_ref[CONV_HALO:CONV_HALO + tm, :] = _rms(x_ref[0], g).astype(BF16)
        h_ref[CONV_HALO + tm:rows, :] = h_next.astype(BF16)
        o_ref[0] = x_ref[0]
        act_ref[...] = gated_chunk()

    @pl.when(jnp.logical_and(f > 0, f < last))
    def _():
        act = gated_chunk()
        o_ref[0] += _dot(act_ref[...], wd_ref[...])
        act_ref[...] = act

    @pl.when(f == last)
    def _():
        o_ref[0] += _dot(act_ref[...], wd_ref[...])


def _conv_ffn(x, g_ffn, w_up, conv_w, conv_b, w_down, *, tm, tf):
    B, S, D = x.shape
    d_ff = w_down.shape[0]
    nf = d_ff // tf
    halo_blocks = tm // CONV_HALO
    n_halo = S // CONV_HALO
    rows = tm + 2 * CONV_HALO
    est = (4 * tm * D * 4 + 2 * 3 * D * tf * 2 + rows * D * 2 + 8 * rows * tf * 4 + 2 * tm * D * 4 + tm * tf * 2
           + (4 << 20))
    gate = lambda b, i, f: (0, jnp.minimum(f, nf - 1))
    val = lambda b, i, f: (0, nf + jnp.minimum(f, nf - 1))
    return pl.pallas_call(
        functools.partial(_conv_ffn_kernel, tm=tm),
        grid=(B, S // tm, nf + 1),
        in_specs=[pl.BlockSpec((1, tm, D), lambda b, i, f: (b, i, 0)),
                  pl.BlockSpec((1, CONV_HALO, D), lambda b, i, f: (b, jnp.maximum(i * halo_blocks - 1, 0), 0)),
                  pl.BlockSpec((1, CONV_HALO, D),
                               lambda b, i, f: (b, jnp.minimum((i + 1) * halo_blocks, n_halo - 1), 0)),
                  pl.BlockSpec((1, D), lambda b, i, f: (0, 0)),
                  pl.BlockSpec((D, tf), gate),
                  pl.BlockSpec((D, tf), val),
                  pl.BlockSpec((conv_w.shape[0], tf), gate),
                  pl.BlockSpec((conv_w.shape[0], tf), val),
                  pl.BlockSpec((1, tf), gate),
                  pl.BlockSpec((1, tf), val),
                  pl.BlockSpec((tf, D), lambda b, i, f: (jnp.maximum(f - 1, 0), 0))],
        out_specs=pl.BlockSpec((1, tm, D), lambda b, i, f: (b, i, 0)),
        out_shape=jax.ShapeDtypeStruct((B, S, D), F32),
        scratch_shapes=[pltpu.VMEM((rows, D), BF16), pltpu.VMEM((tm, tf), BF16)],
        compiler_params=_params(("parallel", "parallel", "arbitrary"), est),
        name="conv_ffn",
    )(x, x, x, g_ffn, w_up, w_up, conv_w, conv_w, conv_b, conv_b, w_down)


def _rope_tables(seq, gain, scale):
    half = DIFF_QKDIM // 2
    inv = ROPE_THETA ** (-jnp.arange(half, dtype=F32) / half)
    ang = jnp.arange(seq, dtype=F32)[:, None] * inv[None, :]
    cos = jnp.cos(ang)
    sin = jnp.sin(ang)
    gain = gain.astype(F32)
    a = jnp.concatenate([gain[:half] * cos, gain[half:] * cos], axis=1) * scale
    b = jnp.concatenate([-gain[half:] * sin, gain[:half] * sin], axis=1) * scale
    reps = V7X_LANES // DIFF_QKDIM
    return jnp.tile(a, (1, reps)), jnp.tile(b, (1, reps))


def _segment_ones(width):
    seg = jnp.arange(width) // DIFF_QKDIM
    return (seg[:, None] == seg[None, :]).astype(BF16)


def _tile(n, target):
    t = min(n, target)
    assert n % t == 0, (n, t)
    return t


def _run_trunk(x, mem, layers):
    B, S, D = x.shape
    tm = _tile(S, 512)
    seg = _segment_ones(256)
    for l, p in enumerate(layers):
        lam_init = 0.8 - 0.6 * math.exp(-0.3 * l)
        q_scale = DIFF_QKDIM ** -0.5 * math.log2(math.e)
        aq, bq = _rope_tables(S, p["g_q"], q_scale)
        ak, bk = _rope_tables(S, p["g_k"], 1.0)
        q, k, v, u = _mix_in(x, p["g_mix"], p["w_in"], seg, aq, bq, ak, bk, tm=tm)
        bound = (DIFF_QKDIM * q_scale * 1.01 * jnp.max(jnp.abs(p["g_q"])) * jnp.max(jnp.abs(p["g_k"]))).astype(F32)
        attn = functools.partial(_diff_attn, p["lam"], q, k, v, p["g_sub"].reshape(-1, 1), lam_init,
                                 tq=_tile(S, 512), tk=_tile(S // 4, 512), blocks_per_trip=4)
        attn_bounded = functools.partial(_diff_attn_bounded, p["lam"], bound.reshape(1), q, k, v,
                                         p["g_sub"].reshape(-1, 1), lam_init,
                                         tq=_tile(S, 512), tk=_tile(S // 8, 512),
                                         blocks_per_trip=min(16, S // _tile(S // 8, 512)))
        a = lax.cond(bound <= SOFTMAX_SHIFT_LIMIT, attn_bounded, attn)
        x = _mix_out(a, u, x, p["w_pool"], p["pool_scale"], p["w_out"], tm=tm)
        k_mem = _mem_kv(mem, p["g_mem"], p["wc_kv"], p["gc_k"], normalise=True, col_block0=0)
        v_mem = _mem_kv(mem, p["g_mem"], p["wc_kv"], p["gc_k"], normalise=False, col_block0=N_CROSS_HEADS)
        x = _cross_attn(x, p["g_cross"], p["wc_q"], p["gc_q"], k_mem, v_mem, p["wc_o"], tm=tm)
        x = _conv_ffn(x, p["g_ffn"], p["w_up"], p["conv_w"], p["conv_b"], p["w_down"], tm=tm, tf=512)
    return x


def kernel(x_prompt, x_sample, mem_prompt, mem_sample, g_mix, w_in, g_q, g_k, lam_q1, lam_k1, lam_q2, lam_k2,
           g_sub, w_pool, pool_scale, w_out, g_cross, g_mem, wc_q, wc_kv, gc_q, gc_k, wc_o, g_ffn, w_up,
           conv_w, conv_b, w_down):
    depth = w_in.shape[0]
    layers = []
    for l in range(depth):
        lam_init = 0.8 - 0.6 * math.exp(-0.3 * l)
        lam = (jnp.exp(jnp.sum(lam_q1[l].astype(F32) * lam_k1[l].astype(F32)))
               - jnp.exp(jnp.sum(lam_q2[l].astype(F32) * lam_k2[l].astype(F32))) + lam_init)
        row = lambda t: t[l].reshape(1, -1).astype(F32)
        layers.append(dict(
            lam=lam.reshape(1).astype(F32),
            g_mix=row(g_mix), g_q=g_q[l], g_k=g_k[l], g_sub=row(g_sub), pool_scale=row(pool_scale),
            g_cross=row(g_cross), g_mem=row(g_mem), gc_q=row(gc_q), gc_k=row(gc_k), g_ffn=row(g_ffn),
            conv_w=conv_w[l].astype(F32), conv_b=row(conv_b),
            w_in=w_in[l].astype(BF16), w_pool=w_pool[l].astype(BF16), w_out=w_out[l].astype(BF16),
            wc_q=wc_q[l].astype(BF16), wc_kv=wc_kv[l].astype(BF16), wc_o=wc_o[l].astype(BF16),
            w_up=w_up[l].astype(BF16), w_down=w_down[l].astype(BF16)))
    return (_run_trunk(x_prompt, mem_prompt, layers), _run_trunk(x_sample, mem_sample, layers))
```

```python
import functools
import math

import jax
import jax.numpy as jnp
from jax import lax
from jax.experimental import pallas as pl
from jax.experimental.pallas import tpu as pltpu

F32 = jnp.float32
BF16 = jnp.bfloat16

N_DIFF_HEADS = 8
DIFF_QKDIM = 64
DIFF_VDIM = 128
HEAD_COLS = 2 * DIFF_QKDIM
POOL_WINDOWS = (2, 4, 8, 16)
POOL_GROUP_WIDTH = 256
POOL_HALO = 16
N_CROSS_HEADS = 4
CONV_HALO = 8
FFN_CHUNK = 512
ROPE_THETA = 10000.0
EPS = 1e-6
SOFTMAX_SHIFT_LIMIT = 60.0

V7X_VMEM_BYTES = 64 * 1024 * 1024
V7X_LANES = 128
VMEM_LIMIT_CAP = V7X_VMEM_BYTES - 6 * 1024 * 1024


def _vmem_limit(estimate_bytes):
    return int(min(VMEM_LIMIT_CAP, max(32 * 1024 * 1024, estimate_bytes)))


def _params(semantics, vmem_estimate):
    return pltpu.CompilerParams(dimension_semantics=semantics, vmem_limit_bytes=_vmem_limit(vmem_estimate))


def _resident(shape):
    return pl.BlockSpec(shape, lambda *_: (0,) * len(shape), pipeline_mode=pl.Buffered(1))


def _rms(x, gain):
    ms = jnp.mean(x * x, axis=-1, keepdims=True)
    return x * lax.rsqrt(ms + EPS) * gain


def _dot(a, b):
    return jnp.dot(a, b, preferred_element_type=F32)


def _dot_nt(a, b):
    return lax.dot_general(a, b, (((1,), (1,)), ((), ())), preferred_element_type=F32)


def _mix_in_kernel(x_ref, g_ref, w_ref, seg_ref, aq_ref, bq_ref, ak_ref, bk_ref,
                   q_ref, k_ref, v_ref, u_ref, *, qk_width, v_width):
    h = _rms(x_ref[0], g_ref[...]).astype(BF16)
    lane = lax.broadcasted_iota(jnp.int32, (1, V7X_LANES), 1)
    partner_is_above = (lane & (DIFF_QKDIM // 2)) == 0
    seg = seg_ref[...]
    chunk = seg.shape[0]
    for col0, a_ref, b_ref, o_ref in ((0, aq_ref, bq_ref, q_ref), (qk_width, ak_ref, bk_ref, k_ref)):
        a = a_ref[...]
        b = b_ref[...]
        zfull = _dot(h, w_ref[:, col0:col0 + qk_width])
        for c0 in range(0, qk_width, chunk):
            z = zfull[:, c0:c0 + chunk]
            ss = _dot((z * z).astype(BF16), seg)
            zn = z * lax.rsqrt(ss * (1.0 / DIFF_QKDIM) + EPS)
            for c in range(0, chunk, V7X_LANES):
                zc = zn[:, c:c + V7X_LANES]
                partner = jnp.where(partner_is_above,
                                    pltpu.roll(zc, V7X_LANES - DIFF_QKDIM // 2, 1),
                                    pltpu.roll(zc, DIFF_QKDIM // 2, 1))
                o_ref[0, :, c0 + c:c0 + c + V7X_LANES] = (zc * a + partner * b).astype(o_ref.dtype)
    v0 = 2 * qk_width
    v_ref[0] = _dot(h, w_ref[:, v0:v0 + v_width]).astype(v_ref.dtype)
    u_ref[0] = _dot(h, w_ref[:, v0 + v_width:])


def _mix_in(x, g, w_in, seg, aq, bq, ak, bk, *, tm):
    B, S, D = x.shape
    qk_width = N_DIFF_HEADS * HEAD_COLS
    v_width = N_DIFF_HEADS * DIFF_VDIM
    u_width = w_in.shape[1] - 2 * qk_width - v_width
    row = lambda b, i: (b, i, 0)
    tab = pl.BlockSpec((tm, V7X_LANES), lambda b, i: (i, 0))
    est = (w_in.size * 2 + 2 * tm * D * 4 + 2 * tm * (2 * qk_width + v_width) * 2 + 2 * tm * u_width * 4
           + tm * D * 2 + 8 * tm * 1024 * 4 + 8 * tm * V7X_LANES * 4 + (4 << 20))
    return pl.pallas_call(
        functools.partial(_mix_in_kernel, qk_width=qk_width, v_width=v_width),
        grid=(B, S // tm),
        in_specs=[pl.BlockSpec((1, tm, D), row),
                  pl.BlockSpec((1, D), lambda b, i: (0, 0)),
                  _resident(w_in.shape),
                  pl.BlockSpec(seg.shape, lambda b, i: (0, 0)),
                  tab, tab, tab, tab],
        out_specs=[pl.BlockSpec((1, tm, qk_width), row),
                   pl.BlockSpec((1, tm, qk_width), row),
                   pl.BlockSpec((1, tm, v_width), row),
                   pl.BlockSpec((1, tm, u_width), row)],
        out_shape=[jax.ShapeDtypeStruct((B, S, qk_width), BF16),
                   jax.ShapeDtypeStruct((B, S, qk_width), BF16),
                   jax.ShapeDtypeStruct((B, S, v_width), BF16),
                   jax.ShapeDtypeStruct((B, S, u_width), F32)],
        compiler_params=_params(("parallel", "parallel"), est),
        name="mix_in",
    )(x, g, w_in, seg, aq, bq, ak, bk)


def _dot_tn(a, b):
    return lax.dot_general(a, b, (((0,), (0,)), ((), ())), preferred_element_type=F32)


def _diff_attn_kernel(lam_ref, q_ref, k_ref, v_ref, gsub_ref, o_ref, qbd_ref, s0_ref, s1_ref, m_ref, l_ref, acc_ref,
                      *, tq, tk, blocks_per_trip, out_scale):
    qt = q_ref[0].astype(F32).T
    row = lax.broadcasted_iota(jnp.int32, (HEAD_COLS, 1), 0)
    zero = jnp.zeros_like(qt)
    qbd_ref[:, 0:tq] = jnp.where(row < DIFF_QKDIM, qt, zero).astype(BF16)
    qbd_ref[:, tq:2 * tq] = jnp.where(row >= DIFF_QKDIM, qt, zero).astype(BF16)
    m_ref[...] = jnp.full(m_ref.shape, -jnp.inf, F32)
    l_ref[...] = jnp.zeros(l_ref.shape, F32)
    acc_ref[...] = jnp.zeros(acc_ref.shape, F32)

    def scores(j, s_ref):
        k0 = pl.multiple_of(j * tk, tk)
        s_ref[...] = _dot(k_ref[0, pl.ds(k0, tk), :], qbd_ref[...])

    def absorb(j, s_ref):
        k0 = pl.multiple_of(j * tk, tk)
        vb = v_ref[0, pl.ds(k0, tk), :]
        s = s_ref[...]
        m_prev = m_ref[...]
        m_new = jnp.maximum(m_prev, jnp.max(s, axis=0, keepdims=True))
        alpha = jnp.exp2(m_prev - m_new)
        p = jnp.exp2(s - m_new)
        l_ref[...] = alpha * l_ref[...] + jnp.sum(p, axis=0, keepdims=True)
        m_ref[...] = m_new
        pb = p.astype(BF16)
        for c in range(2):
            cols = slice(c * tq, (c + 1) * tq)
            acc_ref[c] = alpha[:, cols] * acc_ref[c] + _dot_tn(vb, pb[:, cols])

    bufs = (s0_ref, s1_ref)
    n_trips = k_ref.shape[1] // (blocks_per_trip * tk)
    scores(0, s0_ref)

    def trip(i, carry):
        j = blocks_per_trip * i
        for u in range(blocks_per_trip):
            scores(j + u + 1, bufs[(u + 1) % 2])
            absorb(j + u, bufs[u % 2])
        return carry

    lax.fori_loop(0, n_trips - 1, trip, 0)
    j_last = blocks_per_trip * (n_trips - 1)
    for u in range(blocks_per_trip):
        if u + 1 < blocks_per_trip:
            scores(j_last + u + 1, bufs[(u + 1) % 2])
        absorb(j_last + u, bufs[u % 2])

    inv = 1.0 / l_ref[...]
    o = acc_ref[0] * inv[:, 0:tq] - lam_ref[0] * (acc_ref[1] * inv[:, tq:2 * tq])
    ms = jnp.mean(o * o, axis=0, keepdims=True)
    o = o * lax.rsqrt(ms + EPS) * gsub_ref[...] * out_scale
    o_ref[0] = o.T.astype(o_ref.dtype)


def _diff_attn(lam, q, k, v, g_sub_col, lam_init, *, tq, tk, blocks_per_trip):
    B, S, _ = q.shape
    assert blocks_per_trip % 2 == 0 and S % (blocks_per_trip * tk) == 0, (S, tk, blocks_per_trip)
    est = (2 * 2 * S * (HEAD_COLS + DIFF_VDIM) * 2 + 4 * tq * HEAD_COLS * 2 + 2 * tq * HEAD_COLS * 2
           + 2 * tq * DIFF_VDIM * 4 + 8 * tk * 2 * tq * 4 + (4 << 20))
    return pl.pallas_call(
        functools.partial(_diff_attn_kernel, tq=tq, tk=tk, blocks_per_trip=blocks_per_trip,
                          out_scale=1.0 - lam_init),
        grid=(B, N_DIFF_HEADS, S // tq),
        in_specs=[pl.BlockSpec(memory_space=pltpu.SMEM),
                  pl.BlockSpec((1, tq, HEAD_COLS), lambda b, h, i: (b, i, h)),
                  pl.BlockSpec((1, S, HEAD_COLS), lambda b, h, i: (b, 0, h)),
                  pl.BlockSpec((1, S, DIFF_VDIM), lambda b, h, i: (b, 0, h)),
                  pl.BlockSpec((DIFF_VDIM, 1), lambda b, h, i: (0, 0))],
        out_specs=pl.BlockSpec((1, tq, DIFF_VDIM), lambda b, h, i: (b, i, h)),
        out_shape=jax.ShapeDtypeStruct((B, S, N_DIFF_HEADS * DIFF_VDIM), BF16),
        scratch_shapes=[pltpu.VMEM((HEAD_COLS, 2 * tq), BF16),
                        pltpu.VMEM((tk, 2 * tq), F32),
                        pltpu.VMEM((tk, 2 * tq), F32),
                        pltpu.VMEM((1, 2 * tq), F32),
                        pltpu.VMEM((1, 2 * tq), F32),
                        pltpu.VMEM((2, DIFF_VDIM, tq), F32)],
        compiler_params=_params(("parallel", "parallel", "parallel"), est),
        name="diff_attn",
    )(lam, q, k, v, g_sub_col)


def _diff_attn_bounded_kernel(lam_ref, bound_ref, q_ref, k_ref, v_ref, gsub_ref, o_ref, qbd_ref, l_ref, acc_ref,
                              *, tq, tk, blocks_per_trip, out_scale):
    qt = q_ref[0].astype(F32).T
    row = lax.broadcasted_iota(jnp.int32, (HEAD_COLS, 1), 0)
    zero = jnp.zeros_like(qt)
    qbd_ref[:, 0:tq] = jnp.where(row < DIFF_QKDIM, qt, zero).astype(BF16)
    qbd_ref[:, tq:2 * tq] = jnp.where(row >= DIFF_QKDIM, qt, zero).astype(BF16)
    l_ref[...] = jnp.zeros(l_ref.shape, F32)
    acc_ref[...] = jnp.zeros(acc_ref.shape, F32)
    shift = bound_ref[0]

    def trip(i, carry):
        for u in range(blocks_per_trip):
            k0 = pl.multiple_of((blocks_per_trip * i + u) * tk, tk)
            vb = v_ref[0, pl.ds(k0, tk), :]
            p = jnp.exp2(_dot(k_ref[0, pl.ds(k0, tk), :], qbd_ref[...]) - shift)
            l_ref[...] += jnp.sum(p, axis=0, keepdims=True)
            pb = p.astype(BF16)
            for c in range(2):
                acc_ref[c] += _dot_tn(vb, pb[:, c * tq:(c + 1) * tq])
        return carry

    lax.fori_loop(0, k_ref.shape[1] // (blocks_per_trip * tk), trip, 0)

    inv = 1.0 / l_ref[...]
    o = acc_ref[0] * inv[:, 0:tq] - lam_ref[0] * (acc_ref[1] * inv[:, tq:2 * tq])
    ms = jnp.mean(o * o, axis=0, keepdims=True)
    o = o * lax.rsqrt(ms + EPS) * gsub_ref[...] * out_scale
    o_ref[0] = o.T.astype(o_ref.dtype)


def _diff_attn_bounded(lam, bound, q, k, v, g_sub_col, lam_init, *, tq, tk, blocks_per_trip):
    B, S, _ = q.shape
    assert S % (blocks_per_trip * tk) == 0, (S, tk, blocks_per_trip)
    est = (2 * 2 * S * (HEAD_COLS + DIFF_VDIM) * 2 + 4 * tq * HEAD_COLS * 2 + 2 * tq * HEAD_COLS * 2
           + 2 * tq * DIFF_VDIM * 4 + 8 * tk * 2 * tq * 4 + (4 << 20))
    return pl.pallas_call(
        functools.partial(_diff_attn_bounded_kernel, tq=tq, tk=tk, blocks_per_trip=blocks_per_trip,
                          out_scale=1.0 - lam_init),
        grid=(B, N_DIFF_HEADS, S // tq),
        in_specs=[pl.BlockSpec(memory_space=pltpu.SMEM),
                  pl.BlockSpec(memory_space=pltpu.SMEM),
                  pl.BlockSpec((1, tq, HEAD_COLS), lambda b, h, i: (b, i, h)),
                  pl.BlockSpec((1, S, HEAD_COLS), lambda b, h, i: (b, 0, h)),
                  pl.BlockSpec((1, S, DIFF_VDIM), lambda b, h, i: (b, 0, h)),
                  pl.BlockSpec((DIFF_VDIM, 1), lambda b, h, i: (0, 0))],
        out_specs=pl.BlockSpec((1, tq, DIFF_VDIM), lambda b, h, i: (b, i, h)),
        out_shape=jax.ShapeDtypeStruct((B, S, N_DIFF_HEADS * DIFF_VDIM), BF16),
        scratch_shapes=[pltpu.VMEM((HEAD_COLS, 2 * tq), BF16),
                        pltpu.VMEM((1, 2 * tq), F32),
                        pltpu.VMEM((2, DIFF_VDIM, tq), F32)],
        compiler_params=_params(("parallel", "parallel", "parallel"), est),
        name="diff_attn_bounded",
    )(lam, bound, q, k, v, g_sub_col)


def _mix_out_kernel(a_ref, u_ref, up_ref, un_ref, x_ref, wp_ref, ps_ref, wo_ref, o_ref, *, tm, seq):
    i = pl.program_id(1)
    u_prev = jnp.where(i > 0, up_ref[0], 0.0)
    u_next = jnp.where(i < pl.num_programs(1) - 1, un_ref[0], 0.0)
    ue = jnp.concatenate([u_prev, u_ref[0], u_next], axis=0)
    rows = tm + 2 * POOL_HALO
    pos = i * tm + lax.broadcasted_iota(jnp.int32, (tm, 1), 0)
    a_width = a_ref.shape[2]
    acc = x_ref[0] + _dot(a_ref[0], wo_ref[0:a_width, :])
    pooled = []
    for g, w in enumerate(POOL_WINDOWS):
        c0 = g * POOL_GROUP_WIDTH
        ug = ue[:, c0:c0 + POOL_GROUP_WIDTH]
        win = ug + pltpu.roll(ug, 1, 0)
        shift = 1
        while 2 * shift < w:
            win = pltpu.roll(win, shift, 0) + pltpu.roll(win, rows - shift, 0)
            shift *= 2
        win = win[POOL_HALO:POOL_HALO + tm]
        cnt = jnp.minimum(pos + w // 2, seq) - jnp.maximum(pos - w // 2, 0)
        z = win / cnt.astype(F32) - ug[POOL_HALO:POOL_HALO + tm]
        pg = _dot(z.astype(BF16), wp_ref[g]) * ps_ref[:, c0:c0 + POOL_GROUP_WIDTH]
        pooled.append(pg.astype(BF16))
    o_ref[0] = acc + _dot(jnp.concatenate(pooled, axis=1), wo_ref[a_width:, :])


def _mix_out(a, u, x, w_pool, pool_scale, w_out, *, tm):
    B, S, D = x.shape
    a_width, u_width = a.shape[2], u.shape[2]
    halo_blocks = tm // POOL_HALO
    n_halo = S // POOL_HALO
    row = lambda b, i: (b, i, 0)
    est = (w_out.size * 2 + w_pool.size * 2 * 2 + 4 * tm * D * 4 + 2 * tm * a_width * 2 + 2 * tm * u_width * 4
           + 6 * tm * u_width * 4 + 2 * tm * D * 4 + (4 << 20))
    return pl.pallas_call(
        functools.partial(_mix_out_kernel, tm=tm, seq=S),
        grid=(B, S // tm),
        in_specs=[pl.BlockSpec((1, tm, a_width), row),
                  pl.BlockSpec((1, tm, u_width), row),
                  pl.BlockSpec((1, POOL_HALO, u_width),
                               lambda b, i: (b, jnp.maximum(i * halo_blocks - 1, 0), 0)),
                  pl.BlockSpec((1, POOL_HALO, u_width),
                               lambda b, i: (b, jnp.minimum((i + 1) * halo_blocks, n_halo - 1), 0)),
                  pl.BlockSpec((1, tm, D), row),
                  pl.BlockSpec(w_pool.shape, lambda b, i: (0, 0, 0)),
                  pl.BlockSpec((1, u_width), lambda b, i: (0, 0)),
                  _resident(w_out.shape)],
        out_specs=pl.BlockSpec((1, tm, D), row),
        out_shape=jax.ShapeDtypeStruct((B, S, D), F32),
        compiler_params=_params(("parallel", "parallel"), est),
        name="mix_out",
    )(a, u, u, u, x, w_pool, pool_scale, w_out)


def _mem_kv_kernel(mem_ref, g_ref, w_ref, gk_ref, o_ref, *, normalise):
    m = _rms(mem_ref[0], g_ref[...]).astype(BF16)
    kv = _dot(m, w_ref[...])
    if normalise:
        kv = _rms(kv, gk_ref[...])
    o_ref[0] = kv.astype(o_ref.dtype)


def _mem_kv(mem, g_mem, wc_kv, gc_k, *, normalise, col_block0):
    B, n_mem, D = mem.shape
    hd = D // N_CROSS_HEADS
    est = 2 * n_mem * D * 4 + 2 * D * hd * 2 + 4 * n_mem * hd * 4 + n_mem * D * 4 + (4 << 20)
    return pl.pallas_call(
        functools.partial(_mem_kv_kernel, normalise=normalise),
        grid=(N_CROSS_HEADS, B),
        in_specs=[pl.BlockSpec((1, n_mem, D), lambda h, b: (b, 0, 0)),
                  pl.BlockSpec((1, D), lambda h, b: (0, 0)),
                  pl.BlockSpec((D, hd), lambda h, b: (0, col_block0 + h)),
                  pl.BlockSpec((1, hd), lambda h, b: (0, 0))],
        out_specs=pl.BlockSpec((1, n_mem, hd), lambda h, b: (b, 0, h)),
        out_shape=jax.ShapeDtypeStruct((B, n_mem, D), BF16),
        compiler_params=_params(("parallel", "parallel"), est),
        name="mem_k" if normalise else "mem_v",
    )(mem, g_mem, wc_kv, gc_k)


def _cross_attn_kernel(x_ref, g_ref, wq_ref, gq_ref, k_ref, v_ref, wo_ref, o_ref):
    x = x_ref[0]
    h = _rms(x, g_ref[...]).astype(BF16)
    hd = gq_ref.shape[1]
    scale = hd ** -0.5
    heads = [slice(c0, c0 + hd) for c0 in range(0, x.shape[1], hd)]
    qs = [_dot(h, wq_ref[:, c]) for c in heads]
    qn = [_rms(q, gq_ref[...]).astype(BF16) for q in qs]
    ss = [_dot_nt(q, k_ref[0, :, c]) * scale for q, c in zip(qn, heads)]
    ps = []
    for s in ss:
        p = jnp.exp(s - jnp.max(s, axis=-1, keepdims=True))
        ps.append((p / jnp.sum(p, axis=-1, keepdims=True)).astype(BF16))
    os_ = [_dot(p, v_ref[0, :, c]).astype(BF16) for p, c in zip(ps, heads)]
    acc = x
    for o, c in zip(os_, heads):
        acc = acc + _dot(o, wo_ref[c, :])
    o_ref[0] = acc


def _cross_attn(x, g_cross, wc_q, gc_q, k_mem, v_mem, wc_o, *, tm):
    B, S, D = x.shape
    n_mem = k_mem.shape[1]
    row = lambda b, i: (b, i, 0)
    est = (wc_q.size * 2 + wc_o.size * 2 + 4 * n_mem * D * 2 + 4 * tm * D * 4 + 5 * tm * D * 4 + (4 << 20))
    return pl.pallas_call(
        _cross_attn_kernel,
        grid=(B, S // tm),
        in_specs=[pl.BlockSpec((1, tm, D), row),
                  pl.BlockSpec((1, D), lambda b, i: (0, 0)),
                  _resident(wc_q.shape),
                  pl.BlockSpec(gc_q.shape, lambda b, i: (0, 0)),
                  pl.BlockSpec((1, n_mem, D), lambda b, i: (b, 0, 0)),
                  pl.BlockSpec((1, n_mem, D), lambda b, i: (b, 0, 0)),
                  _resident(wc_o.shape)],
        out_specs=pl.BlockSpec((1, tm, D), row),
        out_shape=jax.ShapeDtypeStruct((B, S, D), F32),
        compiler_params=_params(("parallel", "parallel"), est),
        name="cross_attn",
    )(x, g_cross, wc_q, gc_q, k_mem, v_mem, wc_o)


def _conv_ffn_kernel(x_ref, xp_ref, xn_ref, g_ref, wu_ref, cw_ref, cb_ref, wd_ref, o_ref, h_ref, *, tm):
    i = pl.program_id(1)
    f = pl.program_id(2)
    rows = tm + 2 * CONV_HALO

    @pl.when(f == 0)
    def _():
        g = g_ref[...]
        h_prev = jnp.where(i > 0, _rms(xp_ref[0], g), 0.0)
        h_next = jnp.where(i < pl.num_programs(1) - 1, _rms(xn_ref[0], g), 0.0)
        h_ref[0:CONV_HALO, :] = h_prev.astype(BF16)
        h_ref[CONV_HALO:CONV_HALO + tm, :] = _rms(x_ref[0], g).astype(BF16)
        h_ref[CONV_HALO + tm:rows, :] = h_next.astype(BF16)
        o_ref[0] = x_ref[0]

    u = _dot(h_ref[...], wu_ref[...])
    below = pltpu.roll(u, 1, 0)[CONV_HALO:CONV_HALO + tm]
    above = pltpu.roll(u, rows - 1, 0)[CONV_HALO:CONV_HALO + tm]
    mid = u[CONV_HALO:CONV_HALO + tm]
    c = below * cw_ref[0:1, :] + mid * cw_ref[1:2, :] + above * cw_ref[2:3, :] + cb_ref[...]
    tf = wd_ref.shape[0]
    act = jax.nn.gelu(c[:, 0:tf]) * c[:, tf:2 * tf]
    o_ref[0] += _dot(act.astype(BF16), wd_ref[...])


def _conv_ffn(x, g_ffn, w_up, conv_w, conv_b, w_down, *, tm, tf):
    B, S, D = x.shape
    d_ff = w_down.shape[0]
    nf = d_ff // tf
    halo_blocks = tm // CONV_HALO
    n_halo = S // CONV_HALO
    rows = tm + 2 * CONV_HALO
    est = (4 * tm * D * 4 + 2 * 3 * D * tf * 2 + rows * D * 2 + 8 * rows * tf * 4 + 2 * tm * D * 4 + (4 << 20))
    chunk = lambda b, i, f: (0, f)
    return pl.pallas_call(
        functools.partial(_conv_ffn_kernel, tm=tm),
        grid=(B, S // tm, nf),
        in_specs=[pl.BlockSpec((1, tm, D), lambda b, i, f: (b, i, 0)),
                  pl.BlockSpec((1, CONV_HALO, D), lambda b, i, f: (b, jnp.maximum(i * halo_blocks - 1, 0), 0)),
                  pl.BlockSpec((1, CONV_HALO, D),
                               lambda b, i, f: (b, jnp.minimum((i + 1) * halo_blocks, n_halo - 1), 0)),
                  pl.BlockSpec((1, D), lambda b, i, f: (0, 0)),
                  pl.BlockSpec((D, 2 * tf), chunk),
                  pl.BlockSpec((conv_w.shape[0], 2 * tf), chunk),
                  pl.BlockSpec((1, 2 * tf), chunk),
                  pl.BlockSpec((tf, D), lambda b, i, f: (f, 0))],
        out_specs=pl.BlockSpec((1, tm, D), lambda b, i, f: (b, i, 0)),
        out_shape=jax.ShapeDtypeStruct((B, S, D), F32),
        scratch_shapes=[pltpu.VMEM((rows, D), BF16)],
        compiler_params=_params(("parallel", "parallel", "arbitrary"), est),
        name="conv_ffn",
    )(x, x, x, g_ffn, w_up, conv_w, conv_b, w_down)


def _rope_tables(seq, gain, scale):
    half = DIFF_QKDIM // 2
    inv = ROPE_THETA ** (-jnp.arange(half, dtype=F32) / half)
    ang = jnp.arange(seq, dtype=F32)[:, None] * inv[None, :]
    cos = jnp.cos(ang)
    sin = jnp.sin(ang)
    gain = gain.astype(F32)
    a = jnp.concatenate([gain[:half] * cos, gain[half:] * cos], axis=1) * scale
    b = jnp.concatenate([-gain[half:] * sin, gain[:half] * sin], axis=1) * scale
    reps = V7X_LANES // DIFF_QKDIM
    return jnp.tile(a, (1, reps)), jnp.tile(b, (1, reps))


def _segment_ones(width):
    seg = jnp.arange(width) // DIFF_QKDIM
    return (seg[:, None] == seg[None, :]).astype(BF16)


def _tile(n, target):
    t = min(n, target)
    assert n % t == 0, (n, t)
    return t


def _run_trunk(x, mem, layers):
    B, S, D = x.shape
    tm = _tile(S, 512)
    seg = _segment_ones(256)
    for l, p in enumerate(layers):
        lam_init = 0.8 - 0.6 * math.exp(-0.3 * l)
        q_scale = DIFF_QKDIM ** -0.5 * math.log2(math.e)
        aq, bq = _rope_tables(S, p["g_q"], q_scale)
        ak, bk = _rope_tables(S, p["g_k"], 1.0)
        q, k, v, u = _mix_in(x, p["g_mix"], p["w_in"], seg, aq, bq, ak, bk, tm=tm)
        bound = (DIFF_QKDIM * q_scale * 1.01 * jnp.max(jnp.abs(p["g_q"])) * jnp.max(jnp.abs(p["g_k"]))).astype(F32)
        attn = functools.partial(_diff_attn, p["lam"], q, k, v, p["g_sub"].reshape(-1, 1), lam_init,
                                 tq=_tile(S, 512), tk=_tile(S // 4, 512), blocks_per_trip=4)
        attn_bounded = functools.partial(_diff_attn_bounded, p["lam"], bound.reshape(1), q, k, v,
                                         p["g_sub"].reshape(-1, 1), lam_init,
                                         tq=_tile(S, 512), tk=_tile(S // 8, 512),
                                         blocks_per_trip=min(16, S // _tile(S // 8, 512)))
        a = lax.cond(bound <= SOFTMAX_SHIFT_LIMIT, attn_bounded, attn)
        x = _mix_out(a, u, x, p["w_pool"], p["pool_scale"], p["w_out"], tm=tm)
        k_mem = _mem_kv(mem, p["g_mem"], p["wc_kv"], p["gc_k"], normalise=True, col_block0=0)
        v_mem = _mem_kv(mem, p["g_mem"], p["wc_kv"], p["gc_k"], normalise=False, col_block0=N_CROSS_HEADS)
        x = _cross_attn(x, p["g_cross"], p["wc_q"], p["gc_q"], k_mem, v_mem, p["wc_o"], tm=tm)
        x = _conv_ffn(x, p["g_ffn"], p["w_up"], p["conv_w"], p["conv_b"], p["w_down"], tm=tm, tf=FFN_CHUNK)
    return x


def kernel(x_prompt, x_sample, mem_prompt, mem_sample, g_mix, w_in, g_q, g_k, lam_q1, lam_k1, lam_q2, lam_k2,
           g_sub, w_pool, pool_scale, w_out, g_cross, g_mem, wc_q, wc_kv, gc_q, gc_k, wc_o, g_ffn, w_up,
           conv_w, conv_b, w_down):
    depth = w_in.shape[0]
    layers = []
    for l in range(depth):
        lam_init = 0.8 - 0.6 * math.exp(-0.3 * l)
        lam = (jnp.exp(jnp.sum(lam_q1[l].astype(F32) * lam_k1[l].astype(F32)))
               - jnp.exp(jnp.sum(lam_q2[l].astype(F32) * lam_k2[l].astype(F32))) + lam_init)
        row = lambda t: t[l].reshape(1, -1).astype(F32)

        def chunked(t):
            lead = t.shape[0]
            return t.reshape(lead, 2, -1, FFN_CHUNK).transpose(0, 2, 1, 3).reshape(lead, -1)

        layers.append(dict(
            lam=lam.reshape(1).astype(F32),
            g_mix=row(g_mix), g_q=g_q[l], g_k=g_k[l], g_sub=row(g_sub), pool_scale=row(pool_scale),
            g_cross=row(g_cross), g_mem=row(g_mem), gc_q=row(gc_q), gc_k=row(gc_k), g_ffn=row(g_ffn),
            conv_w=chunked(conv_w[l].astype(F32)), conv_b=chunked(row(conv_b)),
            w_in=w_in[l].astype(BF16), w_pool=w_pool[l].astype(BF16), w_out=w_out[l].astype(BF16),
            wc_q=wc_q[l].astype(BF16), wc_kv=wc_kv[l].astype(BF16), wc_o=wc_o[l].astype(BF16),
            w_up=chunked(w_up[l].astype(BF16)), w_down=w_down[l].astype(BF16)))
    return (_run_trunk(x_prompt, mem_prompt, layers), _run_trunk(x_sample, mem_sample, layers))
```

```python
import functools
import math

import jax
import jax.numpy as jnp
from jax import lax
from jax.experimental import pallas as pl
from jax.experimental.pallas import tpu as pltpu

F32 = jnp.float32
BF16 = jnp.bfloat16

N_DIFF_HEADS = 8
DIFF_QKDIM = 64
DIFF_VDIM = 128
HEAD_COLS = 2 * DIFF_QKDIM
POOL_WINDOWS = (2, 4, 8, 16)
POOL_GROUP_WIDTH = 256
POOL_HALO = 16
N_CROSS_HEADS = 4
CONV_HALO = 8
FFN_CHUNK = 512
ROPE_THETA = 10000.0
EPS = 1e-6
SOFTMAX_SHIFT_LIMIT = 60.0

V7X_VMEM_BYTES = 64 * 1024 * 1024
V7X_LANES = 128
VMEM_LIMIT_CAP = V7X_VMEM_BYTES - 6 * 1024 * 1024


def _vmem_limit(estimate_bytes):
    return int(min(VMEM_LIMIT_CAP, max(32 * 1024 * 1024, estimate_bytes)))


def _params(semantics, vmem_estimate):
    return pltpu.CompilerParams(dimension_semantics=semantics, vmem_limit_bytes=_vmem_limit(vmem_estimate))


def _resident(shape):
    return pl.BlockSpec(shape, lambda *_: (0,) * len(shape), pipeline_mode=pl.Buffered(1))


def _rms(x, gain):
    ms = jnp.mean(x * x, axis=-1, keepdims=True)
    return x * lax.rsqrt(ms + EPS) * gain


def _dot(a, b):
    return jnp.dot(a, b, preferred_element_type=F32)


def _dot_nt(a, b):
    return lax.dot_general(a, b, (((1,), (1,)), ((), ())), preferred_element_type=F32)


def _mix_in_kernel(x_ref, g_ref, w_ref, seg_ref, aq_ref, bq_ref, ak_ref, bk_ref,
                   q_ref, k_ref, v_ref, u_ref, *, qk_width, v_width):
    h = _rms(x_ref[0], g_ref[...]).astype(BF16)
    lane = lax.broadcasted_iota(jnp.int32, (1, V7X_LANES), 1)
    partner_is_above = (lane & (DIFF_QKDIM // 2)) == 0
    seg = seg_ref[...]
    chunk = seg.shape[0]
    for col0, a_ref, b_ref, o_ref in ((0, aq_ref, bq_ref, q_ref), (qk_width, ak_ref, bk_ref, k_ref)):
        a = a_ref[...]
        b = b_ref[...]
        zfull = _dot(h, w_ref[:, col0:col0 + qk_width])
        for c0 in range(0, qk_width, chunk):
            z = zfull[:, c0:c0 + chunk]
            ss = _dot((z * z).astype(BF16), seg)
            zn = z * lax.rsqrt(ss * (1.0 / DIFF_QKDIM) + EPS)
            for c in range(0, chunk, V7X_LANES):
                zc = zn[:, c:c + V7X_LANES]
                partner = jnp.where(partner_is_above,
                                    pltpu.roll(zc, V7X_LANES - DIFF_QKDIM // 2, 1),
                                    pltpu.roll(zc, DIFF_QKDIM // 2, 1))
                o_ref[0, :, c0 + c:c0 + c + V7X_LANES] = (zc * a + partner * b).astype(o_ref.dtype)
    v0 = 2 * qk_width
    v_ref[0] = _dot(h, w_ref[:, v0:v0 + v_width]).astype(v_ref.dtype)
    u_ref[0] = _dot(h, w_ref[:, v0 + v_width:])


def _mix_in(x, g, w_in, seg, aq, bq, ak, bk, *, tm):
    B, S, D = x.shape
    qk_width = N_DIFF_HEADS * HEAD_COLS
    v_width = N_DIFF_HEADS * DIFF_VDIM
    u_width = w_in.shape[1] - 2 * qk_width - v_width
    row = lambda b, i: (b, i, 0)
    tab = pl.BlockSpec((tm, V7X_LANES), lambda b, i: (i, 0))
    est = (w_in.size * 2 + 2 * tm * D * 4 + 2 * tm * (2 * qk_width + v_width) * 2 + 2 * tm * u_width * 4
           + tm * D * 2 + 8 * tm * 1024 * 4 + 8 * tm * V7X_LANES * 4 + (4 << 20))
    return pl.pallas_call(
        functools.partial(_mix_in_kernel, qk_width=qk_width, v_width=v_width),
        grid=(B, S // tm),
        in_specs=[pl.BlockSpec((1, tm, D), row),
                  pl.BlockSpec((1, D), lambda b, i: (0, 0)),
                  _resident(w_in.shape),
                  pl.BlockSpec(seg.shape, lambda b, i: (0, 0)),
                  tab, tab, tab, tab],
        out_specs=[pl.BlockSpec((1, tm, qk_width), row),
                   pl.BlockSpec((1, tm, qk_width), row),
                   pl.BlockSpec((1, tm, v_width), row),
                   pl.BlockSpec((1, tm, u_width), row)],
        out_shape=[jax.ShapeDtypeStruct((B, S, qk_width), BF16),
                   jax.ShapeDtypeStruct((B, S, qk_width), BF16),
                   jax.ShapeDtypeStruct((B, S, v_width), BF16),
                   jax.ShapeDtypeStruct((B, S, u_width), F32)],
        compiler_params=_params(("parallel", "parallel"), est),
        name="mix_in",
    )(x, g, w_in, seg, aq, bq, ak, bk)


def _dot_tn(a, b):
    return lax.dot_general(a, b, (((0,), (0,)), ((), ())), preferred_element_type=F32)


def _diff_attn_kernel(lam_ref, q_ref, k_ref, v_ref, gsub_ref, o_ref, qbd_ref, s0_ref, s1_ref, m_ref, l_ref, acc_ref,
                      *, tq, tk, blocks_per_trip, out_scale):
    qt = q_ref[0].astype(F32).T
    row = lax.broadcasted_iota(jnp.int32, (HEAD_COLS, 1), 0)
    zero = jnp.zeros_like(qt)
    qbd_ref[:, 0:tq] = jnp.where(row < DIFF_QKDIM, qt, zero).astype(BF16)
    qbd_ref[:, tq:2 * tq] = jnp.where(row >= DIFF_QKDIM, qt, zero).astype(BF16)
    m_ref[...] = jnp.full(m_ref.shape, -jnp.inf, F32)
    l_ref[...] = jnp.zeros(l_ref.shape, F32)
    acc_ref[...] = jnp.zeros(acc_ref.shape, F32)

    def scores(j, s_ref):
        k0 = pl.multiple_of(j * tk, tk)
        s_ref[...] = _dot(k_ref[0, pl.ds(k0, tk), :], qbd_ref[...])

    def absorb(j, s_ref):
        k0 = pl.multiple_of(j * tk, tk)
        vb = v_ref[0, pl.ds(k0, tk), :]
        s = s_ref[...]
        m_prev = m_ref[...]
        m_new = jnp.maximum(m_prev, jnp.max(s, axis=0, keepdims=True))
        alpha = jnp.exp2(m_prev - m_new)
        p = jnp.exp2(s - m_new)
        l_ref[...] = alpha * l_ref[...] + jnp.sum(p, axis=0, keepdims=True)
        m_ref[...] = m_new
        pb = p.astype(BF16)
        for c in range(2):
            cols = slice(c * tq, (c + 1) * tq)
            acc_ref[c] = alpha[:, cols] * acc_ref[c] + _dot_tn(vb, pb[:, cols])

    bufs = (s0_ref, s1_ref)
    n_trips = k_ref.shape[1] // (blocks_per_trip * tk)
    scores(0, s0_ref)

    def trip(i, carry):
        j = blocks_per_trip * i
        for u in range(blocks_per_trip):
            scores(j + u + 1, bufs[(u + 1) % 2])
            absorb(j + u, bufs[u % 2])
        return carry

    lax.fori_loop(0, n_trips - 1, trip, 0)
    j_last = blocks_per_trip * (n_trips - 1)
    for u in range(blocks_per_trip):
        if u + 1 < blocks_per_trip:
            scores(j_last + u + 1, bufs[(u + 1) % 2])
        absorb(j_last + u, bufs[u % 2])

    inv = 1.0 / l_ref[...]
    o = acc_ref[0] * inv[:, 0:tq] - lam_ref[0] * (acc_ref[1] * inv[:, tq:2 * tq])
    ms = jnp.mean(o * o, axis=0, keepdims=True)
    o = o * lax.rsqrt(ms + EPS) * gsub_ref[...] * out_scale
    o_ref[0] = o.T.astype(o_ref.dtype)


def _diff_attn(lam, q, k, v, g_sub_col, lam_init, *, tq, tk, blocks_per_trip):
    B, S, _ = q.shape
    assert blocks_per_trip % 2 == 0 and S % (blocks_per_trip * tk) == 0, (S, tk, blocks_per_trip)
    est = (2 * 2 * S * (HEAD_COLS + DIFF_VDIM) * 2 + 4 * tq * HEAD_COLS * 2 + 2 * tq * HEAD_COLS * 2
           + 2 * tq * DIFF_VDIM * 4 + 8 * tk * 2 * tq * 4 + (4 << 20))
    return pl.pallas_call(
        functools.partial(_diff_attn_kernel, tq=tq, tk=tk, blocks_per_trip=blocks_per_trip,
                          out_scale=1.0 - lam_init),
        grid=(B, N_DIFF_HEADS, S // tq),
        in_specs=[pl.BlockSpec(memory_space=pltpu.SMEM),
                  pl.BlockSpec((1, tq, HEAD_COLS), lambda b, h, i: (b, i, h)),
                  pl.BlockSpec((1, S, HEAD_COLS), lambda b, h, i: (b, 0, h)),
                  pl.BlockSpec((1, S, DIFF_VDIM), lambda b, h, i: (b, 0, h)),
                  pl.BlockSpec((DIFF_VDIM, 1), lambda b, h, i: (0, 0))],
        out_specs=pl.BlockSpec((1, tq, DIFF_VDIM), lambda b, h, i: (b, i, h)),
        out_shape=jax.ShapeDtypeStruct((B, S, N_DIFF_HEADS * DIFF_VDIM), BF16),
        scratch_shapes=[pltpu.VMEM((HEAD_COLS, 2 * tq), BF16),
                        pltpu.VMEM((tk, 2 * tq), F32),
                        pltpu.VMEM((tk, 2 * tq), F32),
                        pltpu.VMEM((1, 2 * tq), F32),
                        pltpu.VMEM((1, 2 * tq), F32),
                        pltpu.VMEM((2, DIFF_VDIM, tq), F32)],
        compiler_params=_params(("parallel", "parallel", "parallel"), est),
        name="diff_attn",
    )(lam, q, k, v, g_sub_col)


def _diff_attn_bounded_kernel(lam_ref, bound_ref, q_ref, k_ref, v_ref, gsub_ref, o_ref, qbd_ref, l_ref, acc_ref,
                              *, tq, tk, blocks_per_trip, out_scale):
    qt = q_ref[0].astype(F32).T
    row = lax.broadcasted_iota(jnp.int32, (HEAD_COLS, 1), 0)
    zero = jnp.zeros_like(qt)
    qbd_ref[:, 0:tq] = jnp.where(row < DIFF_QKDIM, qt, zero).astype(BF16)
    qbd_ref[:, tq:2 * tq] = jnp.where(row >= DIFF_QKDIM, qt, zero).astype(BF16)
    l_ref[...] = jnp.zeros(l_ref.shape, F32)
    acc_ref[...] = jnp.zeros(acc_ref.shape, F32)
    shift = bound_ref[0]

    def trip(i, carry):
        for u in range(blocks_per_trip):
            k0 = pl.multiple_of((blocks_per_trip * i + u) * tk, tk)
            vb = v_ref[0, pl.ds(k0, tk), :]
            p = jnp.exp2(_dot(k_ref[0, pl.ds(k0, tk), :], qbd_ref[...]) - shift)
            l_ref[...] += jnp.sum(p, axis=0, keepdims=True)
            pb = p.astype(BF16)
            for c in range(2):
                acc_ref[c] += _dot_tn(vb, pb[:, c * tq:(c + 1) * tq])
        return carry

    lax.fori_loop(0, k_ref.shape[1] // (blocks_per_trip * tk), trip, 0)

    inv = 1.0 / l_ref[...]
    o = acc_ref[0] * inv[:, 0:tq] - lam_ref[0] * (acc_ref[1] * inv[:, tq:2 * tq])
    ms = jnp.mean(o * o, axis=0, keepdims=True)
    o = o * lax.rsqrt(ms + EPS) * gsub_ref[...] * out_scale
    o_ref[0] = o.T.astype(o_ref.dtype)


def _diff_attn_bounded(lam, bound, q, k, v, g_sub_col, lam_init, *, tq, tk, blocks_per_trip):
    B, S, _ = q.shape
    assert S % (blocks_per_trip * tk) == 0, (S, tk, blocks_per_trip)
    est = (2 * 2 * S * (HEAD_COLS + DIFF_VDIM) * 2 + 4 * tq * HEAD_COLS * 2 + 2 * tq * HEAD_COLS * 2
           + 2 * tq * DIFF_VDIM * 4 + 8 * tk * 2 * tq * 4 + (4 << 20))
    return pl.pallas_call(
        functools.partial(_diff_attn_bounded_kernel, tq=tq, tk=tk, blocks_per_trip=blocks_per_trip,
                          out_scale=1.0 - lam_init),
        grid=(B, N_DIFF_HEADS, S // tq),
        in_specs=[pl.BlockSpec(memory_space=pltpu.SMEM),
                  pl.BlockSpec(memory_space=pltpu.SMEM),
                  pl.BlockSpec((1, tq, HEAD_COLS), lambda b, h, i: (b, i, h)),
                  pl.BlockSpec((1, S, HEAD_COLS), lambda b, h, i: (b, 0, h)),
                  pl.BlockSpec((1, S, DIFF_VDIM), lambda b, h, i: (b, 0, h)),
                  pl.BlockSpec((DIFF_VDIM, 1), lambda b, h, i: (0, 0))],
        out_specs=pl.BlockSpec((1, tq, DIFF_VDIM), lambda b, h, i: (b, i, h)),
        out_shape=jax.ShapeDtypeStruct((B, S, N_DIFF_HEADS * DIFF_VDIM), BF16),
        scratch_shapes=[pltpu.VMEM((HEAD_COLS, 2 * tq), BF16),
                        pltpu.VMEM((1, 2 * tq), F32),
                        pltpu.VMEM((2, DIFF_VDIM, tq), F32)],
        compiler_params=_params(("parallel", "parallel", "parallel"), est),
        name="diff_attn_bounded",
    )(lam, bound, q, k, v, g_sub_col)


def _mix_out_kernel(a_ref, u_ref, up_ref, un_ref, x_ref, wp_ref, ps_ref, wo_ref, o_ref, *, tm, seq):
    i = pl.program_id(1)
    u_prev = jnp.where(i > 0, up_ref[0], 0.0)
    u_next = jnp.where(i < pl.num_programs(1) - 1, un_ref[0], 0.0)
    ue = jnp.concatenate([u_prev, u_ref[0], u_next], axis=0)
    rows = tm + 2 * POOL_HALO
    pos = i * tm + lax.broadcasted_iota(jnp.int32, (tm, 1), 0)
    a_width = a_ref.shape[2]
    acc = x_ref[0] + _dot(a_ref[0], wo_ref[0:a_width, :])
    pooled = []
    for g, w in enumerate(POOL_WINDOWS):
        c0 = g * POOL_GROUP_WIDTH
        ug = ue[:, c0:c0 + POOL_GROUP_WIDTH]
        win = ug + pltpu.roll(ug, 1, 0)
        shift = 1
        while 2 * shift < w:
            win = pltpu.roll(win, shift, 0) + pltpu.roll(win, rows - shift, 0)
            shift *= 2
        win = win[POOL_HALO:POOL_HALO + tm]
        cnt = jnp.minimum(pos + w // 2, seq) - jnp.maximum(pos - w // 2, 0)
        z = win / cnt.astype(F32) - ug[POOL_HALO:POOL_HALO + tm]
        pg = _dot(z.astype(BF16), wp_ref[g]) * ps_ref[:, c0:c0 + POOL_GROUP_WIDTH]
        pooled.append(pg.astype(BF16))
    o_ref[0] = acc + _dot(jnp.concatenate(pooled, axis=1), wo_ref[a_width:, :])


def _mix_out(a, u, x, w_pool, pool_scale, w_out, *, tm):
    B, S, D = x.shape
    a_width, u_width = a.shape[2], u.shape[2]
    halo_blocks = tm // POOL_HALO
    n_halo = S // POOL_HALO
    row = lambda b, i: (b, i, 0)
    est = (w_out.size * 2 + w_pool.size * 2 * 2 + 4 * tm * D * 4 + 2 * tm * a_width * 2 + 2 * tm * u_width * 4
           + 6 * tm * u_width * 4 + 2 * tm * D * 4 + (4 << 20))
    return pl.pallas_call(
        functools.partial(_mix_out_kernel, tm=tm, seq=S),
        grid=(B, S // tm),
        in_specs=[pl.BlockSpec((1, tm, a_width), row),
                  pl.BlockSpec((1, tm, u_width), row),
                  pl.BlockSpec((1, POOL_HALO, u_width),
                               lambda b, i: (b, jnp.maximum(i * halo_blocks - 1, 0), 0)),
                  pl.BlockSpec((1, POOL_HALO, u_width),
                               lambda b, i: (b, jnp.minimum((i + 1) * halo_blocks, n_halo - 1), 0)),
                  pl.BlockSpec((1, tm, D), row),
                  pl.BlockSpec(w_pool.shape, lambda b, i: (0, 0, 0)),
                  pl.BlockSpec((1, u_width), lambda b, i: (0, 0)),
                  _resident(w_out.shape)],
        out_specs=pl.BlockSpec((1, tm, D), row),
        out_shape=jax.ShapeDtypeStruct((B, S, D), F32),
        compiler_params=_params(("parallel", "parallel"), est),
        name="mix_out",
    )(a, u, u, u, x, w_pool, pool_scale, w_out)


def _mem_k_kernel(mem_ref, g_ref, w_ref, gk_ref, o_ref):
    m = _rms(mem_ref[0], g_ref[...]).astype(BF16)
    k = _dot(m, w_ref[...])
    o_ref[0] = _rms(k, gk_ref[...]).astype(o_ref.dtype)


def _mem_k(mem, g_mem, wc_kv, gc_k):
    B, n_mem, D = mem.shape
    hd = D // N_CROSS_HEADS
    est = 2 * n_mem * D * 4 + 2 * D * hd * 2 + 4 * n_mem * hd * 4 + n_mem * D * 4 + (4 << 20)
    return pl.pallas_call(
        _mem_k_kernel,
        grid=(N_CROSS_HEADS, B),
        in_specs=[pl.BlockSpec((1, n_mem, D), lambda h, b: (b, 0, 0)),
                  pl.BlockSpec((1, D), lambda h, b: (0, 0)),
                  pl.BlockSpec((D, hd), lambda h, b: (0, h)),
                  pl.BlockSpec((1, hd), lambda h, b: (0, 0))],
        out_specs=pl.BlockSpec((1, n_mem, hd), lambda h, b: (b, 0, h)),
        out_shape=jax.ShapeDtypeStruct((B, n_mem, D), BF16),
        compiler_params=_params(("parallel", "parallel"), est),
        name="mem_k",
    )(mem, g_mem, wc_kv, gc_k)


def _mem_vo_kernel(mem_ref, g_ref, w_ref, wo_ref, o_ref):
    m = _rms(mem_ref[0], g_ref[...]).astype(BF16)
    v = _dot(m, w_ref[...]).astype(BF16)
    o_ref[0] = _dot(v, wo_ref[...]).astype(o_ref.dtype)


def _mem_vo(mem, g_mem, wc_kv, wc_o):
    B, n_mem, D = mem.shape
    hd = D // N_CROSS_HEADS
    est = (2 * n_mem * D * 4 + 2 * D * hd * 2 + 2 * hd * D * 2 + 2 * n_mem * D * 2 + 4 * n_mem * D * 4 + (4 << 20))
    return pl.pallas_call(
        _mem_vo_kernel,
        grid=(N_CROSS_HEADS, B),
        in_specs=[pl.BlockSpec((1, n_mem, D), lambda h, b: (b, 0, 0)),
                  pl.BlockSpec((1, D), lambda h, b: (0, 0)),
                  pl.BlockSpec((D, hd), lambda h, b: (0, N_CROSS_HEADS + h)),
                  pl.BlockSpec((hd, D), lambda h, b: (h, 0))],
        out_specs=pl.BlockSpec((1, n_mem, D), lambda h, b: (b, h, 0)),
        out_shape=jax.ShapeDtypeStruct((B, N_CROSS_HEADS * n_mem, D), BF16),
        compiler_params=_params(("parallel", "parallel"), est),
        name="mem_vo",
    )(mem, g_mem, wc_kv, wc_o)


def _cross_attn_kernel(x_ref, g_ref, wq_ref, gq_ref, k_ref, vo_ref, o_ref):
    x = x_ref[0]
    h = _rms(x, g_ref[...]).astype(BF16)
    hd = gq_ref.shape[1]
    scale = hd ** -0.5
    heads = [slice(c0, c0 + hd) for c0 in range(0, x.shape[1], hd)]
    qs = [_dot(h, wq_ref[:, c]) for c in heads]
    qn = [_rms(q, gq_ref[...]).astype(BF16) for q in qs]
    ss = [_dot_nt(q, k_ref[0, :, c]) * scale for q, c in zip(qn, heads)]
    ps = []
    for s in ss:
        p = jnp.exp(s - jnp.max(s, axis=-1, keepdims=True))
        ps.append((p / jnp.sum(p, axis=-1, keepdims=True)).astype(BF16))
    o_ref[0] = x + _dot(jnp.concatenate(ps, axis=1), vo_ref[0])


def _cross_attn(x, g_cross, wc_q, gc_q, k_mem, vo_mem, *, tm):
    B, S, D = x.shape
    n_mem = k_mem.shape[1]
    row = lambda b, i: (b, i, 0)
    est = (wc_q.size * 2 + 2 * n_mem * D * 2 + 2 * vo_mem.shape[1] * D * 2 + 4 * tm * D * 4 + 5 * tm * D * 4
           + (4 << 20))
    return pl.pallas_call(
        _cross_attn_kernel,
        grid=(B, S // tm),
        in_specs=[pl.BlockSpec((1, tm, D), row),
                  pl.BlockSpec((1, D), lambda b, i: (0, 0)),
                  _resident(wc_q.shape),
                  pl.BlockSpec(gc_q.shape, lambda b, i: (0, 0)),
                  pl.BlockSpec((1, n_mem, D), lambda b, i: (b, 0, 0)),
                  pl.BlockSpec((1, vo_mem.shape[1], D), lambda b, i: (b, 0, 0))],
        out_specs=pl.BlockSpec((1, tm, D), row),
        out_shape=jax.ShapeDtypeStruct((B, S, D), F32),
        compiler_params=_params(("parallel", "parallel"), est),
        name="cross_attn",
    )(x, g_cross, wc_q, gc_q, k_mem, vo_mem)


def _conv_ffn_kernel(x_ref, xp_ref, xn_ref, g_ref, wg_ref, wv_ref, cwg_ref, cwv_ref, cbg_ref, cbv_ref, wd_ref,
                     o_ref, h_ref, *, tm):
    i = pl.program_id(1)
    f = pl.program_id(2)
    rows = tm + 2 * CONV_HALO

    @pl.when(f == 0)
    def _():
        g = g_ref[...]
        h_prev = jnp.where(i > 0, _rms(xp_ref[0], g), 0.0)
        h_next = jnp.where(i < pl.num_programs(1) - 1, _rms(xn_ref[0], g), 0.0)
        h_ref[0:CONV_HALO, :] = h_prev.astype(BF16)
        h_ref[CONV_HALO:CONV_HALO + tm, :] = _rms(x_ref[0], g).astype(BF16)
        h_ref[CONV_HALO + tm:rows, :] = h_next.astype(BF16)
        o_ref[0] = x_ref[0]

    h = h_ref[...]

    def conv(w_ref, cw_ref, cb_ref):
        u = _dot(h, w_ref[...])
        below = pltpu.roll(u, 1, 0)[CONV_HALO:CONV_HALO + tm]
        above = pltpu.roll(u, rows - 1, 0)[CONV_HALO:CONV_HALO + tm]
        mid = u[CONV_HALO:CONV_HALO + tm]
        return below * cw_ref[0:1, :] + mid * cw_ref[1:2, :] + above * cw_ref[2:3, :] + cb_ref[...]

    act = jax.nn.gelu(conv(wg_ref, cwg_ref, cbg_ref)) * conv(wv_ref, cwv_ref, cbv_ref)
    o_ref[0] += _dot(act.astype(BF16), wd_ref[...])


def _conv_ffn(x, g_ffn, w_up, conv_w, conv_b, w_down, *, tm, tf):
    B, S, D = x.shape
    d_ff = w_down.shape[0]
    nf = d_ff // tf
    halo_blocks = tm // CONV_HALO
    n_halo = S // CONV_HALO
    rows = tm + 2 * CONV_HALO
    est = (4 * tm * D * 4 + 2 * 3 * D * tf * 2 + rows * D * 2 + 8 * rows * tf * 4 + 2 * tm * D * 4 + (4 << 20))
    gate = lambda b, i, f: (0, f)
    val = lambda b, i, f: (0, nf + f)
    return pl.pallas_call(
        functools.partial(_conv_ffn_kernel, tm=tm),
        grid=(B, S // tm, nf),
        in_specs=[pl.BlockSpec((1, tm, D), lambda b, i, f: (b, i, 0), pipeline_mode=pl.Buffered(1)),
                  pl.BlockSpec((1, CONV_HALO, D), lambda b, i, f: (b, jnp.maximum(i * halo_blocks - 1, 0), 0)),
                  pl.BlockSpec((1, CONV_HALO, D),
                               lambda b, i, f: (b, jnp.minimum((i + 1) * halo_blocks, n_halo - 1), 0)),
                  pl.BlockSpec((1, D), lambda b, i, f: (0, 0)),
                  pl.BlockSpec((D, tf), gate),
                  pl.BlockSpec((D, tf), val),
                  pl.BlockSpec((conv_w.shape[0], tf), gate),
                  pl.BlockSpec((conv_w.shape[0], tf), val),
                  pl.BlockSpec((1, tf), gate),
                  pl.BlockSpec((1, tf), val),
                  pl.BlockSpec((tf, D), lambda b, i, f: (f, 0))],
        out_specs=pl.BlockSpec((1, tm, D), lambda b, i, f: (b, i, 0)),
        out_shape=jax.ShapeDtypeStruct((B, S, D), F32),
        scratch_shapes=[pltpu.VMEM((rows, D), BF16)],
        compiler_params=_params(("parallel", "parallel", "arbitrary"), est),
        name="conv_ffn",
    )(x, x, x, g_ffn, w_up, w_up, conv_w, conv_w, conv_b, conv_b, w_down)


def _rope_tables(seq, gain, scale):
    half = DIFF_QKDIM // 2
    inv = ROPE_THETA ** (-jnp.arange(half, dtype=F32) / half)
    ang = jnp.arange(seq, dtype=F32)[:, None] * inv[None, :]
    cos = jnp.cos(ang)
    sin = jnp.sin(ang)
    gain = gain.astype(F32)
    a = jnp.concatenate([gain[:half] * cos, gain[half:] * cos], axis=1) * scale
    b = jnp.concatenate([-gain[half:] * sin, gain[:half] * sin], axis=1) * scale
    reps = V7X_LANES // DIFF_QKDIM
    return jnp.tile(a, (1, reps)), jnp.tile(b, (1, reps))


def _segment_ones(width):
    seg = jnp.arange(width) // DIFF_QKDIM
    return (seg[:, None] == seg[None, :]).astype(BF16)


def _tile(n, target):
    t = min(n, target)
    assert n % t == 0, (n, t)
    return t


def _run_trunk(x, mem, layers):
    B, S, D = x.shape
    tm = _tile(S, 512)
    seg = _segment_ones(256)
    for l, p in enumerate(layers):
        lam_init = 0.8 - 0.6 * math.exp(-0.3 * l)
        q_scale = DIFF_QKDIM ** -0.5 * math.log2(math.e)
        aq, bq = _rope_tables(S, p["g_q"], q_scale)
        ak, bk = _rope_tables(S, p["g_k"], 1.0)
        q, k, v, u = _mix_in(x, p["g_mix"], p["w_in"], seg, aq, bq, ak, bk, tm=tm)
        bound = (DIFF_QKDIM * q_scale * 1.01 * jnp.max(jnp.abs(p["g_q"])) * jnp.max(jnp.abs(p["g_k"]))).astype(F32)
        attn = functools.partial(_diff_attn, p["lam"], q, k, v, p["g_sub"].reshape(-1, 1), lam_init,
                                 tq=_tile(S, 512), tk=_tile(S // 4, 512), blocks_per_trip=4)
        attn_bounded = functools.partial(_diff_attn_bounded, p["lam"], bound.reshape(1), q, k, v,
                                         p["g_sub"].reshape(-1, 1), lam_init,
                                         tq=_tile(S, 512), tk=_tile(S // 8, 512),
                                         blocks_per_trip=min(16, S // _tile(S // 8, 512)))
        a = lax.cond(bound <= SOFTMAX_SHIFT_LIMIT, attn_bounded, attn)
        x = _mix_out(a, u, x, p["w_pool"], p["pool_scale"], p["w_out"], tm=tm)
        k_mem = _mem_k(mem, p["g_mem"], p["wc_kv"], p["gc_k"])
        vo_mem = _mem_vo(mem, p["g_mem"], p["wc_kv"], p["wc_o"])
        x = _cross_attn(x, p["g_cross"], p["wc_q"], p["gc_q"], k_mem, vo_mem, tm=tm)
        x = _conv_ffn(x, p["g_ffn"], p["w_up"], p["conv_w"], p["conv_b"], p["w_down"],
                      tm=_tile(S, 1024), tf=FFN_CHUNK)
    return x


def kernel(x_prompt, x_sample, mem_prompt, mem_sample, g_mix, w_in, g_q, g_k, lam_q1, lam_k1, lam_q2, lam_k2,
           g_sub, w_pool, pool_scale, w_out, g_cross, g_mem, wc_q, wc_kv, gc_q, gc_k, wc_o, g_ffn, w_up,
           conv_w, conv_b, w_down):
    depth = w_in.shape[0]
    layers = []
    for l in range(depth):
        lam_init = 0.8 - 0.6 * math.exp(-0.3 * l)
        lam = (jnp.exp(jnp.sum(lam_q1[l].astype(F32) * lam_k1[l].astype(F32)))
               - jnp.exp(jnp.sum(lam_q2[l].astype(F32) * lam_k2[l].astype(F32))) + lam_init)
        row = lambda t: t[l].reshape(1, -1).astype(F32)
        layers.append(dict(
            lam=lam.reshape(1).astype(F32),
            g_mix=row(g_mix), g_q=g_q[l], g_k=g_k[l], g_sub=row(g_sub), pool_scale=row(pool_scale),
            g_cross=row(g_cross), g_mem=row(g_mem), gc_q=row(gc_q), gc_k=row(gc_k), g_ffn=row(g_ffn),
            conv_w=conv_w[l].astype(F32), conv_b=row(conv_b),
            w_in=w_in[l].astype(BF16), w_pool=w_pool[l].astype(BF16), w_out=w_out[l].astype(BF16),
            wc_q=wc_q[l].astype(BF16), wc_kv=wc_kv[l].astype(BF16), wc_o=wc_o[l].astype(BF16),
            w_up=w_up[l].astype(BF16), w_down=w_down[l].astype(BF16)))
    return (_run_trunk(x_prompt, mem_prompt, layers), _run_trunk(x_sample, mem_sample, layers))
```

```python
import functools
import math

import jax
import jax.numpy as jnp
from jax import lax
from jax.experimental import pallas as pl
from jax.experimental.pallas import tpu as pltpu

F32 = jnp.float32
BF16 = jnp.bfloat16

N_DIFF_HEADS = 8
DIFF_QKDIM = 64
DIFF_VDIM = 128
HEAD_COLS = 2 * DIFF_QKDIM
POOL_WINDOWS = (2, 4, 8, 16)
POOL_GROUP_WIDTH = 256
POOL_HALO = 16
N_CROSS_HEADS = 4
CONV_HALO = 8
FFN_CHUNK = 512
ROPE_THETA = 10000.0
EPS = 1e-6
SOFTMAX_SHIFT_LIMIT = 60.0

V7X_VMEM_BYTES = 64 * 1024 * 1024
V7X_LANES = 128
VMEM_LIMIT_CAP = V7X_VMEM_BYTES - 6 * 1024 * 1024


def _vmem_limit(estimate_bytes):
    return int(min(VMEM_LIMIT_CAP, max(32 * 1024 * 1024, estimate_bytes)))


def _params(semantics, vmem_estimate):
    return pltpu.CompilerParams(dimension_semantics=semantics, vmem_limit_bytes=_vmem_limit(vmem_estimate))


def _resident(shape):
    return pl.BlockSpec(shape, lambda *_: (0,) * len(shape), pipeline_mode=pl.Buffered(1))


def _rms(x, gain):
    ms = jnp.mean(x * x, axis=-1, keepdims=True)
    return x * lax.rsqrt(ms + EPS) * gain


def _dot(a, b):
    return jnp.dot(a, b, preferred_element_type=F32)


def _dot_nt(a, b):
    return lax.dot_general(a, b, (((1,), (1,)), ((), ())), preferred_element_type=F32)


def _mix_in_kernel(x_ref, g_ref, w_ref, seg_ref, aq_ref, bq_ref, ak_ref, bk_ref,
                   q_ref, k_ref, v_ref, u_ref, *, qk_width, v_width):
    h = _rms(x_ref[0], g_ref[...]).astype(BF16)
    lane = lax.broadcasted_iota(jnp.int32, (1, V7X_LANES), 1)
    partner_is_above = (lane & (DIFF_QKDIM // 2)) == 0
    seg = seg_ref[...]
    chunk = seg.shape[0]
    for col0, a_ref, b_ref, o_ref in ((0, aq_ref, bq_ref, q_ref), (qk_width, ak_ref, bk_ref, k_ref)):
        a = a_ref[...]
        b = b_ref[...]
        zfull = _dot(h, w_ref[:, col0:col0 + qk_width])
        for c0 in range(0, qk_width, chunk):
            z = zfull[:, c0:c0 + chunk]
            ss = _dot((z * z).astype(BF16), seg)
            zn = z * lax.rsqrt(ss * (1.0 / DIFF_QKDIM) + EPS)
            for c in range(0, chunk, V7X_LANES):
                zc = zn[:, c:c + V7X_LANES]
                partner = jnp.where(partner_is_above,
                                    pltpu.roll(zc, V7X_LANES - DIFF_QKDIM // 2, 1),
                                    pltpu.roll(zc, DIFF_QKDIM // 2, 1))
                o_ref[0, :, c0 + c:c0 + c + V7X_LANES] = (zc * a + partner * b).astype(o_ref.dtype)
    v0 = 2 * qk_width
    v_ref[0] = _dot(h, w_ref[:, v0:v0 + v_width]).astype(v_ref.dtype)
    u_ref[0] = _dot(h, w_ref[:, v0 + v_width:])


def _mix_in(x, g, w_in, seg, aq, bq, ak, bk, *, tm):
    B, S, D = x.shape
    qk_width = N_DIFF_HEADS * HEAD_COLS
    v_width = N_DIFF_HEADS * DIFF_VDIM
    u_width = w_in.shape[1] - 2 * qk_width - v_width
    row = lambda b, i: (b, i, 0)
    tab = pl.BlockSpec((tm, V7X_LANES), lambda b, i: (i, 0))
    est = (w_in.size * 2 + 2 * tm * D * 4 + 2 * tm * (2 * qk_width + v_width) * 2 + 2 * tm * u_width * 4
           + tm * D * 2 + 8 * tm * 1024 * 4 + 8 * tm * V7X_LANES * 4 + (4 << 20))
    return pl.pallas_call(
        functools.partial(_mix_in_kernel, qk_width=qk_width, v_width=v_width),
        grid=(B, S // tm),
        in_specs=[pl.BlockSpec((1, tm, D), row),
                  pl.BlockSpec((1, D), lambda b, i: (0, 0)),
                  _resident(w_in.shape),
                  pl.BlockSpec(seg.shape, lambda b, i: (0, 0)),
                  tab, tab, tab, tab],
        out_specs=[pl.BlockSpec((1, tm, qk_width), row),
                   pl.BlockSpec((1, tm, qk_width), row),
                   pl.BlockSpec((1, tm, v_width), row),
                   pl.BlockSpec((1, tm, u_width), row)],
        out_shape=[jax.ShapeDtypeStruct((B, S, qk_width), BF16),
                   jax.ShapeDtypeStruct((B, S, qk_width), BF16),
                   jax.ShapeDtypeStruct((B, S, v_width), BF16),
                   jax.ShapeDtypeStruct((B, S, u_width), F32)],
        compiler_params=_params(("parallel", "parallel"), est),
        name="mix_in",
    )(x, g, w_in, seg, aq, bq, ak, bk)


def _dot_tn(a, b):
    return lax.dot_general(a, b, (((0,), (0,)), ((), ())), preferred_element_type=F32)


def _diff_attn_kernel(lam_ref, q_ref, k_ref, v_ref, gsub_ref, o_ref, qbd_ref, s0_ref, s1_ref, m_ref, l_ref, acc_ref,
                      *, tq, tk, blocks_per_trip, out_scale):
    qt = q_ref[0].astype(F32).T
    row = lax.broadcasted_iota(jnp.int32, (HEAD_COLS, 1), 0)
    zero = jnp.zeros_like(qt)
    qbd_ref[:, 0:tq] = jnp.where(row < DIFF_QKDIM, qt, zero).astype(BF16)
    qbd_ref[:, tq:2 * tq] = jnp.where(row >= DIFF_QKDIM, qt, zero).astype(BF16)
    m_ref[...] = jnp.full(m_ref.shape, -jnp.inf, F32)
    l_ref[...] = jnp.zeros(l_ref.shape, F32)
    acc_ref[...] = jnp.zeros(acc_ref.shape, F32)

    def scores(j, s_ref):
        k0 = pl.multiple_of(j * tk, tk)
        s_ref[...] = _dot(k_ref[0, pl.ds(k0, tk), :], qbd_ref[...])

    def absorb(j, s_ref):
        k0 = pl.multiple_of(j * tk, tk)
        vb = v_ref[0, pl.ds(k0, tk), :]
        s = s_ref[...]
        m_prev = m_ref[...]
        m_new = jnp.maximum(m_prev, jnp.max(s, axis=0, keepdims=True))
        alpha = jnp.exp2(m_prev - m_new)
        p = jnp.exp2(s - m_new)
        l_ref[...] = alpha * l_ref[...] + jnp.sum(p, axis=0, keepdims=True)
        m_ref[...] = m_new
        pb = p.astype(BF16)
        for c in range(2):
            cols = slice(c * tq, (c + 1) * tq)
            acc_ref[c] = alpha[:, cols] * acc_ref[c] + _dot_tn(vb, pb[:, cols])

    bufs = (s0_ref, s1_ref)
    n_trips = k_ref.shape[1] // (blocks_per_trip * tk)
    scores(0, s0_ref)

    def trip(i, carry):
        j = blocks_per_trip * i
        for u in range(blocks_per_trip):
            scores(j + u + 1, bufs[(u + 1) % 2])
            absorb(j + u, bufs[u % 2])
        return carry

    lax.fori_loop(0, n_trips - 1, trip, 0)
    j_last = blocks_per_trip * (n_trips - 1)
    for u in range(blocks_per_trip):
        if u + 1 < blocks_per_trip:
            scores(j_last + u + 1, bufs[(u + 1) % 2])
        absorb(j_last + u, bufs[u % 2])

    inv = 1.0 / l_ref[...]
    o = acc_ref[0] * inv[:, 0:tq] - lam_ref[0] * (acc_ref[1] * inv[:, tq:2 * tq])
    ms = jnp.mean(o * o, axis=0, keepdims=True)
    o = o * lax.rsqrt(ms + EPS) * gsub_ref[...] * out_scale
    o_ref[0] = o.T.astype(o_ref.dtype)


def _diff_attn(lam, q, k, v, g_sub_col, lam_init, *, tq, tk, blocks_per_trip):
    B, S, _ = q.shape
    assert blocks_per_trip % 2 == 0 and S % (blocks_per_trip * tk) == 0, (S, tk, blocks_per_trip)
    est = (2 * 2 * S * (HEAD_COLS + DIFF_VDIM) * 2 + 4 * tq * HEAD_COLS * 2 + 2 * tq * HEAD_COLS * 2
           + 2 * tq * DIFF_VDIM * 4 + 8 * tk * 2 * tq * 4 + (4 << 20))
    return pl.pallas_call(
        functools.partial(_diff_attn_kernel, tq=tq, tk=tk, blocks_per_trip=blocks_per_trip,
                          out_scale=1.0 - lam_init),
        grid=(B, N_DIFF_HEADS, S // tq),
        in_specs=[pl.BlockSpec(memory_space=pltpu.SMEM),
                  pl.BlockSpec((1, tq, HEAD_COLS), lambda b, h, i: (b, i, h)),
                  pl.BlockSpec((1, S, HEAD_COLS), lambda b, h, i: (b, 0, h)),
                  pl.BlockSpec((1, S, DIFF_VDIM), lambda b, h, i: (b, 0, h)),
                  pl.BlockSpec((DIFF_VDIM, 1), lambda b, h, i: (0, 0))],
        out_specs=pl.BlockSpec((1, tq, DIFF_VDIM), lambda b, h, i: (b, i, h)),
        out_shape=jax.ShapeDtypeStruct((B, S, N_DIFF_HEADS * DIFF_VDIM), BF16),
        scratch_shapes=[pltpu.VMEM((HEAD_COLS, 2 * tq), BF16),
                        pltpu.VMEM((tk, 2 * tq), F32),
                        pltpu.VMEM((tk, 2 * tq), F32),
                        pltpu.VMEM((1, 2 * tq), F32),
                        pltpu.VMEM((1, 2 * tq), F32),
                        pltpu.VMEM((2, DIFF_VDIM, tq), F32)],
        compiler_params=_params(("parallel", "parallel", "parallel"), est),
        name="diff_attn",
    )(lam, q, k, v, g_sub_col)


def _diff_attn_bounded_kernel(lam_ref, bound_ref, q_ref, k_ref, v_ref, gsub_ref, o_ref, qbd_ref, l_ref, acc_ref,
                              *, tq, tk, blocks_per_trip, out_scale):
    shift = bound_ref[0]
    row = lax.broadcasted_iota(jnp.int32, (HEAD_COLS, 1), 0)
    for hh in range(q_ref.shape[2] // HEAD_COLS):
        qk_cols = slice(hh * HEAD_COLS, (hh + 1) * HEAD_COLS)
        v_cols = slice(hh * DIFF_VDIM, (hh + 1) * DIFF_VDIM)
        qt = q_ref[0, :, qk_cols].astype(F32).T
        zero = jnp.zeros_like(qt)
        qbd_ref[hh, :, 0:tq] = jnp.where(row < DIFF_QKDIM, qt, zero).astype(BF16)
        qbd_ref[hh, :, tq:2 * tq] = jnp.where(row >= DIFF_QKDIM, qt, zero).astype(BF16)
        l_ref[hh] = jnp.zeros(l_ref.shape[1:], F32)
        acc_ref[hh] = jnp.zeros(acc_ref.shape[1:], F32)

        def trip(i, carry):
            for u in range(blocks_per_trip):
                k0 = pl.multiple_of((blocks_per_trip * i + u) * tk, tk)
                vb = v_ref[0, pl.ds(k0, tk), v_cols]
                p = jnp.exp2(_dot(k_ref[0, pl.ds(k0, tk), qk_cols], qbd_ref[hh]) - shift)
                l_ref[hh] += jnp.sum(p, axis=0, keepdims=True)
                pb = p.astype(BF16)
                for c in range(2):
                    acc_ref[hh, c] += _dot_tn(vb, pb[:, c * tq:(c + 1) * tq])
            return carry

        lax.fori_loop(0, k_ref.shape[1] // (blocks_per_trip * tk), trip, 0)

        inv = 1.0 / l_ref[hh]
        o = acc_ref[hh, 0] * inv[:, 0:tq] - lam_ref[0] * (acc_ref[hh, 1] * inv[:, tq:2 * tq])
        ms = jnp.mean(o * o, axis=0, keepdims=True)
        o = o * lax.rsqrt(ms + EPS) * gsub_ref[...] * out_scale
        o_ref[0, :, v_cols] = o.T.astype(o_ref.dtype)


def _diff_attn_bounded(lam, bound, q, k, v, g_sub_col, lam_init, *, tq, tk, blocks_per_trip, heads_per_step):
    B, S, _ = q.shape
    assert S % (blocks_per_trip * tk) == 0, (S, tk, blocks_per_trip)
    assert N_DIFF_HEADS % heads_per_step == 0
    hps = heads_per_step
    est = (hps * (2 * 2 * S * (HEAD_COLS + DIFF_VDIM) * 2 + 4 * tq * HEAD_COLS * 2 + 2 * tq * HEAD_COLS * 2
                  + 2 * tq * DIFF_VDIM * 4) + 8 * tk * 2 * tq * 4 + (4 << 20))
    return pl.pallas_call(
        functools.partial(_diff_attn_bounded_kernel, tq=tq, tk=tk, blocks_per_trip=blocks_per_trip,
                          out_scale=1.0 - lam_init),
        grid=(B, N_DIFF_HEADS // hps, S // tq),
        in_specs=[pl.BlockSpec(memory_space=pltpu.SMEM),
                  pl.BlockSpec(memory_space=pltpu.SMEM),
                  pl.BlockSpec((1, tq, hps * HEAD_COLS), lambda b, h, i: (b, i, h)),
                  pl.BlockSpec((1, S, hps * HEAD_COLS), lambda b, h, i: (b, 0, h)),
                  pl.BlockSpec((1, S, hps * DIFF_VDIM), lambda b, h, i: (b, 0, h)),
                  pl.BlockSpec((DIFF_VDIM, 1), lambda b, h, i: (0, 0))],
        out_specs=pl.BlockSpec((1, tq, hps * DIFF_VDIM), lambda b, h, i: (b, i, h)),
        out_shape=jax.ShapeDtypeStruct((B, S, N_DIFF_HEADS * DIFF_VDIM), BF16),
        scratch_shapes=[pltpu.VMEM((hps, HEAD_COLS, 2 * tq), BF16),
                        pltpu.VMEM((hps, 1, 2 * tq), F32),
                        pltpu.VMEM((hps, 2, DIFF_VDIM, tq), F32)],
        compiler_params=_params(("parallel", "parallel", "parallel"), est),
        name="diff_attn_bounded",
    )(lam, bound, q, k, v, g_sub_col)


def _pool_fold_kernel(wp_ref, ps_ref, wo_ref, o_ref):
    w = (wp_ref[0].astype(F32) * ps_ref[...]).astype(BF16)
    o_ref[...] = _dot(w, wo_ref[...]).astype(o_ref.dtype)


def _pool_fold(w_pool, pool_scale, w_out):
    groups, gw, _ = w_pool.shape
    D = w_out.shape[1]
    a_width = w_out.shape[0] - groups * gw
    folded = pl.pallas_call(
        _pool_fold_kernel,
        grid=(groups,),
        in_specs=[pl.BlockSpec((1, gw, gw), lambda g: (g, 0, 0)),
                  pl.BlockSpec((1, gw), lambda g: (0, g)),
                  pl.BlockSpec((gw, D), lambda g: (a_width // gw + g, 0))],
        out_specs=pl.BlockSpec((gw, D), lambda g: (g, 0)),
        out_shape=jax.ShapeDtypeStruct((groups * gw, D), BF16),
        compiler_params=_params(("parallel",), 8 * gw * D * 4 + (4 << 20)),
        name="pool_fold",
    )(w_pool, pool_scale, w_out)
    return jnp.concatenate([w_out[:a_width], folded], axis=0)


def _mix_out_kernel(a_ref, u_ref, up_ref, un_ref, x_ref, wo_ref, o_ref, *, tm, seq):
    i = pl.program_id(1)
    u_prev = jnp.where(i > 0, up_ref[0], 0.0)
    u_next = jnp.where(i < pl.num_programs(1) - 1, un_ref[0], 0.0)
    ue = jnp.concatenate([u_prev, u_ref[0], u_next], axis=0)
    rows = tm + 2 * POOL_HALO
    pos = i * tm + lax.broadcasted_iota(jnp.int32, (tm, 1), 0)
    a_width = a_ref.shape[2]
    acc = x_ref[0] + _dot(a_ref[0], wo_ref[0:a_width, :])
    pooled = []
    for g, w in enumerate(POOL_WINDOWS):
        c0 = g * POOL_GROUP_WIDTH
        ug = ue[:, c0:c0 + POOL_GROUP_WIDTH]
        win = ug + pltpu.roll(ug, 1, 0)
        shift = 1
        while 2 * shift < w:
            win = pltpu.roll(win, shift, 0) + pltpu.roll(win, rows - shift, 0)
            shift *= 2
        win = win[POOL_HALO:POOL_HALO + tm]
        cnt = jnp.minimum(pos + w // 2, seq) - jnp.maximum(pos - w // 2, 0)
        z = win / cnt.astype(F32) - ug[POOL_HALO:POOL_HALO + tm]
        pooled.append(z.astype(BF16))
    o_ref[0] = acc + _dot(jnp.concatenate(pooled, axis=1), wo_ref[a_width:, :])


def _mix_out(a, u, x, w_out, *, tm):
    B, S, D = x.shape
    a_width, u_width = a.shape[2], u.shape[2]
    halo_blocks = tm // POOL_HALO
    n_halo = S // POOL_HALO
    row = lambda b, i: (b, i, 0)
    est = (w_out.size * 2 + 4 * tm * D * 4 + 2 * tm * a_width * 2 + 2 * tm * u_width * 4
           + 6 * tm * u_width * 4 + 2 * tm * D * 4 + (4 << 20))
    return pl.pallas_call(
        functools.partial(_mix_out_kernel, tm=tm, seq=S),
        grid=(B, S // tm),
        in_specs=[pl.BlockSpec((1, tm, a_width), row),
                  pl.BlockSpec((1, tm, u_width), row),
                  pl.BlockSpec((1, POOL_HALO, u_width),
                               lambda b, i: (b, jnp.maximum(i * halo_blocks - 1, 0), 0)),
                  pl.BlockSpec((1, POOL_HALO, u_width),
                               lambda b, i: (b, jnp.minimum((i + 1) * halo_blocks, n_halo - 1), 0)),
                  pl.BlockSpec((1, tm, D), row),
                  _resident(w_out.shape)],
        out_specs=pl.BlockSpec((1, tm, D), row),
        out_shape=jax.ShapeDtypeStruct((B, S, D), F32),
        compiler_params=_params(("parallel", "parallel"), est),
        name="mix_out",
    )(a, u, u, u, x, w_out)


def _mem_k_kernel(mem_ref, g_ref, w_ref, gk_ref, o_ref):
    m = _rms(mem_ref[0], g_ref[...]).astype(BF16)
    k = _dot(m, w_ref[...])
    o_ref[0] = _rms(k, gk_ref[...]).astype(o_ref.dtype)


def _mem_k(mem, g_mem, wc_kv, gc_k):
    B, n_mem, D = mem.shape
    hd = D // N_CROSS_HEADS
    est = 2 * n_mem * D * 4 + 2 * D * hd * 2 + 4 * n_mem * hd * 4 + n_mem * D * 4 + (4 << 20)
    return pl.pallas_call(
        _mem_k_kernel,
        grid=(N_CROSS_HEADS, B),
        in_specs=[pl.BlockSpec((1, n_mem, D), lambda h, b: (b, 0, 0)),
                  pl.BlockSpec((1, D), lambda h, b: (0, 0)),
                  pl.BlockSpec((D, hd), lambda h, b: (0, h)),
                  pl.BlockSpec((1, hd), lambda h, b: (0, 0))],
        out_specs=pl.BlockSpec((1, n_mem, hd), lambda h, b: (b, 0, h)),
        out_shape=jax.ShapeDtypeStruct((B, n_mem, D), BF16),
        compiler_params=_params(("parallel", "parallel"), est),
        name="mem_k",
    )(mem, g_mem, wc_kv, gc_k)


def _mem_vo_kernel(mem_ref, g_ref, w_ref, wo_ref, o_ref):
    m = _rms(mem_ref[0], g_ref[...]).astype(BF16)
    v = _dot(m, w_ref[...]).astype(BF16)
    o_ref[0] = _dot(v, wo_ref[...]).astype(o_ref.dtype)


def _mem_vo(mem, g_mem, wc_kv, wc_o):
    B, n_mem, D = mem.shape
    hd = D // N_CROSS_HEADS
    est = (2 * n_mem * D * 4 + 2 * D * hd * 2 + 2 * hd * D * 2 + 2 * n_mem * D * 2 + 4 * n_mem * D * 4 + (4 << 20))
    return pl.pallas_call(
        _mem_vo_kernel,
        grid=(N_CROSS_HEADS, B),
        in_specs=[pl.BlockSpec((1, n_mem, D), lambda h, b: (b, 0, 0)),
                  pl.BlockSpec((1, D), lambda h, b: (0, 0)),
                  pl.BlockSpec((D, hd), lambda h, b: (0, N_CROSS_HEADS + h)),
                  pl.BlockSpec((hd, D), lambda h, b: (h, 0))],
        out_specs=pl.BlockSpec((1, n_mem, D), lambda h, b: (b, h, 0)),
        out_shape=jax.ShapeDtypeStruct((B, N_CROSS_HEADS * n_mem, D), BF16),
        compiler_params=_params(("parallel", "parallel"), est),
        name="mem_vo",
    )(mem, g_mem, wc_kv, wc_o)


def _cross_attn_kernel(x_ref, g_ref, wq_ref, gq_ref, k_ref, vo_ref, o_ref):
    x = x_ref[0]
    h = _rms(x, g_ref[...]).astype(BF16)
    hd = gq_ref.shape[1]
    scale = hd ** -0.5
    heads = [slice(c0, c0 + hd) for c0 in range(0, x.shape[1], hd)]
    qs = [_dot(h, wq_ref[:, c]) for c in heads]
    qn = [_rms(q, gq_ref[...]).astype(BF16) for q in qs]
    ss = [_dot_nt(q, k_ref[0, :, c]) * scale for q, c in zip(qn, heads)]
    ps = []
    for s in ss:
        p = jnp.exp(s - jnp.max(s, axis=-1, keepdims=True))
        ps.append((p / jnp.sum(p, axis=-1, keepdims=True)).astype(BF16))
    o_ref[0] = x + _dot(jnp.concatenate(ps, axis=1), vo_ref[0])


def _cross_attn(x, g_cross, wc_q, gc_q, k_mem, vo_mem, *, tm):
    B, S, D = x.shape
    n_mem = k_mem.shape[1]
    row = lambda b, i: (b, i, 0)
    est = (wc_q.size * 2 + 2 * n_mem * D * 2 + 2 * vo_mem.shape[1] * D * 2 + 4 * tm * D * 4 + 5 * tm * D * 4
           + (4 << 20))
    return pl.pallas_call(
        _cross_attn_kernel,
        grid=(B, S // tm),
        in_specs=[pl.BlockSpec((1, tm, D), row),
                  pl.BlockSpec((1, D), lambda b, i: (0, 0)),
                  _resident(wc_q.shape),
                  pl.BlockSpec(gc_q.shape, lambda b, i: (0, 0)),
                  pl.BlockSpec((1, n_mem, D), lambda b, i: (b, 0, 0)),
                  pl.BlockSpec((1, vo_mem.shape[1], D), lambda b, i: (b, 0, 0))],
        out_specs=pl.BlockSpec((1, tm, D), row),
        out_shape=jax.ShapeDtypeStruct((B, S, D), F32),
        compiler_params=_params(("parallel", "parallel"), est),
        name="cross_attn",
    )(x, g_cross, wc_q, gc_q, k_mem, vo_mem)


def _conv_ffn_kernel(x_ref, xp_ref, xn_ref, g_ref, wg_ref, wv_ref, cwg_ref, cwv_ref, cbg_ref, cbv_ref, wd_ref,
                     o_ref, h_ref, *, tm):
    i = pl.program_id(1)
    f = pl.program_id(2)
    rows = tm + 2 * CONV_HALO

    @pl.when(f == 0)
    def _():
        g = g_ref[...]
        h_prev = jnp.where(i > 0, _rms(xp_ref[0], g), 0.0)
        h_next = jnp.where(i < pl.num_programs(1) - 1, _rms(xn_ref[0], g), 0.0)
        h_ref[0:CONV_HALO, :] = h_prev.astype(BF16)
        h_ref[CONV_HALO:CONV_HALO + tm, :] = _rms(x_ref[0], g).astype(BF16)
        h_ref[CONV_HALO + tm:rows, :] = h_next.astype(BF16)
        o_ref[0] = x_ref[0]

    h = h_ref[...]

    def conv(w_ref, cw_ref, cb_ref):
        u = _dot(h, w_ref[...])
        below = pltpu.roll(u, 1, 0)[CONV_HALO:CONV_HALO + tm]
        above = pltpu.roll(u, rows - 1, 0)[CONV_HALO:CONV_HALO + tm]
        mid = u[CONV_HALO:CONV_HALO + tm]
        return below * cw_ref[0:1, :] + mid * cw_ref[1:2, :] + above * cw_ref[2:3, :] + cb_ref[...]

    act = jax.nn.gelu(conv(wg_ref, cwg_ref, cbg_ref)) * conv(wv_ref, cwv_ref, cbv_ref)
    o_ref[0] += _dot(act.astype(BF16), wd_ref[...])


def _conv_ffn(x, g_ffn, w_up, conv_w, conv_b, w_down, *, tm, tf):
    B, S, D = x.shape
    d_ff = w_down.shape[0]
    nf = d_ff // tf
    halo_blocks = tm // CONV_HALO
    n_halo = S // CONV_HALO
    rows = tm + 2 * CONV_HALO
    est = (4 * tm * D * 4 + 2 * 3 * D * tf * 2 + rows * D * 2 + 8 * rows * tf * 4 + 2 * tm * D * 4 + (4 << 20))
    gate = lambda b, i, f: (0, f)
    val = lambda b, i, f: (0, nf + f)
    return pl.pallas_call(
        functools.partial(_conv_ffn_kernel, tm=tm),
        grid=(B, S // tm, nf),
        in_specs=[pl.BlockSpec((1, tm, D), lambda b, i, f: (b, i, 0), pipeline_mode=pl.Buffered(1)),
                  pl.BlockSpec((1, CONV_HALO, D), lambda b, i, f: (b, jnp.maximum(i * halo_blocks - 1, 0), 0)),
                  pl.BlockSpec((1, CONV_HALO, D),
                               lambda b, i, f: (b, jnp.minimum((i + 1) * halo_blocks, n_halo - 1), 0)),
                  pl.BlockSpec((1, D), lambda b, i, f: (0, 0)),
                  pl.BlockSpec((D, tf), gate),
                  pl.BlockSpec((D, tf), val),
                  pl.BlockSpec((conv_w.shape[0], tf), gate),
                  pl.BlockSpec((conv_w.shape[0], tf), val),
                  pl.BlockSpec((1, tf), gate),
                  pl.BlockSpec((1, tf), val),
                  pl.BlockSpec((tf, D), lambda b, i, f: (f, 0))],
        out_specs=pl.BlockSpec((1, tm, D), lambda b, i, f: (b, i, 0)),
        out_shape=jax.ShapeDtypeStruct((B, S, D), F32),
        scratch_shapes=[pltpu.VMEM((rows, D), BF16)],
        compiler_params=_params(("parallel", "parallel", "arbitrary"), est),
        name="conv_ffn",
    )(x, x, x, g_ffn, w_up, w_up, conv_w, conv_w, conv_b, conv_b, w_down)


def _rope_tables(seq, gain, scale):
    half = DIFF_QKDIM // 2
    inv = ROPE_THETA ** (-jnp.arange(half, dtype=F32) / half)
    ang = jnp.arange(seq, dtype=F32)[:, None] * inv[None, :]
    cos = jnp.cos(ang)
    sin = jnp.sin(ang)
    gain = gain.astype(F32)
    a = jnp.concatenate([gain[:half] * cos, gain[half:] * cos], axis=1) * scale
    b = jnp.concatenate([-gain[half:] * sin, gain[:half] * sin], axis=1) * scale
    reps = V7X_LANES // DIFF_QKDIM
    return jnp.tile(a, (1, reps)), jnp.tile(b, (1, reps))


def _segment_ones(width):
    seg = jnp.arange(width) // DIFF_QKDIM
    return (seg[:, None] == seg[None, :]).astype(BF16)


def _tile(n, target):
    t = min(n, target)
    assert n % t == 0, (n, t)
    return t


def _run_trunk(x, mem, layers):
    B, S, D = x.shape
    tm = _tile(S, 512)
    seg = _segment_ones(256)
    for l, p in enumerate(layers):
        lam_init = 0.8 - 0.6 * math.exp(-0.3 * l)
        q_scale = DIFF_QKDIM ** -0.5 * math.log2(math.e)
        aq, bq = _rope_tables(S, p["g_q"], q_scale)
        ak, bk = _rope_tables(S, p["g_k"], 1.0)
        q, k, v, u = _mix_in(x, p["g_mix"], p["w_in"], seg, aq, bq, ak, bk, tm=tm)
        bound = (DIFF_QKDIM * q_scale * 1.01 * jnp.max(jnp.abs(p["g_q"])) * jnp.max(jnp.abs(p["g_k"]))).astype(F32)
        attn = functools.partial(_diff_attn, p["lam"], q, k, v, p["g_sub"].reshape(-1, 1), lam_init,
                                 tq=_tile(S, 512), tk=_tile(S // 4, 512), blocks_per_trip=4)
        attn_bounded = functools.partial(_diff_attn_bounded, p["lam"], bound.reshape(1), q, k, v,
                                         p["g_sub"].reshape(-1, 1), lam_init,
                                         tq=_tile(S, 512), tk=_tile(S // 8, 512),
                                         blocks_per_trip=min(16, S // _tile(S // 8, 512)), heads_per_step=2)
        a = lax.cond(bound <= SOFTMAX_SHIFT_LIMIT, attn_bounded, attn)
        x = _mix_out(a, u, x, p["w_out_folded"], tm=tm)
        k_mem = _mem_k(mem, p["g_mem"], p["wc_kv"], p["gc_k"])
        vo_mem = _mem_vo(mem, p["g_mem"], p["wc_kv"], p["wc_o"])
        x = _cross_attn(x, p["g_cross"], p["wc_q"], p["gc_q"], k_mem, vo_mem, tm=tm)
        x = _conv_ffn(x, p["g_ffn"], p["w_up"], p["conv_w"], p["conv_b"], p["w_down"],
                      tm=_tile(S, 1024), tf=FFN_CHUNK)
    return x


def kernel(x_prompt, x_sample, mem_prompt, mem_sample, g_mix, w_in, g_q, g_k, lam_q1, lam_k1, lam_q2, lam_k2,
           g_sub, w_pool, pool_scale, w_out, g_cross, g_mem, wc_q, wc_kv, gc_q, gc_k, wc_o, g_ffn, w_up,
           conv_w, conv_b, w_down):
    depth = w_in.shape[0]
    layers = []
    for l in range(depth):
        lam_init = 0.8 - 0.6 * math.exp(-0.3 * l)
        lam = (jnp.exp(jnp.sum(lam_q1[l].astype(F32) * lam_k1[l].astype(F32)))
               - jnp.exp(jnp.sum(lam_q2[l].astype(F32) * lam_k2[l].astype(F32))) + lam_init)
        row = lambda t: t[l].reshape(1, -1).astype(F32)
        layers.append(dict(
            lam=lam.reshape(1).astype(F32),
            g_mix=row(g_mix), g_q=g_q[l], g_k=g_k[l], g_sub=row(g_sub),
            g_cross=row(g_cross), g_mem=row(g_mem), gc_q=row(gc_q), gc_k=row(gc_k), g_ffn=row(g_ffn),
            conv_w=conv_w[l].astype(F32), conv_b=row(conv_b),
            w_in=w_in[l].astype(BF16),
            w_out_folded=_pool_fold(w_pool[l].astype(BF16), row(pool_scale), w_out[l].astype(BF16)),
            wc_q=wc_q[l].astype(BF16), wc_kv=wc_kv[l].astype(BF16), wc_o=wc_o[l].astype(BF16),
            w_up=w_up[l].astype(BF16), w_down=w_down[l].astype(BF16)))
    return (_run_trunk(x_prompt, mem_prompt, layers), _run_trunk(x_sample, mem_sample, layers))
```

```python
import functools
import math

import jax
import jax.numpy as jnp
from jax import lax
from jax.experimental import pallas as pl
from jax.experimental.pallas import tpu as pltpu

F32 = jnp.float32
BF16 = jnp.bfloat16

N_DIFF_HEADS = 8
DIFF_QKDIM = 64
DIFF_VDIM = 128
HEAD_COLS = 2 * DIFF_QKDIM
POOL_WINDOWS = (2, 4, 8, 16)
POOL_GROUP_WIDTH = 256
POOL_HALO = 16
N_CROSS_HEADS = 4
CONV_HALO = 8
FFN_CHUNK = 512
ROPE_THETA = 10000.0
EPS = 1e-6
UNSHIFTED_SOFTMAX_SCORE_LIMIT = 30.0

V7X_VMEM_BYTES = 64 * 1024 * 1024
V7X_LANES = 128
VMEM_LIMIT_CAP = V7X_VMEM_BYTES - 6 * 1024 * 1024


def _vmem_limit(estimate_bytes):
    return int(min(VMEM_LIMIT_CAP, max(32 * 1024 * 1024, estimate_bytes)))


def _params(semantics, vmem_estimate):
    return pltpu.CompilerParams(dimension_semantics=semantics, vmem_limit_bytes=_vmem_limit(vmem_estimate))


def _resident(shape):
    return pl.BlockSpec(shape, lambda *_: (0,) * len(shape), pipeline_mode=pl.Buffered(1))


def _rms(x, gain):
    ms = jnp.mean(x * x, axis=-1, keepdims=True)
    return x * lax.rsqrt(ms + EPS) * gain


def _dot(a, b):
    return jnp.dot(a, b, preferred_element_type=F32)


def _dot_nt(a, b):
    return lax.dot_general(a, b, (((1,), (1,)), ((), ())), preferred_element_type=F32)


def _mix_in_kernel(x_ref, g_ref, w_ref, seg_ref, aq_ref, bq_ref, ak_ref, bk_ref,
                   q_ref, k_ref, v_ref, u_ref, *, qk_width, v_width):
    h = _rms(x_ref[0], g_ref[...]).astype(BF16)
    lane = lax.broadcasted_iota(jnp.int32, (1, V7X_LANES), 1)
    partner_is_above = (lane & (DIFF_QKDIM // 2)) == 0
    seg = seg_ref[...]
    chunk = seg.shape[0]
    for col0, a_ref, b_ref, o_ref in ((0, aq_ref, bq_ref, q_ref), (qk_width, ak_ref, bk_ref, k_ref)):
        a = a_ref[...]
        b = b_ref[...]
        zfull = _dot(h, w_ref[:, col0:col0 + qk_width])
        for c0 in range(0, qk_width, chunk):
            z = zfull[:, c0:c0 + chunk]
            ss = _dot((z * z).astype(BF16), seg)
            zn = z * lax.rsqrt(ss * (1.0 / DIFF_QKDIM) + EPS)
            for c in range(0, chunk, V7X_LANES):
                zc = zn[:, c:c + V7X_LANES]
                partner = jnp.where(partner_is_above,
                                    pltpu.roll(zc, V7X_LANES - DIFF_QKDIM // 2, 1),
                                    pltpu.roll(zc, DIFF_QKDIM // 2, 1))
                o_ref[0, :, c0 + c:c0 + c + V7X_LANES] = (zc * a + partner * b).astype(o_ref.dtype)
    v0 = 2 * qk_width
    v_ref[0] = _dot(h, w_ref[:, v0:v0 + v_width]).astype(v_ref.dtype)
    u_ref[0] = _dot(h, w_ref[:, v0 + v_width:])


def _mix_in(x, g, w_in, seg, aq, bq, ak, bk, *, tm):
    B, S, D = x.shape
    qk_width = N_DIFF_HEADS * HEAD_COLS
    v_width = N_DIFF_HEADS * DIFF_VDIM
    u_width = w_in.shape[1] - 2 * qk_width - v_width
    row = lambda b, i: (b, i, 0)
    tab = pl.BlockSpec((tm, V7X_LANES), lambda b, i: (i, 0))
    est = (w_in.size * 2 + 2 * tm * D * 4 + 2 * tm * (2 * qk_width + v_width) * 2 + 2 * tm * u_width * 4
           + tm * D * 2 + 8 * tm * 1024 * 4 + 8 * tm * V7X_LANES * 4 + (4 << 20))
    return pl.pallas_call(
        functools.partial(_mix_in_kernel, qk_width=qk_width, v_width=v_width),
        grid=(B, S // tm),
        in_specs=[pl.BlockSpec((1, tm, D), row),
                  pl.BlockSpec((1, D), lambda b, i: (0, 0)),
                  _resident(w_in.shape),
                  pl.BlockSpec(seg.shape, lambda b, i: (0, 0)),
                  tab, tab, tab, tab],
        out_specs=[pl.BlockSpec((1, tm, qk_width), row),
                   pl.BlockSpec((1, tm, qk_width), row),
                   pl.BlockSpec((1, tm, v_width), row),
                   pl.BlockSpec((1, tm, u_width), row)],
        out_shape=[jax.ShapeDtypeStruct((B, S, qk_width), BF16),
                   jax.ShapeDtypeStruct((B, S, qk_width), BF16),
                   jax.ShapeDtypeStruct((B, S, v_width), BF16),
                   jax.ShapeDtypeStruct((B, S, u_width), F32)],
        compiler_params=_params(("parallel", "parallel"), est),
        name="mix_in",
    )(x, g, w_in, seg, aq, bq, ak, bk)


def _dot_tn(a, b):
    return lax.dot_general(a, b, (((0,), (0,)), ((), ())), preferred_element_type=F32)


def _diff_attn_kernel(lam_ref, q_ref, k_ref, v_ref, gsub_ref, o_ref, qbd_ref, s0_ref, s1_ref, m_ref, l_ref, acc_ref,
                      *, tq, tk, blocks_per_trip, out_scale):
    qt = q_ref[0].astype(F32).T
    row = lax.broadcasted_iota(jnp.int32, (HEAD_COLS, 1), 0)
    zero = jnp.zeros_like(qt)
    qbd_ref[:, 0:tq] = jnp.where(row < DIFF_QKDIM, qt, zero).astype(BF16)
    qbd_ref[:, tq:2 * tq] = jnp.where(row >= DIFF_QKDIM, qt, zero).astype(BF16)
    m_ref[...] = jnp.full(m_ref.shape, -jnp.inf, F32)
    l_ref[...] = jnp.zeros(l_ref.shape, F32)
    acc_ref[...] = jnp.zeros(acc_ref.shape, F32)

    def scores(j, s_ref):
        k0 = pl.multiple_of(j * tk, tk)
        s_ref[...] = _dot(k_ref[0, pl.ds(k0, tk), :], qbd_ref[...])

    def absorb(j, s_ref):
        k0 = pl.multiple_of(j * tk, tk)
        vb = v_ref[0, pl.ds(k0, tk), :]
        s = s_ref[...]
        m_prev = m_ref[...]
        m_new = jnp.maximum(m_prev, jnp.max(s, axis=0, keepdims=True))
        alpha = jnp.exp2(m_prev - m_new)
        p = jnp.exp2(s - m_new)
        l_ref[...] = alpha * l_ref[...] + jnp.sum(p, axis=0, keepdims=True)
        m_ref[...] = m_new
        pb = p.astype(BF16)
        for c in range(2):
            cols = slice(c * tq, (c + 1) * tq)
            acc_ref[c] = alpha[:, cols] * acc_ref[c] + _dot_tn(vb, pb[:, cols])

    bufs = (s0_ref, s1_ref)
    n_trips = k_ref.shape[1] // (blocks_per_trip * tk)
    scores(0, s0_ref)

    def trip(i, carry):
        j = blocks_per_trip * i
        for u in range(blocks_per_trip):
            scores(j + u + 1, bufs[(u + 1) % 2])
            absorb(j + u, bufs[u % 2])
        return carry

    lax.fori_loop(0, n_trips - 1, trip, 0)
    j_last = blocks_per_trip * (n_trips - 1)
    for u in range(blocks_per_trip):
        if u + 1 < blocks_per_trip:
            scores(j_last + u + 1, bufs[(u + 1) % 2])
        absorb(j_last + u, bufs[u % 2])

    inv = 1.0 / l_ref[...]
    o = acc_ref[0] * inv[:, 0:tq] - lam_ref[0] * (acc_ref[1] * inv[:, tq:2 * tq])
    ms = jnp.mean(o * o, axis=0, keepdims=True)
    o = o * lax.rsqrt(ms + EPS) * gsub_ref[...] * out_scale
    o_ref[0] = o.T.astype(o_ref.dtype)


def _diff_attn(lam, q, k, v, g_sub_col, lam_init, *, tq, tk, blocks_per_trip):
    B, S, _ = q.shape
    assert blocks_per_trip % 2 == 0 and S % (blocks_per_trip * tk) == 0, (S, tk, blocks_per_trip)
    est = (2 * 2 * S * (HEAD_COLS + DIFF_VDIM) * 2 + 4 * tq * HEAD_COLS * 2 + 2 * tq * HEAD_COLS * 2
           + 2 * tq * DIFF_VDIM * 4 + 8 * tk * 2 * tq * 4 + (4 << 20))
    return pl.pallas_call(
        functools.partial(_diff_attn_kernel, tq=tq, tk=tk, blocks_per_trip=blocks_per_trip,
                          out_scale=1.0 - lam_init),
        grid=(B, N_DIFF_HEADS, S // tq),
        in_specs=[pl.BlockSpec(memory_space=pltpu.SMEM),
                  pl.BlockSpec((1, tq, HEAD_COLS), lambda b, h, i: (b, i, h)),
                  pl.BlockSpec((1, S, HEAD_COLS), lambda b, h, i: (b, 0, h)),
                  pl.BlockSpec((1, S, DIFF_VDIM), lambda b, h, i: (b, 0, h)),
                  pl.BlockSpec((DIFF_VDIM, 1), lambda b, h, i: (0, 0))],
        out_specs=pl.BlockSpec((1, tq, DIFF_VDIM), lambda b, h, i: (b, i, h)),
        out_shape=jax.ShapeDtypeStruct((B, S, N_DIFF_HEADS * DIFF_VDIM), BF16),
        scratch_shapes=[pltpu.VMEM((HEAD_COLS, 2 * tq), BF16),
                        pltpu.VMEM((tk, 2 * tq), F32),
                        pltpu.VMEM((tk, 2 * tq), F32),
                        pltpu.VMEM((1, 2 * tq), F32),
                        pltpu.VMEM((1, 2 * tq), F32),
                        pltpu.VMEM((2, DIFF_VDIM, tq), F32)],
        compiler_params=_params(("parallel", "parallel", "parallel"), est),
        name="diff_attn",
    )(lam, q, k, v, g_sub_col)


def _diff_attn_bounded_kernel(lam_ref, q_ref, k_ref, v_ref, gsub_ref, o_ref, qbd_ref, l_ref, acc_ref,
                              *, tq, tk, blocks_per_trip, out_scale):
    row = lax.broadcasted_iota(jnp.int32, (HEAD_COLS, 1), 0)
    for hh in range(q_ref.shape[2] // HEAD_COLS):
        qk_cols = slice(hh * HEAD_COLS, (hh + 1) * HEAD_COLS)
        v_cols = slice(hh * DIFF_VDIM, (hh + 1) * DIFF_VDIM)
        qt = q_ref[0, :, qk_cols].astype(F32).T
        zero = jnp.zeros_like(qt)
        qbd_ref[hh, :, 0:tq] = jnp.where(row < DIFF_QKDIM, qt, zero).astype(BF16)
        qbd_ref[hh, :, tq:2 * tq] = jnp.where(row >= DIFF_QKDIM, qt, zero).astype(BF16)
        l_ref[hh] = jnp.zeros(l_ref.shape[1:], F32)
        acc_ref[hh] = jnp.zeros(acc_ref.shape[1:], F32)

        def trip(i, carry):
            for u in range(blocks_per_trip):
                k0 = pl.multiple_of((blocks_per_trip * i + u) * tk, tk)
                vb = v_ref[0, pl.ds(k0, tk), v_cols]
                p = jnp.exp2(_dot(k_ref[0, pl.ds(k0, tk), qk_cols], qbd_ref[hh]))
                l_ref[hh] += jnp.sum(p, axis=0, keepdims=True)
                pb = p.astype(BF16)
                for c in range(2):
                    acc_ref[hh, c] += _dot_tn(vb, pb[:, c * tq:(c + 1) * tq])
            return carry

        lax.fori_loop(0, k_ref.shape[1] // (blocks_per_trip * tk), trip, 0)

        inv = 1.0 / l_ref[hh]
        o = acc_ref[hh, 0] * inv[:, 0:tq] - lam_ref[0] * (acc_ref[hh, 1] * inv[:, tq:2 * tq])
        ms = jnp.mean(o * o, axis=0, keepdims=True)
        o = o * lax.rsqrt(ms + EPS) * gsub_ref[...] * out_scale
        o_ref[0, :, v_cols] = o.T.astype(o_ref.dtype)


def _diff_attn_bounded(lam, q, k, v, g_sub_col, lam_init, *, tq, tk, blocks_per_trip, heads_per_step):
    B, S, _ = q.shape
    assert S % (blocks_per_trip * tk) == 0, (S, tk, blocks_per_trip)
    assert N_DIFF_HEADS % heads_per_step == 0
    hps = heads_per_step
    est = (hps * (2 * 2 * S * (HEAD_COLS + DIFF_VDIM) * 2 + 4 * tq * HEAD_COLS * 2 + 2 * tq * HEAD_COLS * 2
                  + 2 * tq * DIFF_VDIM * 4) + 8 * tk * 2 * tq * 4 + (4 << 20))
    return pl.pallas_call(
        functools.partial(_diff_attn_bounded_kernel, tq=tq, tk=tk, blocks_per_trip=blocks_per_trip,
                          out_scale=1.0 - lam_init),
        grid=(B, N_DIFF_HEADS // hps, S // tq),
        in_specs=[pl.BlockSpec(memory_space=pltpu.SMEM),
                  pl.BlockSpec((1, tq, hps * HEAD_COLS), lambda b, h, i: (b, i, h)),
                  pl.BlockSpec((1, S, hps * HEAD_COLS), lambda b, h, i: (b, 0, h)),
                  pl.BlockSpec((1, S, hps * DIFF_VDIM), lambda b, h, i: (b, 0, h)),
                  pl.BlockSpec((DIFF_VDIM, 1), lambda b, h, i: (0, 0))],
        out_specs=pl.BlockSpec((1, tq, hps * DIFF_VDIM), lambda b, h, i: (b, i, h)),
        out_shape=jax.ShapeDtypeStruct((B, S, N_DIFF_HEADS * DIFF_VDIM), BF16),
        scratch_shapes=[pltpu.VMEM((hps, HEAD_COLS, 2 * tq), BF16),
                        pltpu.VMEM((hps, 1, 2 * tq), F32),
                        pltpu.VMEM((hps, 2, DIFF_VDIM, tq), F32)],
        compiler_params=_params(("parallel", "parallel", "parallel"), est),
        name="diff_attn_bounded",
    )(lam, q, k, v, g_sub_col)


def _pool_fold_kernel(wp_ref, ps_ref, wo_ref, o_ref):
    w = (wp_ref[0].astype(F32) * ps_ref[...]).astype(BF16)
    o_ref[...] = _dot(w, wo_ref[...]).astype(o_ref.dtype)


def _pool_fold(w_pool, pool_scale, w_out):
    groups, gw, _ = w_pool.shape
    D = w_out.shape[1]
    a_width = w_out.shape[0] - groups * gw
    folded = pl.pallas_call(
        _pool_fold_kernel,
        grid=(groups,),
        in_specs=[pl.BlockSpec((1, gw, gw), lambda g: (g, 0, 0)),
                  pl.BlockSpec((1, gw), lambda g: (0, g)),
                  pl.BlockSpec((gw, D), lambda g: (a_width // gw + g, 0))],
        out_specs=pl.BlockSpec((gw, D), lambda g: (g, 0)),
        out_shape=jax.ShapeDtypeStruct((groups * gw, D), BF16),
        compiler_params=_params(("parallel",), 8 * gw * D * 4 + (4 << 20)),
        name="pool_fold",
    )(w_pool, pool_scale, w_out)
    return jnp.concatenate([w_out[:a_width], folded], axis=0)


def _mix_out_kernel(a_ref, u_ref, up_ref, un_ref, x_ref, wo_ref, o_ref, *, tm, seq):
    i = pl.program_id(1)
    u_prev = jnp.where(i > 0, up_ref[0], 0.0)
    u_next = jnp.where(i < pl.num_programs(1) - 1, un_ref[0], 0.0)
    ue = jnp.concatenate([u_prev, u_ref[0], u_next], axis=0)
    rows = tm + 2 * POOL_HALO
    pos = i * tm + lax.broadcasted_iota(jnp.int32, (tm, 1), 0)
    a_width = a_ref.shape[2]
    acc = x_ref[0] + _dot(a_ref[0], wo_ref[0:a_width, :])
    pooled = []
    for g, w in enumerate(POOL_WINDOWS):
        c0 = g * POOL_GROUP_WIDTH
        ug = ue[:, c0:c0 + POOL_GROUP_WIDTH]
        win = ug + pltpu.roll(ug, 1, 0)
        shift = 1
        while 2 * shift < w:
            win = pltpu.roll(win, shift, 0) + pltpu.roll(win, rows - shift, 0)
            shift *= 2
        win = win[POOL_HALO:POOL_HALO + tm]
        cnt = jnp.minimum(pos + w // 2, seq) - jnp.maximum(pos - w // 2, 0)
        z = win / cnt.astype(F32) - ug[POOL_HALO:POOL_HALO + tm]
        pooled.append(z.astype(BF16))
    o_ref[0] = acc + _dot(jnp.concatenate(pooled, axis=1), wo_ref[a_width:, :])


def _mix_out(a, u, x, w_out, *, tm):
    B, S, D = x.shape
    a_width, u_width = a.shape[2], u.shape[2]
    halo_blocks = tm // POOL_HALO
    n_halo = S // POOL_HALO
    row = lambda b, i: (b, i, 0)
    est = (w_out.size * 2 + 4 * tm * D * 4 + 2 * tm * a_width * 2 + 2 * tm * u_width * 4
           + 6 * tm * u_width * 4 + 2 * tm * D * 4 + (4 << 20))
    return pl.pallas_call(
        functools.partial(_mix_out_kernel, tm=tm, seq=S),
        grid=(B, S // tm),
        in_specs=[pl.BlockSpec((1, tm, a_width), row),
                  pl.BlockSpec((1, tm, u_width), row),
                  pl.BlockSpec((1, POOL_HALO, u_width),
                               lambda b, i: (b, jnp.maximum(i * halo_blocks - 1, 0), 0)),
                  pl.BlockSpec((1, POOL_HALO, u_width),
                               lambda b, i: (b, jnp.minimum((i + 1) * halo_blocks, n_halo - 1), 0)),
                  pl.BlockSpec((1, tm, D), row),
                  _resident(w_out.shape)],
        out_specs=pl.BlockSpec((1, tm, D), row),
        out_shape=jax.ShapeDtypeStruct((B, S, D), F32),
        compiler_params=_params(("parallel", "parallel"), est),
        name="mix_out",
    )(a, u, u, u, x, w_out)


def _mem_k_kernel(mem_ref, g_ref, w_ref, gk_ref, o_ref):
    m = _rms(mem_ref[0], g_ref[...]).astype(BF16)
    k = _dot(m, w_ref[...])
    o_ref[0] = _rms(k, gk_ref[...]).astype(o_ref.dtype)


def _mem_k(mem, g_mem, wc_kv, gc_k):
    B, n_mem, D = mem.shape
    hd = D // N_CROSS_HEADS
    est = 2 * n_mem * D * 4 + 2 * D * hd * 2 + 4 * n_mem * hd * 4 + n_mem * D * 4 + (4 << 20)
    return pl.pallas_call(
        _mem_k_kernel,
        grid=(N_CROSS_HEADS, B),
        in_specs=[pl.BlockSpec((1, n_mem, D), lambda h, b: (b, 0, 0)),
                  pl.BlockSpec((1, D), lambda h, b: (0, 0)),
                  pl.BlockSpec((D, hd), lambda h, b: (0, h)),
                  pl.BlockSpec((1, hd), lambda h, b: (0, 0))],
        out_specs=pl.BlockSpec((1, n_mem, hd), lambda h, b: (b, 0, h)),
        out_shape=jax.ShapeDtypeStruct((B, n_mem, D), BF16),
        compiler_params=_params(("parallel", "parallel"), est),
        name="mem_k",
    )(mem, g_mem, wc_kv, gc_k)


def _mem_vo_kernel(mem_ref, g_ref, w_ref, wo_ref, o_ref):
    m = _rms(mem_ref[0], g_ref[...]).astype(BF16)
    v = _dot(m, w_ref[...]).astype(BF16)
    o_ref[0] = _dot(v, wo_ref[...]).astype(o_ref.dtype)


def _mem_vo(mem, g_mem, wc_kv, wc_o):
    B, n_mem, D = mem.shape
    hd = D // N_CROSS_HEADS
    est = (2 * n_mem * D * 4 + 2 * D * hd * 2 + 2 * hd * D * 2 + 2 * n_mem * D * 2 + 4 * n_mem * D * 4 + (4 << 20))
    return pl.pallas_call(
        _mem_vo_kernel,
        grid=(N_CROSS_HEADS, B),
        in_specs=[pl.BlockSpec((1, n_mem, D), lambda h, b: (b, 0, 0)),
                  pl.BlockSpec((1, D), lambda h, b: (0, 0)),
                  pl.BlockSpec((D, hd), lambda h, b: (0, N_CROSS_HEADS + h)),
                  pl.BlockSpec((hd, D), lambda h, b: (h, 0))],
        out_specs=pl.BlockSpec((1, n_mem, D), lambda h, b: (b, h, 0)),
        out_shape=jax.ShapeDtypeStruct((B, N_CROSS_HEADS * n_mem, D), BF16),
        compiler_params=_params(("parallel", "parallel"), est),
        name="mem_vo",
    )(mem, g_mem, wc_kv, wc_o)


def _cross_attn_kernel(x_ref, g_ref, wq_ref, gq_ref, k_ref, vo_ref, o_ref):
    x = x_ref[0]
    h = _rms(x, g_ref[...]).astype(BF16)
    hd = gq_ref.shape[1]
    scale = hd ** -0.5
    heads = [slice(c0, c0 + hd) for c0 in range(0, x.shape[1], hd)]
    qs = [_dot(h, wq_ref[:, c]) for c in heads]
    qn = [_rms(q, gq_ref[...]).astype(BF16) for q in qs]
    ss = [_dot_nt(q, k_ref[0, :, c]) * scale for q, c in zip(qn, heads)]
    ps = []
    for s in ss:
        p = jnp.exp(s - jnp.max(s, axis=-1, keepdims=True))
        ps.append((p / jnp.sum(p, axis=-1, keepdims=True)).astype(BF16))
    o_ref[0] = x + _dot(jnp.concatenate(ps, axis=1), vo_ref[0])


def _cross_attn(x, g_cross, wc_q, gc_q, k_mem, vo_mem, *, tm):
    B, S, D = x.shape
    n_mem = k_mem.shape[1]
    row = lambda b, i: (b, i, 0)
    est = (wc_q.size * 2 + 2 * n_mem * D * 2 + 2 * vo_mem.shape[1] * D * 2 + 4 * tm * D * 4 + 5 * tm * D * 4
           + (4 << 20))
    return pl.pallas_call(
        _cross_attn_kernel,
        grid=(B, S // tm),
        in_specs=[pl.BlockSpec((1, tm, D), row),
                  pl.BlockSpec((1, D), lambda b, i: (0, 0)),
                  _resident(wc_q.shape),
                  pl.BlockSpec(gc_q.shape, lambda b, i: (0, 0)),
                  pl.BlockSpec((1, n_mem, D), lambda b, i: (b, 0, 0)),
                  pl.BlockSpec((1, vo_mem.shape[1], D), lambda b, i: (b, 0, 0))],
        out_specs=pl.BlockSpec((1, tm, D), row),
        out_shape=jax.ShapeDtypeStruct((B, S, D), F32),
        compiler_params=_params(("parallel", "parallel"), est),
        name="cross_attn",
    )(x, g_cross, wc_q, gc_q, k_mem, vo_mem)


def _conv_ffn_kernel(x_ref, xp_ref, xn_ref, g_ref, wg_ref, wv_ref, cwg_ref, cwv_ref, cbg_ref, cbv_ref, wd_ref,
                     o_ref, h_ref, *, tm):
    i = pl.program_id(1)
    f = pl.program_id(2)
    rows = tm + 2 * CONV_HALO

    @pl.when(f == 0)
    def _():
        g = g_ref[...]
        h_prev = jnp.where(i > 0, _rms(xp_ref[0], g), 0.0)
        h_next = jnp.where(i < pl.num_programs(1) - 1, _rms(xn_ref[0], g), 0.0)
        h_ref[0:CONV_HALO, :] = h_prev.astype(BF16)
        h_ref[CONV_HALO:CONV_HALO + tm, :] = _rms(x_ref[0], g).astype(BF16)
        h_ref[CONV_HALO + tm:rows, :] = h_next.astype(BF16)
        o_ref[0] = x_ref[0]

    h = h_ref[...]

    def conv(w_ref, cw_ref, cb_ref):
        u = _dot(h, w_ref[...])
        below = pltpu.roll(u, 1, 0)[CONV_HALO:CONV_HALO + tm]
        above = pltpu.roll(u, rows - 1, 0)[CONV_HALO:CONV_HALO + tm]
        mid = u[CONV_HALO:CONV_HALO + tm]
        return below * cw_ref[0:1, :] + mid * cw_ref[1:2, :] + above * cw_ref[2:3, :] + cb_ref[...]

    act = jax.nn.gelu(conv(wg_ref, cwg_ref, cbg_ref)) * conv(wv_ref, cwv_ref, cbv_ref)
    o_ref[0] += _dot(act.astype(BF16), wd_ref[...])


def _conv_ffn(x, g_ffn, w_up, conv_w, conv_b, w_down, *, tm, tf):
    B, S, D = x.shape
    d_ff = w_down.shape[0]
    nf = d_ff // tf
    halo_blocks = tm // CONV_HALO
    n_halo = S // CONV_HALO
    rows = tm + 2 * CONV_HALO
    est = (4 * tm * D * 4 + 2 * 3 * D * tf * 2 + rows * D * 2 + 8 * rows * tf * 4 + 2 * tm * D * 4 + (4 << 20))
    gate = lambda b, i, f: (0, f)
    val = lambda b, i, f: (0, nf + f)
    return pl.pallas_call(
        functools.partial(_conv_ffn_kernel, tm=tm),
        grid=(B, S // tm, nf),
        in_specs=[pl.BlockSpec((1, tm, D), lambda b, i, f: (b, i, 0), pipeline_mode=pl.Buffered(1)),
                  pl.BlockSpec((1, CONV_HALO, D), lambda b, i, f: (b, jnp.maximum(i * halo_blocks - 1, 0), 0)),
                  pl.BlockSpec((1, CONV_HALO, D),
                               lambda b, i, f: (b, jnp.minimum((i + 1) * halo_blocks, n_halo - 1), 0)),
                  pl.BlockSpec((1, D), lambda b, i, f: (0, 0)),
                  pl.BlockSpec((D, tf), gate),
                  pl.BlockSpec((D, tf), val),
                  pl.BlockSpec((conv_w.shape[0], tf), gate),
                  pl.BlockSpec((conv_w.shape[0], tf), val),
                  pl.BlockSpec((1, tf), gate),
                  pl.BlockSpec((1, tf), val),
                  pl.BlockSpec((tf, D), lambda b, i, f: (f, 0))],
        out_specs=pl.BlockSpec((1, tm, D), lambda b, i, f: (b, i, 0)),
        out_shape=jax.ShapeDtypeStruct((B, S, D), F32),
        scratch_shapes=[pltpu.VMEM((rows, D), BF16)],
        compiler_params=_params(("parallel", "parallel", "arbitrary"), est),
        name="conv_ffn",
    )(x, x, x, g_ffn, w_up, w_up, conv_w, conv_w, conv_b, conv_b, w_down)


def _rope_tables(seq, gain, scale):
    half = DIFF_QKDIM // 2
    inv = ROPE_THETA ** (-jnp.arange(half, dtype=F32) / half)
    ang = jnp.arange(seq, dtype=F32)[:, None] * inv[None, :]
    cos = jnp.cos(ang)
    sin = jnp.sin(ang)
    gain = gain.astype(F32)
    a = jnp.concatenate([gain[:half] * cos, gain[half:] * cos], axis=1) * scale
    b = jnp.concatenate([-gain[half:] * sin, gain[:half] * sin], axis=1) * scale
    reps = V7X_LANES // DIFF_QKDIM
    return jnp.tile(a, (1, reps)), jnp.tile(b, (1, reps))


def _segment_ones(width):
    seg = jnp.arange(width) // DIFF_QKDIM
    return (seg[:, None] == seg[None, :]).astype(BF16)


def _tile(n, target):
    t = min(n, target)
    assert n % t == 0, (n, t)
    return t


def _run_trunk(x, mem, layers):
    B, S, D = x.shape
    tm = _tile(S, 512)
    seg = _segment_ones(256)
    for l, p in enumerate(layers):
        lam_init = 0.8 - 0.6 * math.exp(-0.3 * l)
        q_scale = DIFF_QKDIM ** -0.5 * math.log2(math.e)
        aq, bq = _rope_tables(S, p["g_q"], q_scale)
        ak, bk = _rope_tables(S, p["g_k"], 1.0)
        q, k, v, u = _mix_in(x, p["g_mix"], p["w_in"], seg, aq, bq, ak, bk, tm=tm)
        bound = (DIFF_QKDIM * q_scale * 1.01 * jnp.max(jnp.abs(p["g_q"])) * jnp.max(jnp.abs(p["g_k"]))).astype(F32)
        attn = functools.partial(_diff_attn, p["lam"], q, k, v, p["g_sub"].reshape(-1, 1), lam_init,
                                 tq=_tile(S, 512), tk=_tile(S // 4, 512), blocks_per_trip=4)
        attn_bounded = functools.partial(_diff_attn_bounded, p["lam"], q, k, v, p["g_sub"].reshape(-1, 1), lam_init,
                                         tq=_tile(S, 512), tk=_tile(S // 8, 512),
                                         blocks_per_trip=min(16, S // _tile(S // 8, 512)), heads_per_step=2)
        a = lax.cond(bound <= UNSHIFTED_SOFTMAX_SCORE_LIMIT, attn_bounded, attn)
        x = _mix_out(a, u, x, p["w_out_folded"], tm=tm)
        k_mem = _mem_k(mem, p["g_mem"], p["wc_kv"], p["gc_k"])
        vo_mem = _mem_vo(mem, p["g_mem"], p["wc_kv"], p["wc_o"])
        x = _cross_attn(x, p["g_cross"], p["wc_q"], p["gc_q"], k_mem, vo_mem, tm=tm)
        x = _conv_ffn(x, p["g_ffn"], p["w_up"], p["conv_w"], p["conv_b"], p["w_down"],
                      tm=_tile(S, 1024), tf=FFN_CHUNK)
    return x


def kernel(x_prompt, x_sample, mem_prompt, mem_sample, g_mix, w_in, g_q, g_k, lam_q1, lam_k1, lam_q2, lam_k2,
           g_sub, w_pool, pool_scale, w_out, g_cross, g_mem, wc_q, wc_kv, gc_q, gc_k, wc_o, g_ffn, w_up,
           conv_w, conv_b, w_down):
    depth = w_in.shape[0]
    layers = []
    for l in range(depth):
        lam_init = 0.8 - 0.6 * math.exp(-0.3 * l)
        lam = (jnp.exp(jnp.sum(lam_q1[l].astype(F32) * lam_k1[l].astype(F32)))
               - jnp.exp(jnp.sum(lam_q2[l].astype(F32) * lam_k2[l].astype(F32))) + lam_init)
        row = lambda t: t[l].reshape(1, -1).astype(F32)
        layers.append(dict(
            lam=lam.reshape(1).astype(F32),
            g_mix=row(g_mix), g_q=g_q[l], g_k=g_k[l], g_sub=row(g_sub),
            g_cross=row(g_cross), g_mem=row(g_mem), gc_q=row(gc_q), gc_k=row(gc_k), g_ffn=row(g_ffn),
            conv_w=conv_w[l].astype(F32), conv_b=row(conv_b),
            w_in=w_in[l].astype(BF16),
            w_out_folded=_pool_fold(w_pool[l].astype(BF16), row(pool_scale), w_out[l].astype(BF16)),
            wc_q=wc_q[l].astype(BF16), wc_kv=wc_kv[l].astype(BF16), wc_o=wc_o[l].astype(BF16),
            w_up=w_up[l].astype(BF16), w_down=w_down[l].astype(BF16)))
    return (_run_trunk(x_prompt, mem_prompt, layers), _run_trunk(x_sample, mem_sample, layers))
```

```python
import functools
import math

import jax
import jax.numpy as jnp
from jax import lax
from jax.experimental import pallas as pl
from jax.experimental.pallas import tpu as pltpu

F32 = jnp.float32
BF16 = jnp.bfloat16

N_DIFF_HEADS = 8
DIFF_QKDIM = 64
DIFF_VDIM = 128
HEAD_COLS = 2 * DIFF_QKDIM
POOL_WINDOWS = (2, 4, 8, 16)
POOL_GROUP_WIDTH = 256
POOL_HALO = 16
N_CROSS_HEADS = 4
CONV_HALO = 8
FFN_CHUNK = 512
ROPE_THETA = 10000.0
EPS = 1e-6
UNSHIFTED_SOFTMAX_SCORE_LIMIT = 30.0

V7X_VMEM_BYTES = 64 * 1024 * 1024
V7X_LANES = 128
V7X_MXU_WIDTH = 256
VMEM_LIMIT_CAP = V7X_VMEM_BYTES - 6 * 1024 * 1024

TOKEN_TILE = 512
FFN_TOKEN_TILE = 1024
ATTN_Q_TILE = 512
ATTN_K_TILE = 512
ATTN_BLOCKS_PER_TRIP = 16
ATTN_HEADS_PER_STEP = 2
FLASH_BLOCKS_PER_TRIP = 4


def _vmem_limit(estimate_bytes):
    return int(min(VMEM_LIMIT_CAP, max(32 * 1024 * 1024, estimate_bytes)))


def _params(semantics, vmem_estimate):
    return pltpu.CompilerParams(dimension_semantics=semantics, vmem_limit_bytes=_vmem_limit(vmem_estimate))


def _resident(shape):
    return pl.BlockSpec(shape, lambda *_: (0,) * len(shape), pipeline_mode=pl.Buffered(1))


def _rms(x, gain):
    ms = jnp.mean(x * x, axis=-1, keepdims=True)
    return x * lax.rsqrt(ms + EPS) * gain


def _dot(a, b):
    return jnp.dot(a, b, preferred_element_type=F32)


def _dot_nt(a, b):
    return lax.dot_general(a, b, (((1,), (1,)), ((), ())), preferred_element_type=F32)


def _mix_in_kernel(x_ref, g_ref, w_ref, seg_ref, aq_ref, bq_ref, ak_ref, bk_ref,
                   q_ref, k_ref, v_ref, u_ref, *, qk_width, v_width):
    h = _rms(x_ref[0], g_ref[...]).astype(BF16)
    lane = lax.broadcasted_iota(jnp.int32, (1, V7X_LANES), 1)
    partner_is_above = (lane & (DIFF_QKDIM // 2)) == 0
    seg = seg_ref[...]
    chunk = seg.shape[0]
    for col0, a_ref, b_ref, o_ref in ((0, aq_ref, bq_ref, q_ref), (qk_width, ak_ref, bk_ref, k_ref)):
        a = a_ref[...]
        b = b_ref[...]
        zfull = _dot(h, w_ref[:, col0:col0 + qk_width])
        for c0 in range(0, qk_width, chunk):
            z = zfull[:, c0:c0 + chunk]
            ss = _dot((z * z).astype(BF16), seg)
            zn = z * lax.rsqrt(ss * (1.0 / DIFF_QKDIM) + EPS)
            for c in range(0, chunk, V7X_LANES):
                zc = zn[:, c:c + V7X_LANES]
                partner = jnp.where(partner_is_above,
                                    pltpu.roll(zc, V7X_LANES - DIFF_QKDIM // 2, 1),
                                    pltpu.roll(zc, DIFF_QKDIM // 2, 1))
                o_ref[0, :, c0 + c:c0 + c + V7X_LANES] = (zc * a + partner * b).astype(o_ref.dtype)
    v0 = 2 * qk_width
    v_ref[0] = _dot(h, w_ref[:, v0:v0 + v_width]).astype(v_ref.dtype)
    u_ref[0] = _dot(h, w_ref[:, v0 + v_width:])


def _mix_in(x, g, w_in, seg, aq, bq, ak, bk, *, tm):
    B, S, D = x.shape
    qk_width = N_DIFF_HEADS * HEAD_COLS
    v_width = N_DIFF_HEADS * DIFF_VDIM
    u_width = w_in.shape[1] - 2 * qk_width - v_width
    row = lambda b, i: (b, i, 0)
    tab = pl.BlockSpec((tm, V7X_LANES), lambda b, i: (i, 0))
    est = (w_in.size * 2 + 2 * tm * D * 4 + 2 * tm * (2 * qk_width + v_width) * 2 + 2 * tm * u_width * 4
           + tm * D * 2 + 8 * tm * 1024 * 4 + 8 * tm * V7X_LANES * 4 + (4 << 20))
    return pl.pallas_call(
        functools.partial(_mix_in_kernel, qk_width=qk_width, v_width=v_width),
        grid=(B, S // tm),
        in_specs=[pl.BlockSpec((1, tm, D), row),
                  pl.BlockSpec((1, D), lambda b, i: (0, 0)),
                  _resident(w_in.shape),
                  pl.BlockSpec(seg.shape, lambda b, i: (0, 0)),
                  tab, tab, tab, tab],
        out_specs=[pl.BlockSpec((1, tm, qk_width), row),
                   pl.BlockSpec((1, tm, qk_width), row),
                   pl.BlockSpec((1, tm, v_width), row),
                   pl.BlockSpec((1, tm, u_width), row)],
        out_shape=[jax.ShapeDtypeStruct((B, S, qk_width), BF16),
                   jax.ShapeDtypeStruct((B, S, qk_width), BF16),
                   jax.ShapeDtypeStruct((B, S, v_width), BF16),
                   jax.ShapeDtypeStruct((B, S, u_width), F32)],
        compiler_params=_params(("parallel", "parallel"), est),
        name="mix_in",
    )(x, g, w_in, seg, aq, bq, ak, bk)


def _dot_tn(a, b):
    return lax.dot_general(a, b, (((0,), (0,)), ((), ())), preferred_element_type=F32)


def _diff_attn_kernel(lam_ref, q_ref, k_ref, v_ref, gsub_ref, o_ref, qbd_ref, s0_ref, s1_ref, m_ref, l_ref, acc_ref,
                      *, tq, tk, blocks_per_trip, out_scale):
    qt = q_ref[0].astype(F32).T
    row = lax.broadcasted_iota(jnp.int32, (HEAD_COLS, 1), 0)
    zero = jnp.zeros_like(qt)
    qbd_ref[:, 0:tq] = jnp.where(row < DIFF_QKDIM, qt, zero).astype(BF16)
    qbd_ref[:, tq:2 * tq] = jnp.where(row >= DIFF_QKDIM, qt, zero).astype(BF16)
    m_ref[...] = jnp.full(m_ref.shape, -jnp.inf, F32)
    l_ref[...] = jnp.zeros(l_ref.shape, F32)
    acc_ref[...] = jnp.zeros(acc_ref.shape, F32)

    def scores(j, s_ref):
        k0 = pl.multiple_of(j * tk, tk)
        s_ref[...] = _dot(k_ref[0, pl.ds(k0, tk), :], qbd_ref[...])

    def absorb(j, s_ref):
        k0 = pl.multiple_of(j * tk, tk)
        vb = v_ref[0, pl.ds(k0, tk), :]
        s = s_ref[...]
        m_prev = m_ref[...]
        m_new = jnp.maximum(m_prev, jnp.max(s, axis=0, keepdims=True))
        alpha = jnp.exp2(m_prev - m_new)
        p = jnp.exp2(s - m_new)
        l_ref[...] = alpha * l_ref[...] + jnp.sum(p, axis=0, keepdims=True)
        m_ref[...] = m_new
        pb = p.astype(BF16)
        for c in range(2):
            cols = slice(c * tq, (c + 1) * tq)
            acc_ref[c] = alpha[:, cols] * acc_ref[c] + _dot_tn(vb, pb[:, cols])

    bufs = (s0_ref, s1_ref)
    n_trips = k_ref.shape[1] // (blocks_per_trip * tk)
    scores(0, s0_ref)

    def trip(i, carry):
        j = blocks_per_trip * i
        for u in range(blocks_per_trip):
            scores(j + u + 1, bufs[(u + 1) % 2])
            absorb(j + u, bufs[u % 2])
        return carry

    lax.fori_loop(0, n_trips - 1, trip, 0)
    j_last = blocks_per_trip * (n_trips - 1)
    for u in range(blocks_per_trip):
        if u + 1 < blocks_per_trip:
            scores(j_last + u + 1, bufs[(u + 1) % 2])
        absorb(j_last + u, bufs[u % 2])

    inv = 1.0 / l_ref[...]
    o = acc_ref[0] * inv[:, 0:tq] - lam_ref[0] * (acc_ref[1] * inv[:, tq:2 * tq])
    ms = jnp.mean(o * o, axis=0, keepdims=True)
    o = o * lax.rsqrt(ms + EPS) * gsub_ref[...] * out_scale
    o_ref[0] = o.T.astype(o_ref.dtype)


def _diff_attn(lam, q, k, v, g_sub_col, lam_init, *, tq, tk, blocks_per_trip):
    B, S, _ = q.shape
    assert blocks_per_trip % 2 == 0 and S % (blocks_per_trip * tk) == 0, (S, tk, blocks_per_trip)
    est = (2 * 2 * S * (HEAD_COLS + DIFF_VDIM) * 2 + 4 * tq * HEAD_COLS * 2 + 2 * tq * HEAD_COLS * 2
           + 2 * tq * DIFF_VDIM * 4 + 8 * tk * 2 * tq * 4 + (4 << 20))
    return pl.pallas_call(
        functools.partial(_diff_attn_kernel, tq=tq, tk=tk, blocks_per_trip=blocks_per_trip,
                          out_scale=1.0 - lam_init),
        grid=(B, N_DIFF_HEADS, S // tq),
        in_specs=[pl.BlockSpec(memory_space=pltpu.SMEM),
                  pl.BlockSpec((1, tq, HEAD_COLS), lambda b, h, i: (b, i, h)),
                  pl.BlockSpec((1, S, HEAD_COLS), lambda b, h, i: (b, 0, h)),
                  pl.BlockSpec((1, S, DIFF_VDIM), lambda b, h, i: (b, 0, h)),
                  pl.BlockSpec((DIFF_VDIM, 1), lambda b, h, i: (0, 0))],
        out_specs=pl.BlockSpec((1, tq, DIFF_VDIM), lambda b, h, i: (b, i, h)),
        out_shape=jax.ShapeDtypeStruct((B, S, N_DIFF_HEADS * DIFF_VDIM), BF16),
        scratch_shapes=[pltpu.VMEM((HEAD_COLS, 2 * tq), BF16),
                        pltpu.VMEM((tk, 2 * tq), F32),
                        pltpu.VMEM((tk, 2 * tq), F32),
                        pltpu.VMEM((1, 2 * tq), F32),
                        pltpu.VMEM((1, 2 * tq), F32),
                        pltpu.VMEM((2, DIFF_VDIM, tq), F32)],
        compiler_params=_params(("parallel", "parallel", "parallel"), est),
        name="diff_attn",
    )(lam, q, k, v, g_sub_col)


def _diff_attn_bounded_kernel(lam_ref, q_ref, k_ref, v_ref, gsub_ref, o_ref, qbd_ref, l_ref, acc_ref,
                              *, tq, tk, blocks_per_trip, out_scale):
    row = lax.broadcasted_iota(jnp.int32, (HEAD_COLS, 1), 0)
    for hh in range(q_ref.shape[2] // HEAD_COLS):
        qk_cols = slice(hh * HEAD_COLS, (hh + 1) * HEAD_COLS)
        v_cols = slice(hh * DIFF_VDIM, (hh + 1) * DIFF_VDIM)
        qt = q_ref[0, :, qk_cols].astype(F32).T
        zero = jnp.zeros_like(qt)
        qbd_ref[hh, :, 0:tq] = jnp.where(row < DIFF_QKDIM, qt, zero).astype(BF16)
        qbd_ref[hh, :, tq:2 * tq] = jnp.where(row >= DIFF_QKDIM, qt, zero).astype(BF16)
        l_ref[hh] = jnp.zeros(l_ref.shape[1:], F32)
        acc_ref[hh] = jnp.zeros(acc_ref.shape[1:], F32)

        def trip(i, carry):
            for u in range(blocks_per_trip):
                k0 = pl.multiple_of((blocks_per_trip * i + u) * tk, tk)
                vb = v_ref[0, pl.ds(k0, tk), v_cols]
                p = jnp.exp2(_dot(k_ref[0, pl.ds(k0, tk), qk_cols], qbd_ref[hh]))
                l_ref[hh] += jnp.sum(p, axis=0, keepdims=True)
                pb = p.astype(BF16)
                for c in range(2):
                    acc_ref[hh, c] += _dot_tn(vb, pb[:, c * tq:(c + 1) * tq])
            return carry

        lax.fori_loop(0, k_ref.shape[1] // (blocks_per_trip * tk), trip, 0)

        inv = 1.0 / l_ref[hh]
        o = acc_ref[hh, 0] * inv[:, 0:tq] - lam_ref[0] * (acc_ref[hh, 1] * inv[:, tq:2 * tq])
        ms = jnp.mean(o * o, axis=0, keepdims=True)
        o = o * lax.rsqrt(ms + EPS) * gsub_ref[...] * out_scale
        o_ref[0, :, v_cols] = o.T.astype(o_ref.dtype)


def _diff_attn_bounded(lam, q, k, v, g_sub_col, lam_init, *, tq, tk, blocks_per_trip, heads_per_step):
    B, S, _ = q.shape
    assert S % (blocks_per_trip * tk) == 0, (S, tk, blocks_per_trip)
    assert N_DIFF_HEADS % heads_per_step == 0
    hps = heads_per_step
    est = (hps * (2 * 2 * S * (HEAD_COLS + DIFF_VDIM) * 2 + 4 * tq * HEAD_COLS * 2 + 2 * tq * HEAD_COLS * 2
                  + 2 * tq * DIFF_VDIM * 4) + 8 * tk * 2 * tq * 4 + (4 << 20))
    return pl.pallas_call(
        functools.partial(_diff_attn_bounded_kernel, tq=tq, tk=tk, blocks_per_trip=blocks_per_trip,
                          out_scale=1.0 - lam_init),
        grid=(B, N_DIFF_HEADS // hps, S // tq),
        in_specs=[pl.BlockSpec(memory_space=pltpu.SMEM),
                  pl.BlockSpec((1, tq, hps * HEAD_COLS), lambda b, h, i: (b, i, h)),
                  pl.BlockSpec((1, S, hps * HEAD_COLS), lambda b, h, i: (b, 0, h)),
                  pl.BlockSpec((1, S, hps * DIFF_VDIM), lambda b, h, i: (b, 0, h)),
                  pl.BlockSpec((DIFF_VDIM, 1), lambda b, h, i: (0, 0))],
        out_specs=pl.BlockSpec((1, tq, hps * DIFF_VDIM), lambda b, h, i: (b, i, h)),
        out_shape=jax.ShapeDtypeStruct((B, S, N_DIFF_HEADS * DIFF_VDIM), BF16),
        scratch_shapes=[pltpu.VMEM((hps, HEAD_COLS, 2 * tq), BF16),
                        pltpu.VMEM((hps, 1, 2 * tq), F32),
                        pltpu.VMEM((hps, 2, DIFF_VDIM, tq), F32)],
        compiler_params=_params(("parallel", "parallel", "parallel"), est),
        name="diff_attn_bounded",
    )(lam, q, k, v, g_sub_col)


def _pool_fold_kernel(wp_ref, ps_ref, wo_ref, o_ref):
    w = (wp_ref[0].astype(F32) * ps_ref[...]).astype(BF16)
    o_ref[...] = _dot(w, wo_ref[...]).astype(o_ref.dtype)


def _pool_fold(w_pool, pool_scale, w_out):
    groups, gw, _ = w_pool.shape
    D = w_out.shape[1]
    a_width = w_out.shape[0] - groups * gw
    folded = pl.pallas_call(
        _pool_fold_kernel,
        grid=(groups,),
        in_specs=[pl.BlockSpec((1, gw, gw), lambda g: (g, 0, 0)),
                  pl.BlockSpec((1, gw), lambda g: (0, g)),
                  pl.BlockSpec((gw, D), lambda g: (a_width // gw + g, 0))],
        out_specs=pl.BlockSpec((gw, D), lambda g: (g, 0)),
        out_shape=jax.ShapeDtypeStruct((groups * gw, D), BF16),
        compiler_params=_params(("parallel",), 8 * gw * D * 4 + (4 << 20)),
        name="pool_fold",
    )(w_pool, pool_scale, w_out)
    return jnp.concatenate([w_out[:a_width], folded], axis=0)


def _mix_out_kernel(a_ref, u_ref, up_ref, un_ref, x_ref, wo_ref, o_ref, *, tm, seq):
    i = pl.program_id(1)
    u_prev = jnp.where(i > 0, up_ref[0], 0.0)
    u_next = jnp.where(i < pl.num_programs(1) - 1, un_ref[0], 0.0)
    ue = jnp.concatenate([u_prev, u_ref[0], u_next], axis=0)
    rows = tm + 2 * POOL_HALO
    pos = i * tm + lax.broadcasted_iota(jnp.int32, (tm, 1), 0)
    a_width = a_ref.shape[2]
    acc = x_ref[0] + _dot(a_ref[0], wo_ref[0:a_width, :])
    pooled = []
    for g, w in enumerate(POOL_WINDOWS):
        c0 = g * POOL_GROUP_WIDTH
        ug = ue[:, c0:c0 + POOL_GROUP_WIDTH]
        win = ug + pltpu.roll(ug, 1, 0)
        shift = 1
        while 2 * shift < w:
            win = pltpu.roll(win, shift, 0) + pltpu.roll(win, rows - shift, 0)
            shift *= 2
        win = win[POOL_HALO:POOL_HALO + tm]
        cnt = jnp.minimum(pos + w // 2, seq) - jnp.maximum(pos - w // 2, 0)
        z = win / cnt.astype(F32) - ug[POOL_HALO:POOL_HALO + tm]
        pooled.append(z.astype(BF16))
    o_ref[0] = acc + _dot(jnp.concatenate(pooled, axis=1), wo_ref[a_width:, :])


def _mix_out(a, u, x, w_out, *, tm):
    B, S, D = x.shape
    a_width, u_width = a.shape[2], u.shape[2]
    halo_blocks = tm // POOL_HALO
    n_halo = S // POOL_HALO
    row = lambda b, i: (b, i, 0)
    est = (w_out.size * 2 + 4 * tm * D * 4 + 2 * tm * a_width * 2 + 2 * tm * u_width * 4
           + 6 * tm * u_width * 4 + 2 * tm * D * 4 + (4 << 20))
    return pl.pallas_call(
        functools.partial(_mix_out_kernel, tm=tm, seq=S),
        grid=(B, S // tm),
        in_specs=[pl.BlockSpec((1, tm, a_width), row),
                  pl.BlockSpec((1, tm, u_width), row),
                  pl.BlockSpec((1, POOL_HALO, u_width),
                               lambda b, i: (b, jnp.maximum(i * halo_blocks - 1, 0), 0)),
                  pl.BlockSpec((1, POOL_HALO, u_width),
                               lambda b, i: (b, jnp.minimum((i + 1) * halo_blocks, n_halo - 1), 0)),
                  pl.BlockSpec((1, tm, D), row),
                  _resident(w_out.shape)],
        out_specs=pl.BlockSpec((1, tm, D), row),
        out_shape=jax.ShapeDtypeStruct((B, S, D), F32),
        compiler_params=_params(("parallel", "parallel"), est),
        name="mix_out",
    )(a, u, u, u, x, w_out)


def _mem_k_kernel(mem_ref, g_ref, w_ref, gk_ref, o_ref):
    m = _rms(mem_ref[0], g_ref[...]).astype(BF16)
    k = _dot(m, w_ref[...])
    o_ref[0] = _rms(k, gk_ref[...]).astype(o_ref.dtype)


def _mem_k(mem, g_mem, wc_kv, gc_k):
    B, n_mem, D = mem.shape
    hd = D // N_CROSS_HEADS
    est = 2 * n_mem * D * 4 + 2 * D * hd * 2 + 4 * n_mem * hd * 4 + n_mem * D * 4 + (4 << 20)
    return pl.pallas_call(
        _mem_k_kernel,
        grid=(N_CROSS_HEADS, B),
        in_specs=[pl.BlockSpec((1, n_mem, D), lambda h, b: (b, 0, 0)),
                  pl.BlockSpec((1, D), lambda h, b: (0, 0)),
                  pl.BlockSpec((D, hd), lambda h, b: (0, h)),
                  pl.BlockSpec((1, hd), lambda h, b: (0, 0))],
        out_specs=pl.BlockSpec((1, n_mem, hd), lambda h, b: (b, 0, h)),
        out_shape=jax.ShapeDtypeStruct((B, n_mem, D), BF16),
        compiler_params=_params(("parallel", "parallel"), est),
        name="mem_k",
    )(mem, g_mem, wc_kv, gc_k)


def _mem_vo_kernel(mem_ref, g_ref, w_ref, wo_ref, o_ref):
    m = _rms(mem_ref[0], g_ref[...]).astype(BF16)
    v = _dot(m, w_ref[...]).astype(BF16)
    o_ref[0] = _dot(v, wo_ref[...]).astype(o_ref.dtype)


def _mem_vo(mem, g_mem, wc_kv, wc_o):
    B, n_mem, D = mem.shape
    hd = D // N_CROSS_HEADS
    est = (2 * n_mem * D * 4 + 2 * D * hd * 2 + 2 * hd * D * 2 + 2 * n_mem * D * 2 + 4 * n_mem * D * 4 + (4 << 20))
    return pl.pallas_call(
        _mem_vo_kernel,
        grid=(N_CROSS_HEADS, B),
        in_specs=[pl.BlockSpec((1, n_mem, D), lambda h, b: (b, 0, 0)),
                  pl.BlockSpec((1, D), lambda h, b: (0, 0)),
                  pl.BlockSpec((D, hd), lambda h, b: (0, N_CROSS_HEADS + h)),
                  pl.BlockSpec((hd, D), lambda h, b: (h, 0))],
        out_specs=pl.BlockSpec((1, n_mem, D), lambda h, b: (b, h, 0)),
        out_shape=jax.ShapeDtypeStruct((B, N_CROSS_HEADS * n_mem, D), BF16),
        compiler_params=_params(("parallel", "parallel"), est),
        name="mem_vo",
    )(mem, g_mem, wc_kv, wc_o)


def _cross_attn_kernel(x_ref, g_ref, wq_ref, gq_ref, k_ref, vo_ref, gn_ref, o_ref, hn_ref):
    x = x_ref[0]
    h = _rms(x, g_ref[...]).astype(BF16)
    hd = gq_ref.shape[1]
    scale = hd ** -0.5
    heads = [slice(c0, c0 + hd) for c0 in range(0, x.shape[1], hd)]
    qs = [_dot(h, wq_ref[:, c]) for c in heads]
    qn = [_rms(q, gq_ref[...]).astype(BF16) for q in qs]
    ss = [_dot_nt(q, k_ref[0, :, c]) * scale for q, c in zip(qn, heads)]
    ps = []
    for s in ss:
        p = jnp.exp(s - jnp.max(s, axis=-1, keepdims=True))
        ps.append((p / jnp.sum(p, axis=-1, keepdims=True)).astype(BF16))
    out = x + _dot(jnp.concatenate(ps, axis=1), vo_ref[0])
    o_ref[0] = out
    hn_ref[0] = _rms(out, gn_ref[...]).astype(hn_ref.dtype)


def _cross_attn(x, g_cross, wc_q, gc_q, k_mem, vo_mem, g_next, *, tm):
    B, S, D = x.shape
    n_mem = k_mem.shape[1]
    row = lambda b, i: (b, i, 0)
    est = (wc_q.size * 2 + 2 * n_mem * D * 2 + 2 * vo_mem.shape[1] * D * 2 + 4 * tm * D * 4 + 5 * tm * D * 4
           + (4 << 20))
    return pl.pallas_call(
        _cross_attn_kernel,
        grid=(B, S // tm),
        in_specs=[pl.BlockSpec((1, tm, D), row),
                  pl.BlockSpec((1, D), lambda b, i: (0, 0)),
                  _resident(wc_q.shape),
                  pl.BlockSpec(gc_q.shape, lambda b, i: (0, 0)),
                  pl.BlockSpec((1, n_mem, D), lambda b, i: (b, 0, 0)),
                  pl.BlockSpec((1, vo_mem.shape[1], D), lambda b, i: (b, 0, 0)),
                  pl.BlockSpec((1, D), lambda b, i: (0, 0))],
        out_specs=[pl.BlockSpec((1, tm, D), row), pl.BlockSpec((1, tm, D), row)],
        out_shape=[jax.ShapeDtypeStruct((B, S, D), F32), jax.ShapeDtypeStruct((B, S, D), BF16)],
        compiler_params=_params(("parallel", "parallel"), est),
        name="cross_attn",
    )(x, g_cross, wc_q, gc_q, k_mem, vo_mem, g_next)


def _conv_ffn_kernel(hin_ref, xr_ref, xp_ref, xn_ref, g_ref, wg_ref, wv_ref, cwg_ref, cwv_ref, cbg_ref, cbv_ref,
                     wd_ref, o_ref, h_ref, *, tm, res_chunks):
    i = pl.program_id(1)
    f = pl.program_id(2)
    rows = tm + 2 * CONV_HALO
    slab = tm // res_chunks

    @pl.when(f == 0)
    def _():
        g = g_ref[...]
        h_prev = jnp.where(i > 0, _rms(xp_ref[0], g), 0.0)
        h_next = jnp.where(i < pl.num_programs(1) - 1, _rms(xn_ref[0], g), 0.0)
        h_ref[0:CONV_HALO, :] = h_prev.astype(BF16)
        h_ref[CONV_HALO:CONV_HALO + tm, :] = hin_ref[0]
        h_ref[CONV_HALO + tm:rows, :] = h_next.astype(BF16)
        o_ref[0] = jnp.zeros(o_ref.shape[1:], o_ref.dtype)

    h = h_ref[...]

    def conv(w_ref, cw_ref, cb_ref):
        u = _dot(h, w_ref[...])
        below = pltpu.roll(u, 1, 0)[CONV_HALO:CONV_HALO + tm]
        above = pltpu.roll(u, rows - 1, 0)[CONV_HALO:CONV_HALO + tm]
        mid = u[CONV_HALO:CONV_HALO + tm]
        return below * cw_ref[0:1, :] + mid * cw_ref[1:2, :] + above * cw_ref[2:3, :] + cb_ref[...]

    act = jax.nn.gelu(conv(wg_ref, cwg_ref, cbg_ref)) * conv(wv_ref, cwv_ref, cbv_ref)
    o_ref[0] += _dot(act.astype(BF16), wd_ref[...])

    @pl.when(f < res_chunks)
    def _():
        r0 = pl.multiple_of(f * slab, slab)
        o_ref[0, pl.ds(r0, slab), :] += xr_ref[0]


def _conv_ffn(x, h, g_ffn, w_up, conv_w, conv_b, w_down, *, tm, tf):
    B, S, D = x.shape
    d_ff = w_down.shape[0]
    nf = d_ff // tf
    halo_blocks = tm // CONV_HALO
    n_halo = S // CONV_HALO
    rows = tm + 2 * CONV_HALO
    res_chunks = min(nf, 8)
    assert tm % (res_chunks * CONV_HALO) == 0, (tm, res_chunks)
    est = (2 * tm * D * 4 + 2 * tm * D * 2 + 2 * 3 * D * tf * 2 + rows * D * 2 + 8 * rows * tf * 4 + 2 * tm * D * 4
           + (4 << 20))
    gate = lambda b, i, f: (0, f)
    val = lambda b, i, f: (0, nf + f)
    return pl.pallas_call(
        functools.partial(_conv_ffn_kernel, tm=tm, res_chunks=res_chunks),
        grid=(B, S // tm, nf),
        in_specs=[pl.BlockSpec((1, tm, D), lambda b, i, f: (b, i, 0)),
                  pl.BlockSpec((1, tm // res_chunks, D),
                               lambda b, i, f: (b, i * res_chunks + jnp.minimum(f, res_chunks - 1), 0)),
                  pl.BlockSpec((1, CONV_HALO, D), lambda b, i, f: (b, jnp.maximum(i * halo_blocks - 1, 0), 0)),
                  pl.BlockSpec((1, CONV_HALO, D),
                               lambda b, i, f: (b, jnp.minimum((i + 1) * halo_blocks, n_halo - 1), 0)),
                  pl.BlockSpec((1, D), lambda b, i, f: (0, 0)),
                  pl.BlockSpec((D, tf), gate),
                  pl.BlockSpec((D, tf), val),
                  pl.BlockSpec((conv_w.shape[0], tf), gate),
                  pl.BlockSpec((conv_w.shape[0], tf), val),
                  pl.BlockSpec((1, tf), gate),
                  pl.BlockSpec((1, tf), val),
                  pl.BlockSpec((tf, D), lambda b, i, f: (f, 0))],
        out_specs=pl.BlockSpec((1, tm, D), lambda b, i, f: (b, i, 0)),
        out_shape=jax.ShapeDtypeStruct((B, S, D), F32),
        scratch_shapes=[pltpu.VMEM((rows, D), BF16)],
        compiler_params=_params(("parallel", "parallel", "arbitrary"), est),
        name="conv_ffn",
    )(h, x, x, x, g_ffn, w_up, w_up, conv_w, conv_w, conv_b, conv_b, w_down)


def _rope_tables(seq, gain, scale):
    half = DIFF_QKDIM // 2
    inv = ROPE_THETA ** (-jnp.arange(half, dtype=F32) / half)
    ang = jnp.arange(seq, dtype=F32)[:, None] * inv[None, :]
    cos = jnp.cos(ang)
    sin = jnp.sin(ang)
    gain = gain.astype(F32)
    a = jnp.concatenate([gain[:half] * cos, gain[half:] * cos], axis=1) * scale
    b = jnp.concatenate([-gain[half:] * sin, gain[:half] * sin], axis=1) * scale
    reps = V7X_LANES // DIFF_QKDIM
    return jnp.tile(a, (1, reps)), jnp.tile(b, (1, reps))


def _segment_ones(width):
    seg = jnp.arange(width) // DIFF_QKDIM
    return (seg[:, None] == seg[None, :]).astype(BF16)


def _tile(n, target):
    t = min(n, target)
    assert n % t == 0, (n, t)
    return t


def _run_trunk(x, mem, layers):
    B, S, D = x.shape
    tm = _tile(S, TOKEN_TILE)
    tq = _tile(S, ATTN_Q_TILE)
    flash_tk = _tile(S // FLASH_BLOCKS_PER_TRIP, ATTN_K_TILE)
    tk = _tile(S // 8, ATTN_K_TILE)
    seg = _segment_ones(V7X_MXU_WIDTH)
    for l, p in enumerate(layers):
        lam_init = 0.8 - 0.6 * math.exp(-0.3 * l)
        q_scale = DIFF_QKDIM ** -0.5 * math.log2(math.e)
        aq, bq = _rope_tables(S, p["g_q"], q_scale)
        ak, bk = _rope_tables(S, p["g_k"], 1.0)
        q, k, v, u = _mix_in(x, p["g_mix"], p["w_in"], seg, aq, bq, ak, bk, tm=tm)
        bound = (DIFF_QKDIM * q_scale * 1.01 * jnp.max(jnp.abs(p["g_q"])) * jnp.max(jnp.abs(p["g_k"]))).astype(F32)
        attn = functools.partial(_diff_attn, p["lam"], q, k, v, p["g_sub"].reshape(-1, 1), lam_init,
                                 tq=tq, tk=flash_tk, blocks_per_trip=FLASH_BLOCKS_PER_TRIP)
        attn_bounded = functools.partial(_diff_attn_bounded, p["lam"], q, k, v, p["g_sub"].reshape(-1, 1), lam_init,
                                         tq=tq, tk=tk, blocks_per_trip=min(ATTN_BLOCKS_PER_TRIP, S // tk),
                                         heads_per_step=ATTN_HEADS_PER_STEP)
        a = lax.cond(bound <= UNSHIFTED_SOFTMAX_SCORE_LIMIT, attn_bounded, attn)
        x = _mix_out(a, u, x, p["w_out_folded"], tm=tm)
        k_mem = _mem_k(mem, p["g_mem"], p["wc_kv"], p["gc_k"])
        vo_mem = _mem_vo(mem, p["g_mem"], p["wc_kv"], p["wc_o"])
        x, h_ffn = _cross_attn(x, p["g_cross"], p["wc_q"], p["gc_q"], k_mem, vo_mem, p["g_ffn"], tm=tm)
        x = _conv_ffn(x, h_ffn, p["g_ffn"], p["w_up"], p["conv_w"], p["conv_b"], p["w_down"],
                      tm=_tile(S, FFN_TOKEN_TILE), tf=FFN_CHUNK)
    return x


def kernel(x_prompt, x_sample, mem_prompt, mem_sample, g_mix, w_in, g_q, g_k, lam_q1, lam_k1, lam_q2, lam_k2,
           g_sub, w_pool, pool_scale, w_out, g_cross, g_mem, wc_q, wc_kv, gc_q, gc_k, wc_o, g_ffn, w_up,
           conv_w, conv_b, w_down):
    depth = w_in.shape[0]
    layers = []
    for l in range(depth):
        lam_init = 0.8 - 0.6 * math.exp(-0.3 * l)
        lam = (jnp.exp(jnp.sum(lam_q1[l].astype(F32) * lam_k1[l].astype(F32)))
               - jnp.exp(jnp.sum(lam_q2[l].astype(F32) * lam_k2[l].astype(F32))) + lam_init)
        row = lambda t: t[l].reshape(1, -1).astype(F32)
        layers.append(dict(
            lam=lam.reshape(1).astype(F32),
            g_mix=row(g_mix), g_q=g_q[l], g_k=g_k[l], g_sub=row(g_sub),
            g_cross=row(g_cross), g_mem=row(g_mem), gc_q=row(gc_q), gc_k=row(gc_k), g_ffn=row(g_ffn),
            conv_w=conv_w[l].astype(F32), conv_b=row(conv_b),
            w_in=w_in[l].astype(BF16),
            w_out_folded=_pool_fold(w_pool[l].astype(BF16), row(pool_scale), w_out[l].astype(BF16)),
            wc_q=wc_q[l].astype(BF16), wc_kv=wc_kv[l].astype(BF16), wc_o=wc_o[l].astype(BF16),
            w_up=w_up[l].astype(BF16), w_down=w_down[l].astype(BF16)))
    return (_run_trunk(x_prompt, mem_prompt, layers), _run_trunk(x_sample, mem_sample, layers))
```

```python
import functools
import math

import jax
import jax.numpy as jnp
from jax import lax
from jax.experimental import pallas as pl
from jax.experimental.pallas import tpu as pltpu

F32 = jnp.float32
BF16 = jnp.bfloat16

N_DIFF_HEADS = 8
DIFF_QKDIM = 64
DIFF_VDIM = 128
HEAD_COLS = 2 * DIFF_QKDIM
POOL_WINDOWS = (2, 4, 8, 16)
POOL_GROUP_WIDTH = 256
POOL_HALO = 16
N_CROSS_HEADS = 4
CONV_HALO = 8
FFN_CHUNK = 512
ROPE_THETA = 10000.0
EPS = 1e-6
UNSHIFTED_SOFTMAX_SCORE_LIMIT = 30.0

V7X_VMEM_BYTES = 64 * 1024 * 1024
V7X_LANES = 128
V7X_MXU_WIDTH = 256
VMEM_LIMIT_CAP = V7X_VMEM_BYTES - 6 * 1024 * 1024

TOKEN_TILE = 512
FFN_TOKEN_TILE = 1024
ATTN_Q_TILE = 512
ATTN_K_TILE = 512
ATTN_BLOCKS_PER_TRIP = 16
ATTN_HEADS_PER_STEP = 4
FLASH_BLOCKS_PER_TRIP = 4


def _vmem_limit(estimate_bytes):
    return int(min(VMEM_LIMIT_CAP, max(32 * 1024 * 1024, estimate_bytes)))


def _params(semantics, vmem_estimate):
    return pltpu.CompilerParams(dimension_semantics=semantics, vmem_limit_bytes=_vmem_limit(vmem_estimate))


def _resident(shape):
    return pl.BlockSpec(shape, lambda *_: (0,) * len(shape), pipeline_mode=pl.Buffered(1))


def _rms(x, gain):
    ms = jnp.mean(x * x, axis=-1, keepdims=True)
    return x * lax.rsqrt(ms + EPS) * gain


def _dot(a, b):
    return jnp.dot(a, b, preferred_element_type=F32)


def _dot_nt(a, b):
    return lax.dot_general(a, b, (((1,), (1,)), ((), ())), preferred_element_type=F32)


def _mix_in_kernel(x_ref, g_ref, w_ref, seg_ref, aq_ref, bq_ref, ak_ref, bk_ref,
                   q_ref, k_ref, v_ref, u_ref, *, qk_width, v_width):
    h = _rms(x_ref[0], g_ref[...]).astype(BF16)
    lane = lax.broadcasted_iota(jnp.int32, (1, V7X_LANES), 1)
    partner_is_above = (lane & (DIFF_QKDIM // 2)) == 0
    seg = seg_ref[...]
    chunk = seg.shape[0]
    for col0, a_ref, b_ref, o_ref in ((0, aq_ref, bq_ref, q_ref), (qk_width, ak_ref, bk_ref, k_ref)):
        a = a_ref[...]
        b = b_ref[...]
        zfull = _dot(h, w_ref[:, col0:col0 + qk_width])
        for c0 in range(0, qk_width, chunk):
            z = zfull[:, c0:c0 + chunk]
            ss = _dot((z * z).astype(BF16), seg)
            zn = z * lax.rsqrt(ss * (1.0 / DIFF_QKDIM) + EPS)
            for c in range(0, chunk, V7X_LANES):
                zc = zn[:, c:c + V7X_LANES]
                partner = jnp.where(partner_is_above,
                                    pltpu.roll(zc, V7X_LANES - DIFF_QKDIM // 2, 1),
                                    pltpu.roll(zc, DIFF_QKDIM // 2, 1))
                o_ref[0, :, c0 + c:c0 + c + V7X_LANES] = (zc * a + partner * b).astype(o_ref.dtype)
    v0 = 2 * qk_width
    v_ref[0] = _dot(h, w_ref[:, v0:v0 + v_width]).astype(v_ref.dtype)
    u_ref[0] = _dot(h, w_ref[:, v0 + v_width:])


def _mix_in(x, g, w_in, seg, aq, bq, ak, bk, *, tm):
    B, S, D = x.shape
    qk_width = N_DIFF_HEADS * HEAD_COLS
    v_width = N_DIFF_HEADS * DIFF_VDIM
    u_width = w_in.shape[1] - 2 * qk_width - v_width
    row = lambda b, i: (b, i, 0)
    tab = pl.BlockSpec((tm, V7X_LANES), lambda b, i: (i, 0))
    est = (w_in.size * 2 + 2 * tm * D * 4 + 2 * tm * (2 * qk_width + v_width) * 2 + 2 * tm * u_width * 4
           + tm * D * 2 + 8 * tm * 1024 * 4 + 8 * tm * V7X_LANES * 4 + (4 << 20))
    return pl.pallas_call(
        functools.partial(_mix_in_kernel, qk_width=qk_width, v_width=v_width),
        grid=(B, S // tm),
        in_specs=[pl.BlockSpec((1, tm, D), row),
                  pl.BlockSpec((1, D), lambda b, i: (0, 0)),
                  _resident(w_in.shape),
                  pl.BlockSpec(seg.shape, lambda b, i: (0, 0)),
                  tab, tab, tab, tab],
        out_specs=[pl.BlockSpec((1, tm, qk_width), row),
                   pl.BlockSpec((1, tm, qk_width), row),
                   pl.BlockSpec((1, tm, v_width), row),
                   pl.BlockSpec((1, tm, u_width), row)],
        out_shape=[jax.ShapeDtypeStruct((B, S, qk_width), BF16),
                   jax.ShapeDtypeStruct((B, S, qk_width), BF16),
                   jax.ShapeDtypeStruct((B, S, v_width), BF16),
                   jax.ShapeDtypeStruct((B, S, u_width), F32)],
        compiler_params=_params(("parallel", "parallel"), est),
        name="mix_in",
    )(x, g, w_in, seg, aq, bq, ak, bk)


def _dot_tn(a, b):
    return lax.dot_general(a, b, (((0,), (0,)), ((), ())), preferred_element_type=F32)


def _diff_attn_kernel(lam_ref, q_ref, k_ref, v_ref, gsub_ref, o_ref, qbd_ref, s0_ref, s1_ref, m_ref, l_ref, acc_ref,
                      *, tq, tk, blocks_per_trip, out_scale):
    qt = q_ref[0].astype(F32).T
    row = lax.broadcasted_iota(jnp.int32, (HEAD_COLS, 1), 0)
    zero = jnp.zeros_like(qt)
    qbd_ref[:, 0:tq] = jnp.where(row < DIFF_QKDIM, qt, zero).astype(BF16)
    qbd_ref[:, tq:2 * tq] = jnp.where(row >= DIFF_QKDIM, qt, zero).astype(BF16)
    m_ref[...] = jnp.full(m_ref.shape, -jnp.inf, F32)
    l_ref[...] = jnp.zeros(l_ref.shape, F32)
    acc_ref[...] = jnp.zeros(acc_ref.shape, F32)

    def scores(j, s_ref):
        k0 = pl.multiple_of(j * tk, tk)
        s_ref[...] = _dot(k_ref[0, pl.ds(k0, tk), :], qbd_ref[...])

    def absorb(j, s_ref):
        k0 = pl.multiple_of(j * tk, tk)
        vb = v_ref[0, pl.ds(k0, tk), :]
        s = s_ref[...]
        m_prev = m_ref[...]
        m_new = jnp.maximum(m_prev, jnp.max(s, axis=0, keepdims=True))
        alpha = jnp.exp2(m_prev - m_new)
        p = jnp.exp2(s - m_new)
        l_ref[...] = alpha * l_ref[...] + jnp.sum(p, axis=0, keepdims=True)
        m_ref[...] = m_new
        pb = p.astype(BF16)
        for c in range(2):
            cols = slice(c * tq, (c + 1) * tq)
            acc_ref[c] = alpha[:, cols] * acc_ref[c] + _dot_tn(vb, pb[:, cols])

    bufs = (s0_ref, s1_ref)
    n_trips = k_ref.shape[1] // (blocks_per_trip * tk)
    scores(0, s0_ref)

    def trip(i, carry):
        j = blocks_per_trip * i
        for u in range(blocks_per_trip):
            scores(j + u + 1, bufs[(u + 1) % 2])
            absorb(j + u, bufs[u % 2])
        return carry

    lax.fori_loop(0, n_trips - 1, trip, 0)
    j_last = blocks_per_trip * (n_trips - 1)
    for u in range(blocks_per_trip):
        if u + 1 < blocks_per_trip:
            scores(j_last + u + 1, bufs[(u + 1) % 2])
        absorb(j_last + u, bufs[u % 2])

    inv = 1.0 / l_ref[...]
    o = acc_ref[0] * inv[:, 0:tq] - lam_ref[0] * (acc_ref[1] * inv[:, tq:2 * tq])
    ms = jnp.mean(o * o, axis=0, keepdims=True)
    o = o * lax.rsqrt(ms + EPS) * gsub_ref[...] * out_scale
    o_ref[0] = o.T.astype(o_ref.dtype)


def _diff_attn(lam, q, k, v, g_sub_col, lam_init, *, tq, tk, blocks_per_trip):
    B, S, _ = q.shape
    assert blocks_per_trip % 2 == 0 and S % (blocks_per_trip * tk) == 0, (S, tk, blocks_per_trip)
    est = (2 * 2 * S * (HEAD_COLS + DIFF_VDIM) * 2 + 4 * tq * HEAD_COLS * 2 + 2 * tq * HEAD_COLS * 2
           + 2 * tq * DIFF_VDIM * 4 + 8 * tk * 2 * tq * 4 + (4 << 20))
    return pl.pallas_call(
        functools.partial(_diff_attn_kernel, tq=tq, tk=tk, blocks_per_trip=blocks_per_trip,
                          out_scale=1.0 - lam_init),
        grid=(B, N_DIFF_HEADS, S // tq),
        in_specs=[pl.BlockSpec(memory_space=pltpu.SMEM),
                  pl.BlockSpec((1, tq, HEAD_COLS), lambda b, h, i: (b, i, h)),
                  pl.BlockSpec((1, S, HEAD_COLS), lambda b, h, i: (b, 0, h)),
                  pl.BlockSpec((1, S, DIFF_VDIM), lambda b, h, i: (b, 0, h)),
                  pl.BlockSpec((DIFF_VDIM, 1), lambda b, h, i: (0, 0))],
        out_specs=pl.BlockSpec((1, tq, DIFF_VDIM), lambda b, h, i: (b, i, h)),
        out_shape=jax.ShapeDtypeStruct((B, S, N_DIFF_HEADS * DIFF_VDIM), BF16),
        scratch_shapes=[pltpu.VMEM((HEAD_COLS, 2 * tq), BF16),
                        pltpu.VMEM((tk, 2 * tq), F32),
                        pltpu.VMEM((tk, 2 * tq), F32),
                        pltpu.VMEM((1, 2 * tq), F32),
                        pltpu.VMEM((1, 2 * tq), F32),
                        pltpu.VMEM((2, DIFF_VDIM, tq), F32)],
        compiler_params=_params(("parallel", "parallel", "parallel"), est),
        name="diff_attn",
    )(lam, q, k, v, g_sub_col)


def _diff_attn_bounded_kernel(lam_ref, q_ref, k_ref, v_ref, gsub_ref, o_ref, qbd_ref, l_ref, acc_ref,
                              *, tq, tk, blocks_per_trip, out_scale):
    row = lax.broadcasted_iota(jnp.int32, (HEAD_COLS, 1), 0)
    for hh in range(q_ref.shape[2] // HEAD_COLS):
        qk_cols = slice(hh * HEAD_COLS, (hh + 1) * HEAD_COLS)
        v_cols = slice(hh * DIFF_VDIM, (hh + 1) * DIFF_VDIM)
        qt = q_ref[0, :, qk_cols].astype(F32).T
        zero = jnp.zeros_like(qt)
        qbd_ref[hh, :, 0:tq] = jnp.where(row < DIFF_QKDIM, qt, zero).astype(BF16)
        qbd_ref[hh, :, tq:2 * tq] = jnp.where(row >= DIFF_QKDIM, qt, zero).astype(BF16)
        l_ref[hh] = jnp.zeros(l_ref.shape[1:], F32)
        acc_ref[hh] = jnp.zeros(acc_ref.shape[1:], F32)

        def trip(i, carry):
            for u in range(blocks_per_trip):
                k0 = pl.multiple_of((blocks_per_trip * i + u) * tk, tk)
                vb = v_ref[0, pl.ds(k0, tk), v_cols]
                p = jnp.exp2(_dot(k_ref[0, pl.ds(k0, tk), qk_cols], qbd_ref[hh]))
                l_ref[hh] += jnp.sum(p, axis=0, keepdims=True)
                pb = p.astype(BF16)
                for c in range(2):
                    acc_ref[hh, c] += _dot_tn(vb, pb[:, c * tq:(c + 1) * tq])
            return carry

        lax.fori_loop(0, k_ref.shape[1] // (blocks_per_trip * tk), trip, 0)

        inv = 1.0 / l_ref[hh]
        o = acc_ref[hh, 0] * inv[:, 0:tq] - lam_ref[0] * (acc_ref[hh, 1] * inv[:, tq:2 * tq])
        ms = jnp.mean(o * o, axis=0, keepdims=True)
        o = o * lax.rsqrt(ms + EPS) * gsub_ref[...] * out_scale
        o_ref[0, :, v_cols] = o.T.astype(o_ref.dtype)


def _diff_attn_bounded(lam, q, k, v, g_sub_col, lam_init, *, tq, tk, blocks_per_trip, heads_per_step):
    B, S, _ = q.shape
    assert S % (blocks_per_trip * tk) == 0, (S, tk, blocks_per_trip)
    assert N_DIFF_HEADS % heads_per_step == 0
    hps = heads_per_step
    est = (hps * (2 * 2 * S * (HEAD_COLS + DIFF_VDIM) * 2 + 4 * tq * HEAD_COLS * 2 + 2 * tq * HEAD_COLS * 2
                  + 2 * tq * DIFF_VDIM * 4) + 8 * tk * 2 * tq * 4 + (4 << 20))
    return pl.pallas_call(
        functools.partial(_diff_attn_bounded_kernel, tq=tq, tk=tk, blocks_per_trip=blocks_per_trip,
                          out_scale=1.0 - lam_init),
        grid=(B, N_DIFF_HEADS // hps, S // tq),
        in_specs=[pl.BlockSpec(memory_space=pltpu.SMEM),
                  pl.BlockSpec((1, tq, hps * HEAD_COLS), lambda b, h, i: (b, i, h)),
                  pl.BlockSpec((1, S, hps * HEAD_COLS), lambda b, h, i: (b, 0, h)),
                  pl.BlockSpec((1, S, hps * DIFF_VDIM), lambda b, h, i: (b, 0, h)),
                  pl.BlockSpec((DIFF_VDIM, 1), lambda b, h, i: (0, 0))],
        out_specs=pl.BlockSpec((1, tq, hps * DIFF_VDIM), lambda b, h, i: (b, i, h)),
        out_shape=jax.ShapeDtypeStruct((B, S, N_DIFF_HEADS * DIFF_VDIM), BF16),
        scratch_shapes=[pltpu.VMEM((hps, HEAD_COLS, 2 * tq), BF16),
                        pltpu.VMEM((hps, 1, 2 * tq), F32),
                        pltpu.VMEM((hps, 2, DIFF_VDIM, tq), F32)],
        compiler_params=_params(("parallel", "parallel", "parallel"), est),
        name="diff_attn_bounded",
    )(lam, q, k, v, g_sub_col)


def _pool_fold_kernel(wp_ref, ps_ref, wo_ref, o_ref):
    w = (wp_ref[0].astype(F32) * ps_ref[...]).astype(BF16)
    o_ref[...] = _dot(w, wo_ref[...]).astype(o_ref.dtype)


def _pool_fold(w_pool, pool_scale, w_out):
    groups, gw, _ = w_pool.shape
    D = w_out.shape[1]
    a_width = w_out.shape[0] - groups * gw
    folded = pl.pallas_call(
        _pool_fold_kernel,
        grid=(groups,),
        in_specs=[pl.BlockSpec((1, gw, gw), lambda g: (g, 0, 0)),
                  pl.BlockSpec((1, gw), lambda g: (0, g)),
                  pl.BlockSpec((gw, D), lambda g: (a_width // gw + g, 0))],
        out_specs=pl.BlockSpec((gw, D), lambda g: (g, 0)),
        out_shape=jax.ShapeDtypeStruct((groups * gw, D), BF16),
        compiler_params=_params(("parallel",), 8 * gw * D * 4 + (4 << 20)),
        name="pool_fold",
    )(w_pool, pool_scale, w_out)
    return jnp.concatenate([w_out[:a_width], folded], axis=0)


def _mix_out_kernel(a_ref, u_ref, up_ref, un_ref, x_ref, wo_ref, o_ref, *, tm, seq):
    i = pl.program_id(1)
    u_prev = jnp.where(i > 0, up_ref[0], 0.0)
    u_next = jnp.where(i < pl.num_programs(1) - 1, un_ref[0], 0.0)
    ue = jnp.concatenate([u_prev, u_ref[0], u_next], axis=0)
    rows = tm + 2 * POOL_HALO
    pos = i * tm + lax.broadcasted_iota(jnp.int32, (tm, 1), 0)
    a_width = a_ref.shape[2]
    acc = x_ref[0] + _dot(a_ref[0], wo_ref[0:a_width, :])
    pooled = []
    for g, w in enumerate(POOL_WINDOWS):
        c0 = g * POOL_GROUP_WIDTH
        ug = ue[:, c0:c0 + POOL_GROUP_WIDTH]
        win = ug + pltpu.roll(ug, 1, 0)
        shift = 1
        while 2 * shift < w:
            win = pltpu.roll(win, shift, 0) + pltpu.roll(win, rows - shift, 0)
            shift *= 2
        win = win[POOL_HALO:POOL_HALO + tm]
        cnt = jnp.minimum(pos + w // 2, seq) - jnp.maximum(pos - w // 2, 0)
        z = win / cnt.astype(F32) - ug[POOL_HALO:POOL_HALO + tm]
        pooled.append(z.astype(BF16))
    o_ref[0] = acc + _dot(jnp.concatenate(pooled, axis=1), wo_ref[a_width:, :])


def _mix_out(a, u, x, w_out, *, tm):
    B, S, D = x.shape
    a_width, u_width = a.shape[2], u.shape[2]
    halo_blocks = tm // POOL_HALO
    n_halo = S // POOL_HALO
    row = lambda b, i: (b, i, 0)
    est = (w_out.size * 2 + 4 * tm * D * 4 + 2 * tm * a_width * 2 + 2 * tm * u_width * 4
           + 6 * tm * u_width * 4 + 2 * tm * D * 4 + (4 << 20))
    return pl.pallas_call(
        functools.partial(_mix_out_kernel, tm=tm, seq=S),
        grid=(B, S // tm),
        in_specs=[pl.BlockSpec((1, tm, a_width), row),
                  pl.BlockSpec((1, tm, u_width), row),
                  pl.BlockSpec((1, POOL_HALO, u_width),
                               lambda b, i: (b, jnp.maximum(i * halo_blocks - 1, 0), 0)),
                  pl.BlockSpec((1, POOL_HALO, u_width),
                               lambda b, i: (b, jnp.minimum((i + 1) * halo_blocks, n_halo - 1), 0)),
                  pl.BlockSpec((1, tm, D), row),
                  _resident(w_out.shape)],
        out_specs=pl.BlockSpec((1, tm, D), row),
        out_shape=jax.ShapeDtypeStruct((B, S, D), F32),
        compiler_params=_params(("parallel", "parallel"), est),
        name="mix_out",
    )(a, u, u, u, x, w_out)


def _mem_k_kernel(mem_ref, g_ref, w_ref, gk_ref, o_ref):
    m = _rms(mem_ref[0], g_ref[...]).astype(BF16)
    k = _dot(m, w_ref[...])
    o_ref[0] = _rms(k, gk_ref[...]).astype(o_ref.dtype)


def _mem_k(mem, g_mem, wc_kv, gc_k):
    B, n_mem, D = mem.shape
    hd = D // N_CROSS_HEADS
    est = 2 * n_mem * D * 4 + 2 * D * hd * 2 + 4 * n_mem * hd * 4 + n_mem * D * 4 + (4 << 20)
    return pl.pallas_call(
        _mem_k_kernel,
        grid=(N_CROSS_HEADS, B),
        in_specs=[pl.BlockSpec((1, n_mem, D), lambda h, b: (b, 0, 0)),
                  pl.BlockSpec((1, D), lambda h, b: (0, 0)),
                  pl.BlockSpec((D, hd), lambda h, b: (0, h)),
                  pl.BlockSpec((1, hd), lambda h, b: (0, 0))],
        out_specs=pl.BlockSpec((1, n_mem, hd), lambda h, b: (b, 0, h)),
        out_shape=jax.ShapeDtypeStruct((B, n_mem, D), BF16),
        compiler_params=_params(("parallel", "parallel"), est),
        name="mem_k",
    )(mem, g_mem, wc_kv, gc_k)


def _mem_vo_kernel(mem_ref, g_ref, w_ref, wo_ref, o_ref):
    m = _rms(mem_ref[0], g_ref[...]).astype(BF16)
    v = _dot(m, w_ref[...]).astype(BF16)
    o_ref[0] = _dot(v, wo_ref[...]).astype(o_ref.dtype)


def _mem_vo(mem, g_mem, wc_kv, wc_o):
    B, n_mem, D = mem.shape
    hd = D // N_CROSS_HEADS
    est = (2 * n_mem * D * 4 + 2 * D * hd * 2 + 2 * hd * D * 2 + 2 * n_mem * D * 2 + 4 * n_mem * D * 4 + (4 << 20))
    return pl.pallas_call(
        _mem_vo_kernel,
        grid=(N_CROSS_HEADS, B),
        in_specs=[pl.BlockSpec((1, n_mem, D), lambda h, b: (b, 0, 0)),
                  pl.BlockSpec((1, D), lambda h, b: (0, 0)),
                  pl.BlockSpec((D, hd), lambda h, b: (0, N_CROSS_HEADS + h)),
                  pl.BlockSpec((hd, D), lambda h, b: (h, 0))],
        out_specs=pl.BlockSpec((1, n_mem, D), lambda h, b: (b, h, 0)),
        out_shape=jax.ShapeDtypeStruct((B, N_CROSS_HEADS * n_mem, D), BF16),
        compiler_params=_params(("parallel", "parallel"), est),
        name="mem_vo",
    )(mem, g_mem, wc_kv, wc_o)


def _cross_attn_kernel(x_ref, g_ref, wq_ref, gq_ref, k_ref, vo_ref, gn_ref, o_ref, hn_ref):
    x = x_ref[0]
    h = _rms(x, g_ref[...]).astype(BF16)
    hd = gq_ref.shape[1]
    scale = hd ** -0.5
    heads = [slice(c0, c0 + hd) for c0 in range(0, x.shape[1], hd)]
    qs = [_dot(h, wq_ref[:, c]) for c in heads]
    qn = [_rms(q, gq_ref[...]).astype(BF16) for q in qs]
    ss = [_dot_nt(q, k_ref[0, :, c]) * scale for q, c in zip(qn, heads)]
    ps = []
    for s in ss:
        p = jnp.exp(s - jnp.max(s, axis=-1, keepdims=True))
        ps.append((p / jnp.sum(p, axis=-1, keepdims=True)).astype(BF16))
    out = x + _dot(jnp.concatenate(ps, axis=1), vo_ref[0])
    o_ref[0] = out
    hn_ref[0] = _rms(out, gn_ref[...]).astype(hn_ref.dtype)


def _cross_attn(x, g_cross, wc_q, gc_q, k_mem, vo_mem, g_next, *, tm):
    B, S, D = x.shape
    n_mem = k_mem.shape[1]
    row = lambda b, i: (b, i, 0)
    est = (wc_q.size * 2 + 2 * n_mem * D * 2 + 2 * vo_mem.shape[1] * D * 2 + 4 * tm * D * 4 + 5 * tm * D * 4
           + (4 << 20))
    return pl.pallas_call(
        _cross_attn_kernel,
        grid=(B, S // tm),
        in_specs=[pl.BlockSpec((1, tm, D), row),
                  pl.BlockSpec((1, D), lambda b, i: (0, 0)),
                  _resident(wc_q.shape),
                  pl.BlockSpec(gc_q.shape, lambda b, i: (0, 0)),
                  pl.BlockSpec((1, n_mem, D), lambda b, i: (b, 0, 0)),
                  pl.BlockSpec((1, vo_mem.shape[1], D), lambda b, i: (b, 0, 0)),
                  pl.BlockSpec((1, D), lambda b, i: (0, 0))],
        out_specs=[pl.BlockSpec((1, tm, D), row), pl.BlockSpec((1, tm, D), row)],
        out_shape=[jax.ShapeDtypeStruct((B, S, D), F32), jax.ShapeDtypeStruct((B, S, D), BF16)],
        compiler_params=_params(("parallel", "parallel"), est),
        name="cross_attn",
    )(x, g_cross, wc_q, gc_q, k_mem, vo_mem, g_next)


def _conv_ffn_kernel(hin_ref, xr_ref, xp_ref, xn_ref, g_ref, wg_ref, wv_ref, cwg_ref, cwv_ref, cbg_ref, cbv_ref,
                     wd_ref, o_ref, h_ref, *, tm, res_chunks):
    i = pl.program_id(1)
    f = pl.program_id(2)
    rows = tm + 2 * CONV_HALO
    slab = tm // res_chunks

    @pl.when(f == 0)
    def _():
        g = g_ref[...]
        h_prev = jnp.where(i > 0, _rms(xp_ref[0], g), 0.0)
        h_next = jnp.where(i < pl.num_programs(1) - 1, _rms(xn_ref[0], g), 0.0)
        h_ref[0:CONV_HALO, :] = h_prev.astype(BF16)
        h_ref[CONV_HALO:CONV_HALO + tm, :] = hin_ref[0]
        h_ref[CONV_HALO + tm:rows, :] = h_next.astype(BF16)
        o_ref[0] = jnp.zeros(o_ref.shape[1:], o_ref.dtype)

    h = h_ref[...]

    def conv(w_ref, cw_ref, cb_ref):
        u = _dot(h, w_ref[...])
        below = pltpu.roll(u, 1, 0)[CONV_HALO:CONV_HALO + tm]
        above = pltpu.roll(u, rows - 1, 0)[CONV_HALO:CONV_HALO + tm]
        mid = u[CONV_HALO:CONV_HALO + tm]
        return below * cw_ref[0:1, :] + mid * cw_ref[1:2, :] + above * cw_ref[2:3, :] + cb_ref[...]

    act = jax.nn.gelu(conv(wg_ref, cwg_ref, cbg_ref)) * conv(wv_ref, cwv_ref, cbv_ref)
    o_ref[0] += _dot(act.astype(BF16), wd_ref[...])

    @pl.when(f < res_chunks)
    def _():
        r0 = pl.multiple_of(f * slab, slab)
        o_ref[0, pl.ds(r0, slab), :] += xr_ref[0]


def _conv_ffn(x, h, g_ffn, w_up, conv_w, conv_b, w_down, *, tm, tf):
    B, S, D = x.shape
    d_ff = w_down.shape[0]
    nf = d_ff // tf
    halo_blocks = tm // CONV_HALO
    n_halo = S // CONV_HALO
    rows = tm + 2 * CONV_HALO
    res_chunks = min(nf, 8)
    assert tm % (res_chunks * CONV_HALO) == 0, (tm, res_chunks)
    est = (2 * tm * D * 4 + 2 * tm * D * 2 + 2 * 3 * D * tf * 2 + rows * D * 2 + 8 * rows * tf * 4 + 2 * tm * D * 4
           + (4 << 20))
    gate = lambda b, i, f: (0, f)
    val = lambda b, i, f: (0, nf + f)
    return pl.pallas_call(
        functools.partial(_conv_ffn_kernel, tm=tm, res_chunks=res_chunks),
        grid=(B, S // tm, nf),
        in_specs=[pl.BlockSpec((1, tm, D), lambda b, i, f: (b, i, 0)),
                  pl.BlockSpec((1, tm // res_chunks, D),
                               lambda b, i, f: (b, i * res_chunks + jnp.minimum(f, res_chunks - 1), 0)),
                  pl.BlockSpec((1, CONV_HALO, D), lambda b, i, f: (b, jnp.maximum(i * halo_blocks - 1, 0), 0)),
                  pl.BlockSpec((1, CONV_HALO, D),
                               lambda b, i, f: (b, jnp.minimum((i + 1) * halo_blocks, n_halo - 1), 0)),
                  pl.BlockSpec((1, D), lambda b, i, f: (0, 0)),
                  pl.BlockSpec((D, tf), gate),
                  pl.BlockSpec((D, tf), val),
                  pl.BlockSpec((conv_w.shape[0], tf), gate),
                  pl.BlockSpec((conv_w.shape[0], tf), val),
                  pl.BlockSpec((1, tf), gate),
                  pl.BlockSpec((1, tf), val),
                  pl.BlockSpec((tf, D), lambda b, i, f: (f, 0))],
        out_specs=pl.BlockSpec((1, tm, D), lambda b, i, f: (b, i, 0)),
        out_shape=jax.ShapeDtypeStruct((B, S, D), F32),
        scratch_shapes=[pltpu.VMEM((rows, D), BF16)],
        compiler_params=_params(("parallel", "parallel", "arbitrary"), est),
        name="conv_ffn",
    )(h, x, x, x, g_ffn, w_up, w_up, conv_w, conv_w, conv_b, conv_b, w_down)


def _rope_tables(seq, gain, scale):
    half = DIFF_QKDIM // 2
    inv = ROPE_THETA ** (-jnp.arange(half, dtype=F32) / half)
    ang = jnp.arange(seq, dtype=F32)[:, None] * inv[None, :]
    cos = jnp.cos(ang)
    sin = jnp.sin(ang)
    gain = gain.astype(F32)
    a = jnp.concatenate([gain[:half] * cos, gain[half:] * cos], axis=1) * scale
    b = jnp.concatenate([-gain[half:] * sin, gain[:half] * sin], axis=1) * scale
    reps = V7X_LANES // DIFF_QKDIM
    return jnp.tile(a, (1, reps)), jnp.tile(b, (1, reps))


def _segment_ones(width):
    seg = jnp.arange(width) // DIFF_QKDIM
    return (seg[:, None] == seg[None, :]).astype(BF16)


def _tile(n, target):
    t = min(n, target)
    assert n % t == 0, (n, t)
    return t


def _run_trunk(x, mem, layers):
    B, S, D = x.shape
    tm = _tile(S, TOKEN_TILE)
    tq = _tile(S, ATTN_Q_TILE)
    flash_tk = _tile(S // FLASH_BLOCKS_PER_TRIP, ATTN_K_TILE)
    tk = _tile(S // 8, ATTN_K_TILE)
    seg = _segment_ones(V7X_MXU_WIDTH)
    for l, p in enumerate(layers):
        lam_init = 0.8 - 0.6 * math.exp(-0.3 * l)
        q_scale = DIFF_QKDIM ** -0.5 * math.log2(math.e)
        aq, bq = _rope_tables(S, p["g_q"], q_scale)
        ak, bk = _rope_tables(S, p["g_k"], 1.0)
        q, k, v, u = _mix_in(x, p["g_mix"], p["w_in"], seg, aq, bq, ak, bk, tm=tm)
        bound = (DIFF_QKDIM * q_scale * 1.01 * jnp.max(jnp.abs(p["g_q"])) * jnp.max(jnp.abs(p["g_k"]))).astype(F32)
        attn = functools.partial(_diff_attn, p["lam"], q, k, v, p["g_sub"].reshape(-1, 1), lam_init,
                                 tq=tq, tk=flash_tk, blocks_per_trip=FLASH_BLOCKS_PER_TRIP)
        attn_bounded = functools.partial(_diff_attn_bounded, p["lam"], q, k, v, p["g_sub"].reshape(-1, 1), lam_init,
                                         tq=tq, tk=tk, blocks_per_trip=min(ATTN_BLOCKS_PER_TRIP, S // tk),
                                         heads_per_step=ATTN_HEADS_PER_STEP)
        a = lax.cond(bound <= UNSHIFTED_SOFTMAX_SCORE_LIMIT, attn_bounded, attn)
        x = _mix_out(a, u, x, p["w_out_folded"], tm=tm)
        k_mem = _mem_k(mem, p["g_mem"], p["wc_kv"], p["gc_k"])
        vo_mem = _mem_vo(mem, p["g_mem"], p["wc_kv"], p["wc_o"])
        x, h_ffn = _cross_attn(x, p["g_cross"], p["wc_q"], p["gc_q"], k_mem, vo_mem, p["g_ffn"], tm=tm)
        x = _conv_ffn(x, h_ffn, p["g_ffn"], p["w_up"], p["conv_w"], p["conv_b"], p["w_down"],
                      tm=_tile(S, FFN_TOKEN_TILE), tf=FFN_CHUNK)
    return x


def kernel(x_prompt, x_sample, mem_prompt, mem_sample, g_mix, w_in, g_q, g_k, lam_q1, lam_k1, lam_q2, lam_k2,
           g_sub, w_pool, pool_scale, w_out, g_cross, g_mem, wc_q, wc_kv, gc_q, gc_k, wc_o, g_ffn, w_up,
           conv_w, conv_b, w_down):
    depth = w_in.shape[0]
    layers = []
    for l in range(depth):
        lam_init = 0.8 - 0.6 * math.exp(-0.3 * l)
        lam = (jnp.exp(jnp.sum(lam_q1[l].astype(F32) * lam_k1[l].astype(F32)))
               - jnp.exp(jnp.sum(lam_q2[l].astype(F32) * lam_k2[l].astype(F32))) + lam_init)
        row = lambda t: t[l].reshape(1, -1).astype(F32)
        layers.append(dict(
            lam=lam.reshape(1).astype(F32),
            g_mix=row(g_mix), g_q=g_q[l], g_k=g_k[l], g_sub=row(g_sub),
            g_cross=row(g_cross), g_mem=row(g_mem), gc_q=row(gc_q), gc_k=row(gc_k), g_ffn=row(g_ffn),
            conv_w=conv_w[l].astype(F32), conv_b=row(conv_b),
            w_in=w_in[l].astype(BF16),
            w_out_folded=_pool_fold(w_pool[l].astype(BF16), row(pool_scale), w_out[l].astype(BF16)),
            wc_q=wc_q[l].astype(BF16), wc_kv=wc_kv[l].astype(BF16), wc_o=wc_o[l].astype(BF16),
            w_up=w_up[l].astype(BF16), w_down=w_down[l].astype(BF16)))
    return (_run_trunk(x_prompt, mem_prompt, layers), _run_trunk(x_sample, mem_sample, layers))
```

```python
import functools
import math

import jax
import jax.numpy as jnp
from jax import lax
from jax.experimental import pallas as pl
from jax.experimental.pallas import tpu as pltpu

F32 = jnp.float32
BF16 = jnp.bfloat16

N_DIFF_HEADS = 8
DIFF_QKDIM = 64
DIFF_VDIM = 128
HEAD_COLS = 2 * DIFF_QKDIM
POOL_WINDOWS = (2, 4, 8, 16)
POOL_GROUP_WIDTH = 256
POOL_HALO = 16
N_CROSS_HEADS = 4
CONV_HALO = 8
FFN_CHUNK = 512
ROPE_THETA = 10000.0
EPS = 1e-6
UNSHIFTED_SOFTMAX_SCORE_LIMIT = 30.0

V7X_VMEM_BYTES = 64 * 1024 * 1024
V7X_LANES = 128
V7X_MXU_WIDTH = 256
VMEM_LIMIT_CAP = V7X_VMEM_BYTES - 6 * 1024 * 1024

TOKEN_TILE = 512
FFN_TOKEN_TILE = 1024
ATTN_Q_TILE = 512
ATTN_K_TILE = 512
ATTN_BLOCKS_PER_TRIP = 16
ATTN_HEADS_PER_STEP = 4
FLASH_BLOCKS_PER_TRIP = 4


def _vmem_limit(estimate_bytes):
    return int(min(VMEM_LIMIT_CAP, max(32 * 1024 * 1024, estimate_bytes)))


def _params(semantics, vmem_estimate):
    return pltpu.CompilerParams(dimension_semantics=semantics, vmem_limit_bytes=_vmem_limit(vmem_estimate))


def _resident(shape):
    return pl.BlockSpec(shape, lambda *_: (0,) * len(shape), pipeline_mode=pl.Buffered(1))


def _rms(x, gain):
    ms = jnp.mean(x * x, axis=-1, keepdims=True)
    return x * lax.rsqrt(ms + EPS) * gain


def _dot(a, b):
    return jnp.dot(a, b, preferred_element_type=F32)


def _dot_nt(a, b):
    return lax.dot_general(a, b, (((1,), (1,)), ((), ())), preferred_element_type=F32)


def _mix_in_kernel(x_ref, g_ref, w_ref, seg_ref, aq_ref, bq_ref, ak_ref, bk_ref,
                   q_ref, k_ref, v_ref, u_ref, *, qk_width, v_width):
    h = _rms(x_ref[0], g_ref[...]).astype(BF16)
    lane = lax.broadcasted_iota(jnp.int32, (1, V7X_LANES), 1)
    partner_is_above = (lane & (DIFF_QKDIM // 2)) == 0
    seg = seg_ref[...]
    chunk = seg.shape[0]
    for col0, a_ref, b_ref, o_ref in ((0, aq_ref, bq_ref, q_ref), (qk_width, ak_ref, bk_ref, k_ref)):
        a = a_ref[...]
        b = b_ref[...]
        zfull = _dot(h, w_ref[:, col0:col0 + qk_width])
        for c0 in range(0, qk_width, chunk):
            z = zfull[:, c0:c0 + chunk]
            ss = _dot((z * z).astype(BF16), seg)
            zn = z * lax.rsqrt(ss * (1.0 / DIFF_QKDIM) + EPS)
            for c in range(0, chunk, V7X_LANES):
                zc = zn[:, c:c + V7X_LANES]
                partner = jnp.where(partner_is_above,
                                    pltpu.roll(zc, V7X_LANES - DIFF_QKDIM // 2, 1),
                                    pltpu.roll(zc, DIFF_QKDIM // 2, 1))
                o_ref[0, :, c0 + c:c0 + c + V7X_LANES] = (zc * a + partner * b).astype(o_ref.dtype)
    v0 = 2 * qk_width
    v_ref[0] = _dot(h, w_ref[:, v0:v0 + v_width]).astype(v_ref.dtype)
    u_ref[0] = _dot(h, w_ref[:, v0 + v_width:])


def _mix_in(x, g, w_in, seg, aq, bq, ak, bk, *, tm):
    B, S, D = x.shape
    qk_width = N_DIFF_HEADS * HEAD_COLS
    v_width = N_DIFF_HEADS * DIFF_VDIM
    u_width = w_in.shape[1] - 2 * qk_width - v_width
    row = lambda b, i: (b, i, 0)
    tab = pl.BlockSpec((tm, V7X_LANES), lambda b, i: (i, 0))
    est = (w_in.size * 2 + 2 * tm * D * 4 + 2 * tm * (2 * qk_width + v_width) * 2 + 2 * tm * u_width * 4
           + tm * D * 2 + 8 * tm * 1024 * 4 + 8 * tm * V7X_LANES * 4 + (4 << 20))
    return pl.pallas_call(
        functools.partial(_mix_in_kernel, qk_width=qk_width, v_width=v_width),
        grid=(B, S // tm),
        in_specs=[pl.BlockSpec((1, tm, D), row),
                  pl.BlockSpec((1, D), lambda b, i: (0, 0)),
                  _resident(w_in.shape),
                  pl.BlockSpec(seg.shape, lambda b, i: (0, 0)),
                  tab, tab, tab, tab],
        out_specs=[pl.BlockSpec((1, tm, qk_width), row),
                   pl.BlockSpec((1, tm, qk_width), row),
                   pl.BlockSpec((1, tm, v_width), row),
                   pl.BlockSpec((1, tm, u_width), row)],
        out_shape=[jax.ShapeDtypeStruct((B, S, qk_width), BF16),
                   jax.ShapeDtypeStruct((B, S, qk_width), BF16),
                   jax.ShapeDtypeStruct((B, S, v_width), BF16),
                   jax.ShapeDtypeStruct((B, S, u_width), F32)],
        compiler_params=_params(("parallel", "parallel"), est),
        name="mix_in",
    )(x, g, w_in, seg, aq, bq, ak, bk)


def _dot_tn(a, b):
    return lax.dot_general(a, b, (((0,), (0,)), ((), ())), preferred_element_type=F32)


def _diff_attn_kernel(lam_ref, q_ref, k_ref, v_ref, gsub_ref, o_ref, qbd_ref, s0_ref, s1_ref, m_ref, l_ref, acc_ref,
                      *, tq, tk, blocks_per_trip, out_scale):
    qt = q_ref[0].astype(F32).T
    row = lax.broadcasted_iota(jnp.int32, (HEAD_COLS, 1), 0)
    zero = jnp.zeros_like(qt)
    qbd_ref[:, 0:tq] = jnp.where(row < DIFF_QKDIM, qt, zero).astype(BF16)
    qbd_ref[:, tq:2 * tq] = jnp.where(row >= DIFF_QKDIM, qt, zero).astype(BF16)
    m_ref[...] = jnp.full(m_ref.shape, -jnp.inf, F32)
    l_ref[...] = jnp.zeros(l_ref.shape, F32)
    acc_ref[...] = jnp.zeros(acc_ref.shape, F32)

    def scores(j, s_ref):
        k0 = pl.multiple_of(j * tk, tk)
        s_ref[...] = _dot(k_ref[0, pl.ds(k0, tk), :], qbd_ref[...])

    def absorb(j, s_ref):
        k0 = pl.multiple_of(j * tk, tk)
        vb = v_ref[0, pl.ds(k0, tk), :]
        s = s_ref[...]
        m_prev = m_ref[...]
        m_new = jnp.maximum(m_prev, jnp.max(s, axis=0, keepdims=True))
        alpha = jnp.exp2(m_prev - m_new)
        p = jnp.exp2(s - m_new)
        l_ref[...] = alpha * l_ref[...] + jnp.sum(p, axis=0, keepdims=True)
        m_ref[...] = m_new
        pb = p.astype(BF16)
        for c in range(2):
            cols = slice(c * tq, (c + 1) * tq)
            acc_ref[c] = alpha[:, cols] * acc_ref[c] + _dot_tn(vb, pb[:, cols])

    bufs = (s0_ref, s1_ref)
    n_trips = k_ref.shape[1] // (blocks_per_trip * tk)
    scores(0, s0_ref)

    def trip(i, carry):
        j = blocks_per_trip * i
        for u in range(blocks_per_trip):
            scores(j + u + 1, bufs[(u + 1) % 2])
            absorb(j + u, bufs[u % 2])
        return carry

    lax.fori_loop(0, n_trips - 1, trip, 0)
    j_last = blocks_per_trip * (n_trips - 1)
    for u in range(blocks_per_trip):
        if u + 1 < blocks_per_trip:
            scores(j_last + u + 1, bufs[(u + 1) % 2])
        absorb(j_last + u, bufs[u % 2])

    inv = 1.0 / l_ref[...]
    o = acc_ref[0] * inv[:, 0:tq] - lam_ref[0] * (acc_ref[1] * inv[:, tq:2 * tq])
    ms = jnp.mean(o * o, axis=0, keepdims=True)
    o = o * lax.rsqrt(ms + EPS) * gsub_ref[...] * out_scale
    o_ref[0] = o.T.astype(o_ref.dtype)


def _diff_attn(lam, q, k, v, g_sub_col, lam_init, *, tq, tk, blocks_per_trip):
    B, S, _ = q.shape
    assert blocks_per_trip % 2 == 0 and S % (blocks_per_trip * tk) == 0, (S, tk, blocks_per_trip)
    est = (2 * 2 * S * (HEAD_COLS + DIFF_VDIM) * 2 + 4 * tq * HEAD_COLS * 2 + 2 * tq * HEAD_COLS * 2
           + 2 * tq * DIFF_VDIM * 4 + 8 * tk * 2 * tq * 4 + (4 << 20))
    return pl.pallas_call(
        functools.partial(_diff_attn_kernel, tq=tq, tk=tk, blocks_per_trip=blocks_per_trip,
                          out_scale=1.0 - lam_init),
        grid=(B, N_DIFF_HEADS, S // tq),
        in_specs=[pl.BlockSpec(memory_space=pltpu.SMEM),
                  pl.BlockSpec((1, tq, HEAD_COLS), lambda b, h, i: (b, i, h)),
                  pl.BlockSpec((1, S, HEAD_COLS), lambda b, h, i: (b, 0, h)),
                  pl.BlockSpec((1, S, DIFF_VDIM), lambda b, h, i: (b, 0, h)),
                  pl.BlockSpec((DIFF_VDIM, 1), lambda b, h, i: (0, 0))],
        out_specs=pl.BlockSpec((1, tq, DIFF_VDIM), lambda b, h, i: (b, i, h)),
        out_shape=jax.ShapeDtypeStruct((B, S, N_DIFF_HEADS * DIFF_VDIM), BF16),
        scratch_shapes=[pltpu.VMEM((HEAD_COLS, 2 * tq), BF16),
                        pltpu.VMEM((tk, 2 * tq), F32),
                        pltpu.VMEM((tk, 2 * tq), F32),
                        pltpu.VMEM((1, 2 * tq), F32),
                        pltpu.VMEM((1, 2 * tq), F32),
                        pltpu.VMEM((2, DIFF_VDIM, tq), F32)],
        compiler_params=_params(("parallel", "parallel", "parallel"), est),
        name="diff_attn",
    )(lam, q, k, v, g_sub_col)


def _diff_attn_bounded_kernel(lam_ref, q_ref, k_ref, v_ref, gsub_ref, o_ref, qbd_ref, l_ref, acc_ref,
                              *, tq, tk, blocks_per_trip, out_scale):
    row = lax.broadcasted_iota(jnp.int32, (HEAD_COLS, 1), 0)
    for hh in range(q_ref.shape[2] // HEAD_COLS):
        qk_cols = slice(hh * HEAD_COLS, (hh + 1) * HEAD_COLS)
        v_cols = slice(hh * DIFF_VDIM, (hh + 1) * DIFF_VDIM)
        qt = q_ref[0, :, qk_cols].astype(F32).T
        zero = jnp.zeros_like(qt)
        qbd_ref[hh, :, 0:tq] = jnp.where(row < DIFF_QKDIM, qt, zero).astype(BF16)
        qbd_ref[hh, :, tq:2 * tq] = jnp.where(row >= DIFF_QKDIM, qt, zero).astype(BF16)
        l_ref[hh] = jnp.zeros(l_ref.shape[1:], F32)
        acc_ref[hh] = jnp.zeros(acc_ref.shape[1:], F32)

        def trip(i, carry):
            def scores(u):
                k0 = pl.multiple_of((blocks_per_trip * i + u) * tk, tk)
                return _dot(k_ref[0, pl.ds(k0, tk), qk_cols], qbd_ref[hh])

            s = scores(0)
            for u in range(blocks_per_trip):
                s_next = scores(u + 1) if u + 1 < blocks_per_trip else None
                k0 = pl.multiple_of((blocks_per_trip * i + u) * tk, tk)
                vb = v_ref[0, pl.ds(k0, tk), v_cols]
                p = jnp.exp2(s)
                l_ref[hh] += jnp.sum(p, axis=0, keepdims=True)
                pb = p.astype(BF16)
                for c in range(2):
                    acc_ref[hh, c] += _dot_tn(vb, pb[:, c * tq:(c + 1) * tq])
                s = s_next
            return carry

        lax.fori_loop(0, k_ref.shape[1] // (blocks_per_trip * tk), trip, 0)

        inv = 1.0 / l_ref[hh]
        o = acc_ref[hh, 0] * inv[:, 0:tq] - lam_ref[0] * (acc_ref[hh, 1] * inv[:, tq:2 * tq])
        ms = jnp.mean(o * o, axis=0, keepdims=True)
        o = o * lax.rsqrt(ms + EPS) * gsub_ref[...] * out_scale
        o_ref[0, :, v_cols] = o.T.astype(o_ref.dtype)


def _diff_attn_bounded(lam, q, k, v, g_sub_col, lam_init, *, tq, tk, blocks_per_trip, heads_per_step):
    B, S, _ = q.shape
    assert S % (blocks_per_trip * tk) == 0, (S, tk, blocks_per_trip)
    assert N_DIFF_HEADS % heads_per_step == 0
    hps = heads_per_step
    est = (hps * (2 * 2 * S * (HEAD_COLS + DIFF_VDIM) * 2 + 4 * tq * HEAD_COLS * 2 + 2 * tq * HEAD_COLS * 2
                  + 2 * tq * DIFF_VDIM * 4) + 8 * tk * 2 * tq * 4 + (4 << 20))
    return pl.pallas_call(
        functools.partial(_diff_attn_bounded_kernel, tq=tq, tk=tk, blocks_per_trip=blocks_per_trip,
                          out_scale=1.0 - lam_init),
        grid=(B, N_DIFF_HEADS // hps, S // tq),
        in_specs=[pl.BlockSpec(memory_space=pltpu.SMEM),
                  pl.BlockSpec((1, tq, hps * HEAD_COLS), lambda b, h, i: (b, i, h)),
                  pl.BlockSpec((1, S, hps * HEAD_COLS), lambda b, h, i: (b, 0, h)),
                  pl.BlockSpec((1, S, hps * DIFF_VDIM), lambda b, h, i: (b, 0, h)),
                  pl.BlockSpec((DIFF_VDIM, 1), lambda b, h, i: (0, 0))],
        out_specs=pl.BlockSpec((1, tq, hps * DIFF_VDIM), lambda b, h, i: (b, i, h)),
        out_shape=jax.ShapeDtypeStruct((B, S, N_DIFF_HEADS * DIFF_VDIM), BF16),
        scratch_shapes=[pltpu.VMEM((hps, HEAD_COLS, 2 * tq), BF16),
                        pltpu.VMEM((hps, 1, 2 * tq), F32),
                        pltpu.VMEM((hps, 2, DIFF_VDIM, tq), F32)],
        compiler_params=_params(("parallel", "parallel", "parallel"), est),
        name="diff_attn_bounded",
    )(lam, q, k, v, g_sub_col)


def _pool_fold_kernel(wp_ref, ps_ref, wo_ref, o_ref):
    w = (wp_ref[0].astype(F32) * ps_ref[...]).astype(BF16)
    o_ref[...] = _dot(w, wo_ref[...]).astype(o_ref.dtype)


def _pool_fold(w_pool, pool_scale, w_out):
    groups, gw, _ = w_pool.shape
    D = w_out.shape[1]
    a_width = w_out.shape[0] - groups * gw
    folded = pl.pallas_call(
        _pool_fold_kernel,
        grid=(groups,),
        in_specs=[pl.BlockSpec((1, gw, gw), lambda g: (g, 0, 0)),
                  pl.BlockSpec((1, gw), lambda g: (0, g)),
                  pl.BlockSpec((gw, D), lambda g: (a_width // gw + g, 0))],
        out_specs=pl.BlockSpec((gw, D), lambda g: (g, 0)),
        out_shape=jax.ShapeDtypeStruct((groups * gw, D), BF16),
        compiler_params=_params(("parallel",), 8 * gw * D * 4 + (4 << 20)),
        name="pool_fold",
    )(w_pool, pool_scale, w_out)
    return jnp.concatenate([w_out[:a_width], folded], axis=0)


def _mix_out_kernel(a_ref, u_ref, up_ref, un_ref, x_ref, wo_ref, o_ref, *, tm, seq):
    i = pl.program_id(1)
    u_prev = jnp.where(i > 0, up_ref[0], 0.0)
    u_next = jnp.where(i < pl.num_programs(1) - 1, un_ref[0], 0.0)
    ue = jnp.concatenate([u_prev, u_ref[0], u_next], axis=0)
    rows = tm + 2 * POOL_HALO
    pos = i * tm + lax.broadcasted_iota(jnp.int32, (tm, 1), 0)
    a_width = a_ref.shape[2]
    acc = x_ref[0] + _dot(a_ref[0], wo_ref[0:a_width, :])
    pooled = []
    for g, w in enumerate(POOL_WINDOWS):
        c0 = g * POOL_GROUP_WIDTH
        ug = ue[:, c0:c0 + POOL_GROUP_WIDTH]
        win = ug + pltpu.roll(ug, 1, 0)
        shift = 1
        while 2 * shift < w:
            win = pltpu.roll(win, shift, 0) + pltpu.roll(win, rows - shift, 0)
            shift *= 2
        win = win[POOL_HALO:POOL_HALO + tm]
        cnt = jnp.minimum(pos + w // 2, seq) - jnp.maximum(pos - w // 2, 0)
        z = win / cnt.astype(F32) - ug[POOL_HALO:POOL_HALO + tm]
        pooled.append(z.astype(BF16))
    o_ref[0] = acc + _dot(jnp.concatenate(pooled, axis=1), wo_ref[a_width:, :])


def _mix_out(a, u, x, w_out, *, tm):
    B, S, D = x.shape
    a_width, u_width = a.shape[2], u.shape[2]
    halo_blocks = tm // POOL_HALO
    n_halo = S // POOL_HALO
    row = lambda b, i: (b, i, 0)
    est = (w_out.size * 2 + 4 * tm * D * 4 + 2 * tm * a_width * 2 + 2 * tm * u_width * 4
           + 6 * tm * u_width * 4 + 2 * tm * D * 4 + (4 << 20))
    return pl.pallas_call(
        functools.partial(_mix_out_kernel, tm=tm, seq=S),
        grid=(B, S // tm),
        in_specs=[pl.BlockSpec((1, tm, a_width), row),
                  pl.BlockSpec((1, tm, u_width), row),
                  pl.BlockSpec((1, POOL_HALO, u_width),
                               lambda b, i: (b, jnp.maximum(i * halo_blocks - 1, 0), 0)),
                  pl.BlockSpec((1, POOL_HALO, u_width),
                               lambda b, i: (b, jnp.minimum((i + 1) * halo_blocks, n_halo - 1), 0)),
                  pl.BlockSpec((1, tm, D), row),
                  _resident(w_out.shape)],
        out_specs=pl.BlockSpec((1, tm, D), row),
        out_shape=jax.ShapeDtypeStruct((B, S, D), F32),
        compiler_params=_params(("parallel", "parallel"), est),
        name="mix_out",
    )(a, u, u, u, x, w_out)


def _mem_k_kernel(mem_ref, g_ref, w_ref, gk_ref, o_ref):
    m = _rms(mem_ref[0], g_ref[...]).astype(BF16)
    k = _dot(m, w_ref[...])
    o_ref[0] = _rms(k, gk_ref[...]).astype(o_ref.dtype)


def _mem_k(mem, g_mem, wc_kv, gc_k):
    B, n_mem, D = mem.shape
    hd = D // N_CROSS_HEADS
    est = 2 * n_mem * D * 4 + 2 * D * hd * 2 + 4 * n_mem * hd * 4 + n_mem * D * 4 + (4 << 20)
    return pl.pallas_call(
        _mem_k_kernel,
        grid=(N_CROSS_HEADS, B),
        in_specs=[pl.BlockSpec((1, n_mem, D), lambda h, b: (b, 0, 0)),
                  pl.BlockSpec((1, D), lambda h, b: (0, 0)),
                  pl.BlockSpec((D, hd), lambda h, b: (0, h)),
                  pl.BlockSpec((1, hd), lambda h, b: (0, 0))],
        out_specs=pl.BlockSpec((1, n_mem, hd), lambda h, b: (b, 0, h)),
        out_shape=jax.ShapeDtypeStruct((B, n_mem, D), BF16),
        compiler_params=_params(("parallel", "parallel"), est),
        name="mem_k",
    )(mem, g_mem, wc_kv, gc_k)


def _mem_vo_kernel(mem_ref, g_ref, w_ref, wo_ref, o_ref):
    m = _rms(mem_ref[0], g_ref[...]).astype(BF16)
    v = _dot(m, w_ref[...]).astype(BF16)
    o_ref[0] = _dot(v, wo_ref[...]).astype(o_ref.dtype)


def _mem_vo(mem, g_mem, wc_kv, wc_o):
    B, n_mem, D = mem.shape
    hd = D // N_CROSS_HEADS
    est = (2 * n_mem * D * 4 + 2 * D * hd * 2 + 2 * hd * D * 2 + 2 * n_mem * D * 2 + 4 * n_mem * D * 4 + (4 << 20))
    return pl.pallas_call(
        _mem_vo_kernel,
        grid=(N_CROSS_HEADS, B),
        in_specs=[pl.BlockSpec((1, n_mem, D), lambda h, b: (b, 0, 0)),
                  pl.BlockSpec((1, D), lambda h, b: (0, 0)),
                  pl.BlockSpec((D, hd), lambda h, b: (0, N_CROSS_HEADS + h)),
                  pl.BlockSpec((hd, D), lambda h, b: (h, 0))],
        out_specs=pl.BlockSpec((1, n_mem, D), lambda h, b: (b, h, 0)),
        out_shape=jax.ShapeDtypeStruct((B, N_CROSS_HEADS * n_mem, D), BF16),
        compiler_params=_params(("parallel", "parallel"), est),
        name="mem_vo",
    )(mem, g_mem, wc_kv, wc_o)


def _cross_attn_kernel(x_ref, g_ref, wq_ref, gq_ref, k_ref, vo_ref, gn_ref, o_ref, hn_ref):
    x = x_ref[0]
    h = _rms(x, g_ref[...]).astype(BF16)
    hd = gq_ref.shape[1]
    scale = hd ** -0.5
    heads = [slice(c0, c0 + hd) for c0 in range(0, x.shape[1], hd)]
    qs = [_dot(h, wq_ref[:, c]) for c in heads]
    qn = [_rms(q, gq_ref[...]).astype(BF16) for q in qs]
    ss = [_dot_nt(q, k_ref[0, :, c]) * scale for q, c in zip(qn, heads)]
    ps = []
    for s in ss:
        p = jnp.exp(s - jnp.max(s, axis=-1, keepdims=True))
        ps.append((p / jnp.sum(p, axis=-1, keepdims=True)).astype(BF16))
    out = x + _dot(jnp.concatenate(ps, axis=1), vo_ref[0])
    o_ref[0] = out
    hn_ref[0] = _rms(out, gn_ref[...]).astype(hn_ref.dtype)


def _cross_attn(x, g_cross, wc_q, gc_q, k_mem, vo_mem, g_next, *, tm):
    B, S, D = x.shape
    n_mem = k_mem.shape[1]
    row = lambda b, i: (b, i, 0)
    est = (wc_q.size * 2 + 2 * n_mem * D * 2 + 2 * vo_mem.shape[1] * D * 2 + 4 * tm * D * 4 + 5 * tm * D * 4
           + (4 << 20))
    return pl.pallas_call(
        _cross_attn_kernel,
        grid=(B, S // tm),
        in_specs=[pl.BlockSpec((1, tm, D), row),
                  pl.BlockSpec((1, D), lambda b, i: (0, 0)),
                  _resident(wc_q.shape),
                  pl.BlockSpec(gc_q.shape, lambda b, i: (0, 0)),
                  pl.BlockSpec((1, n_mem, D), lambda b, i: (b, 0, 0)),
                  pl.BlockSpec((1, vo_mem.shape[1], D), lambda b, i: (b, 0, 0)),
                  pl.BlockSpec((1, D), lambda b, i: (0, 0))],
        out_specs=[pl.BlockSpec((1, tm, D), row), pl.BlockSpec((1, tm, D), row)],
        out_shape=[jax.ShapeDtypeStruct((B, S, D), F32), jax.ShapeDtypeStruct((B, S, D), BF16)],
        compiler_params=_params(("parallel", "parallel"), est),
        name="cross_attn",
    )(x, g_cross, wc_q, gc_q, k_mem, vo_mem, g_next)


def _conv_ffn_kernel(hin_ref, xr_ref, xp_ref, xn_ref, g_ref, wg_ref, wv_ref, cwg_ref, cwv_ref, cbg_ref, cbv_ref,
                     wd_ref, o_ref, h_ref, *, tm, res_chunks):
    i = pl.program_id(1)
    f = pl.program_id(2)
    rows = tm + 2 * CONV_HALO
    slab = tm // res_chunks

    @pl.when(f == 0)
    def _():
        g = g_ref[...]
        h_prev = jnp.where(i > 0, _rms(xp_ref[0], g), 0.0)
        h_next = jnp.where(i < pl.num_programs(1) - 1, _rms(xn_ref[0], g), 0.0)
        h_ref[0:CONV_HALO, :] = h_prev.astype(BF16)
        h_ref[CONV_HALO:CONV_HALO + tm, :] = hin_ref[0]
        h_ref[CONV_HALO + tm:rows, :] = h_next.astype(BF16)
        o_ref[0] = jnp.zeros(o_ref.shape[1:], o_ref.dtype)

    h = h_ref[...]

    def conv(w_ref, cw_ref, cb_ref):
        u = _dot(h, w_ref[...])
        below = pltpu.roll(u, 1, 0)[CONV_HALO:CONV_HALO + tm]
        above = pltpu.roll(u, rows - 1, 0)[CONV_HALO:CONV_HALO + tm]
        mid = u[CONV_HALO:CONV_HALO + tm]
        return below * cw_ref[0:1, :] + mid * cw_ref[1:2, :] + above * cw_ref[2:3, :] + cb_ref[...]

    act = jax.nn.gelu(conv(wg_ref, cwg_ref, cbg_ref)) * conv(wv_ref, cwv_ref, cbv_ref)
    o_ref[0] += _dot(act.astype(BF16), wd_ref[...])

    @pl.when(f < res_chunks)
    def _():
        r0 = pl.multiple_of(f * slab, slab)
        o_ref[0, pl.ds(r0, slab), :] += xr_ref[0]


def _conv_ffn(x, h, g_ffn, w_up, conv_w, conv_b, w_down, *, tm, tf):
    B, S, D = x.shape
    d_ff = w_down.shape[0]
    nf = d_ff // tf
    halo_blocks = tm // CONV_HALO
    n_halo = S // CONV_HALO
    rows = tm + 2 * CONV_HALO
    res_chunks = min(nf, 8)
    assert tm % (res_chunks * CONV_HALO) == 0, (tm, res_chunks)
    est = (2 * tm * D * 4 + 2 * tm * D * 2 + 2 * 3 * D * tf * 2 + rows * D * 2 + 8 * rows * tf * 4 + 2 * tm * D * 4
           + (4 << 20))
    gate = lambda b, i, f: (0, f)
    val = lambda b, i, f: (0, nf + f)
    return pl.pallas_call(
        functools.partial(_conv_ffn_kernel, tm=tm, res_chunks=res_chunks),
        grid=(B, S // tm, nf),
        in_specs=[pl.BlockSpec((1, tm, D), lambda b, i, f: (b, i, 0)),
                  pl.BlockSpec((1, tm // res_chunks, D),
                               lambda b, i, f: (b, i * res_chunks + jnp.minimum(f, res_chunks - 1), 0)),
                  pl.BlockSpec((1, CONV_HALO, D), lambda b, i, f: (b, jnp.maximum(i * halo_blocks - 1, 0), 0)),
                  pl.BlockSpec((1, CONV_HALO, D),
                               lambda b, i, f: (b, jnp.minimum((i + 1) * halo_blocks, n_halo - 1), 0)),
                  pl.BlockSpec((1, D), lambda b, i, f: (0, 0)),
                  pl.BlockSpec((D, tf), gate),
                  pl.BlockSpec((D, tf), val),
                  pl.BlockSpec((conv_w.shape[0], tf), gate),
                  pl.BlockSpec((conv_w.shape[0], tf), val),
                  pl.BlockSpec((1, tf), gate),
                  pl.BlockSpec((1, tf), val),
                  pl.BlockSpec((tf, D), lambda b, i, f: (f, 0))],
        out_specs=pl.BlockSpec((1, tm, D), lambda b, i, f: (b, i, 0)),
        out_shape=jax.ShapeDtypeStruct((B, S, D), F32),
        scratch_shapes=[pltpu.VMEM((rows, D), BF16)],
        compiler_params=_params(("parallel", "parallel", "arbitrary"), est),
        name="conv_ffn",
    )(h, x, x, x, g_ffn, w_up, w_up, conv_w, conv_w, conv_b, conv_b, w_down)


def _rope_tables(seq, gain, scale):
    half = DIFF_QKDIM // 2
    inv = ROPE_THETA ** (-jnp.arange(half, dtype=F32) / half)
    ang = jnp.arange(seq, dtype=F32)[:, None] * inv[None, :]
    cos = jnp.cos(ang)
    sin = jnp.sin(ang)
    gain = gain.astype(F32)
    a = jnp.concatenate([gain[:half] * cos, gain[half:] * cos], axis=1) * scale
    b = jnp.concatenate([-gain[half:] * sin, gain[:half] * sin], axis=1) * scale
    reps = V7X_LANES // DIFF_QKDIM
    return jnp.tile(a, (1, reps)), jnp.tile(b, (1, reps))


def _segment_ones(width):
    seg = jnp.arange(width) // DIFF_QKDIM
    return (seg[:, None] == seg[None, :]).astype(BF16)


def _tile(n, target):
    t = min(n, target)
    assert n % t == 0, (n, t)
    return t


def _run_trunk(x, mem, layers):
    B, S, D = x.shape
    tm = _tile(S, TOKEN_TILE)
    tq = _tile(S, ATTN_Q_TILE)
    flash_tk = _tile(S // FLASH_BLOCKS_PER_TRIP, ATTN_K_TILE)
    tk = _tile(S // 8, ATTN_K_TILE)
    seg = _segment_ones(V7X_MXU_WIDTH)
    for l, p in enumerate(layers):
        lam_init = 0.8 - 0.6 * math.exp(-0.3 * l)
        q_scale = DIFF_QKDIM ** -0.5 * math.log2(math.e)
        aq, bq = _rope_tables(S, p["g_q"], q_scale)
        ak, bk = _rope_tables(S, p["g_k"], 1.0)
        q, k, v, u = _mix_in(x, p["g_mix"], p["w_in"], seg, aq, bq, ak, bk, tm=tm)
        bound = (DIFF_QKDIM * q_scale * 1.01 * jnp.max(jnp.abs(p["g_q"])) * jnp.max(jnp.abs(p["g_k"]))).astype(F32)
        attn = functools.partial(_diff_attn, p["lam"], q, k, v, p["g_sub"].reshape(-1, 1), lam_init,
                                 tq=tq, tk=flash_tk, blocks_per_trip=FLASH_BLOCKS_PER_TRIP)
        attn_bounded = functools.partial(_diff_attn_bounded, p["lam"], q, k, v, p["g_sub"].reshape(-1, 1), lam_init,
                                         tq=tq, tk=tk, blocks_per_trip=min(ATTN_BLOCKS_PER_TRIP, S // tk),
                                         heads_per_step=ATTN_HEADS_PER_STEP)
        a = lax.cond(bound <= UNSHIFTED_SOFTMAX_SCORE_LIMIT, attn_bounded, attn)
        x = _mix_out(a, u, x, p["w_out_folded"], tm=tm)
        k_mem = _mem_k(mem, p["g_mem"], p["wc_kv"], p["gc_k"])
        vo_mem = _mem_vo(mem, p["g_mem"], p["wc_kv"], p["wc_o"])
        x, h_ffn = _cross_attn(x, p["g_cross"], p["wc_q"], p["gc_q"], k_mem, vo_mem, p["g_ffn"], tm=tm)
        x = _conv_ffn(x, h_ffn, p["g_ffn"], p["w_up"], p["conv_w"], p["conv_b"], p["w_down"],
                      tm=_tile(S, FFN_TOKEN_TILE), tf=FFN_CHUNK)
    return x


def kernel(x_prompt, x_sample, mem_prompt, mem_sample, g_mix, w_in, g_q, g_k, lam_q1, lam_k1, lam_q2, lam_k2,
           g_sub, w_pool, pool_scale, w_out, g_cross, g_mem, wc_q, wc_kv, gc_q, gc_k, wc_o, g_ffn, w_up,
           conv_w, conv_b, w_down):
    depth = w_in.shape[0]
    layers = []
    for l in range(depth):
        lam_init = 0.8 - 0.6 * math.exp(-0.3 * l)
        lam = (jnp.exp(jnp.sum(lam_q1[l].astype(F32) * lam_k1[l].astype(F32)))
               - jnp.exp(jnp.sum(lam_q2[l].astype(F32) * lam_k2[l].astype(F32))) + lam_init)
        row = lambda t: t[l].reshape(1, -1).astype(F32)
        layers.append(dict(
            lam=lam.reshape(1).astype(F32),
            g_mix=row(g_mix), g_q=g_q[l], g_k=g_k[l], g_sub=row(g_sub),
            g_cross=row(g_cross), g_mem=row(g_mem), gc_q=row(gc_q), gc_k=row(gc_k), g_ffn=row(g_ffn),
            conv_w=conv_w[l].astype(F32), conv_b=row(conv_b),
            w_in=w_in[l].astype(BF16),
            w_out_folded=_pool_fold(w_pool[l].astype(BF16), row(pool_scale), w_out[l].astype(BF16)),
            wc_q=wc_q[l].astype(BF16), wc_kv=wc_kv[l].astype(BF16), wc_o=wc_o[l].astype(BF16),
            w_up=w_up[l].astype(BF16), w_down=w_down[l].astype(BF16)))
    return (_run_trunk(x_prompt, mem_prompt, layers), _run_trunk(x_sample, mem_sample, layers))
```

```python
import functools
import math

import jax
import jax.numpy as jnp
from jax import lax
from jax.experimental import pallas as pl
from jax.experimental.pallas import tpu as pltpu

F32 = jnp.float32
BF16 = jnp.bfloat16

N_DIFF_HEADS = 8
DIFF_QKDIM = 64
DIFF_VDIM = 128
HEAD_COLS = 2 * DIFF_QKDIM
POOL_WINDOWS = (2, 4, 8, 16)
POOL_GROUP_WIDTH = 256
POOL_HALO = 16
N_CROSS_HEADS = 4
CONV_HALO = 8
FFN_CHUNK = 512
ROPE_THETA = 10000.0
EPS = 1e-6
UNSHIFTED_SOFTMAX_SCORE_LIMIT = 30.0

V7X_VMEM_BYTES = 64 * 1024 * 1024
V7X_LANES = 128
V7X_MXU_WIDTH = 256
VMEM_LIMIT_CAP = V7X_VMEM_BYTES - 6 * 1024 * 1024

TOKEN_TILE = 512
FFN_TOKEN_TILE = 1024
ATTN_Q_TILE = 512
ATTN_K_TILE = 512
ATTN_BLOCKS_PER_TRIP = 16
ATTN_HEADS_PER_STEP = 4
FLASH_BLOCKS_PER_TRIP = 4


def _vmem_limit(estimate_bytes):
    return int(min(VMEM_LIMIT_CAP, max(32 * 1024 * 1024, estimate_bytes)))


def _params(semantics, vmem_estimate):
    return pltpu.CompilerParams(dimension_semantics=semantics, vmem_limit_bytes=_vmem_limit(vmem_estimate))


def _resident(shape):
    return pl.BlockSpec(shape, lambda *_: (0,) * len(shape), pipeline_mode=pl.Buffered(1))


def _rms(x, gain):
    ms = jnp.mean(x * x, axis=-1, keepdims=True)
    return x * lax.rsqrt(ms + EPS) * gain


def _dot(a, b):
    return jnp.dot(a, b, preferred_element_type=F32)


def _dot_nt(a, b):
    return lax.dot_general(a, b, (((1,), (1,)), ((), ())), preferred_element_type=F32)


def _mix_in_kernel(x_ref, g_ref, w_ref, seg_ref, aq_ref, bq_ref, ak_ref, bk_ref,
                   q_ref, k_ref, v_ref, u_ref, *, qk_width, v_width):
    h = _rms(x_ref[0], g_ref[...]).astype(BF16)
    lane = lax.broadcasted_iota(jnp.int32, (1, V7X_LANES), 1)
    partner_is_above = (lane & (DIFF_QKDIM // 2)) == 0
    seg = seg_ref[...]
    chunk = seg.shape[0]
    for col0, a_ref, b_ref, o_ref in ((0, aq_ref, bq_ref, q_ref), (qk_width, ak_ref, bk_ref, k_ref)):
        a = a_ref[...]
        b = b_ref[...]
        zfull = _dot(h, w_ref[:, col0:col0 + qk_width])
        for c0 in range(0, qk_width, chunk):
            z = zfull[:, c0:c0 + chunk]
            ss = _dot((z * z).astype(BF16), seg)
            zn = z * lax.rsqrt(ss * (1.0 / DIFF_QKDIM) + EPS)
            for c in range(0, chunk, V7X_LANES):
                zc = zn[:, c:c + V7X_LANES]
                partner = jnp.where(partner_is_above,
                                    pltpu.roll(zc, V7X_LANES - DIFF_QKDIM // 2, 1),
                                    pltpu.roll(zc, DIFF_QKDIM // 2, 1))
                o_ref[0, :, c0 + c:c0 + c + V7X_LANES] = (zc * a + partner * b).astype(o_ref.dtype)
    v0 = 2 * qk_width
    v_ref[0] = _dot(h, w_ref[:, v0:v0 + v_width]).astype(v_ref.dtype)
    u_ref[0] = _dot(h, w_ref[:, v0 + v_width:])


def _mix_in(x, g, w_in, seg, aq, bq, ak, bk, *, tm):
    B, S, D = x.shape
    qk_width = N_DIFF_HEADS * HEAD_COLS
    v_width = N_DIFF_HEADS * DIFF_VDIM
    u_width = w_in.shape[1] - 2 * qk_width - v_width
    row = lambda b, i: (b, i, 0)
    tab = pl.BlockSpec((tm, V7X_LANES), lambda b, i: (i, 0))
    est = (w_in.size * 2 + 2 * tm * D * 4 + 2 * tm * (2 * qk_width + v_width) * 2 + 2 * tm * u_width * 4
           + tm * D * 2 + 8 * tm * 1024 * 4 + 8 * tm * V7X_LANES * 4 + (4 << 20))
    return pl.pallas_call(
        functools.partial(_mix_in_kernel, qk_width=qk_width, v_width=v_width),
        grid=(B, S // tm),
        in_specs=[pl.BlockSpec((1, tm, D), row),
                  pl.BlockSpec((1, D), lambda b, i: (0, 0)),
                  _resident(w_in.shape),
                  pl.BlockSpec(seg.shape, lambda b, i: (0, 0)),
                  tab, tab, tab, tab],
        out_specs=[pl.BlockSpec((1, tm, qk_width), row),
                   pl.BlockSpec((1, tm, qk_width), row),
                   pl.BlockSpec((1, tm, v_width), row),
                   pl.BlockSpec((1, tm, u_width), row)],
        out_shape=[jax.ShapeDtypeStruct((B, S, qk_width), BF16),
                   jax.ShapeDtypeStruct((B, S, qk_width), BF16),
                   jax.ShapeDtypeStruct((B, S, v_width), BF16),
                   jax.ShapeDtypeStruct((B, S, u_width), F32)],
        compiler_params=_params(("parallel", "parallel"), est),
        name="mix_in",
    )(x, g, w_in, seg, aq, bq, ak, bk)


def _dot_tn(a, b):
    return lax.dot_general(a, b, (((0,), (0,)), ((), ())), preferred_element_type=F32)


def _diff_attn_kernel(lam_ref, q_ref, k_ref, v_ref, gsub_ref, o_ref, qbd_ref, s0_ref, s1_ref, m_ref, l_ref, acc_ref,
                      *, tq, tk, blocks_per_trip, out_scale):
    qt = q_ref[0].astype(F32).T
    row = lax.broadcasted_iota(jnp.int32, (HEAD_COLS, 1), 0)
    zero = jnp.zeros_like(qt)
    qbd_ref[:, 0:tq] = jnp.where(row < DIFF_QKDIM, qt, zero).astype(BF16)
    qbd_ref[:, tq:2 * tq] = jnp.where(row >= DIFF_QKDIM, qt, zero).astype(BF16)
    m_ref[...] = jnp.full(m_ref.shape, -jnp.inf, F32)
    l_ref[...] = jnp.zeros(l_ref.shape, F32)
    acc_ref[...] = jnp.zeros(acc_ref.shape, F32)

    def scores(j, s_ref):
        k0 = pl.multiple_of(j * tk, tk)
        s_ref[...] = _dot(k_ref[0, pl.ds(k0, tk), :], qbd_ref[...])

    def absorb(j, s_ref):
        k0 = pl.multiple_of(j * tk, tk)
        vb = v_ref[0, pl.ds(k0, tk), :]
        s = s_ref[...]
        m_prev = m_ref[...]
        m_new = jnp.maximum(m_prev, jnp.max(s, axis=0, keepdims=True))
        alpha = jnp.exp2(m_prev - m_new)
        p = jnp.exp2(s - m_new)
        l_ref[...] = alpha * l_ref[...] + jnp.sum(p, axis=0, keepdims=True)
        m_ref[...] = m_new
        pb = p.astype(BF16)
        for c in range(2):
            cols = slice(c * tq, (c + 1) * tq)
            acc_ref[c] = alpha[:, cols] * acc_ref[c] + _dot_tn(vb, pb[:, cols])

    bufs = (s0_ref, s1_ref)
    n_trips = k_ref.shape[1] // (blocks_per_trip * tk)
    scores(0, s0_ref)

    def trip(i, carry):
        j = blocks_per_trip * i
        for u in range(blocks_per_trip):
            scores(j + u + 1, bufs[(u + 1) % 2])
            absorb(j + u, bufs[u % 2])
        return carry

    lax.fori_loop(0, n_trips - 1, trip, 0)
    j_last = blocks_per_trip * (n_trips - 1)
    for u in range(blocks_per_trip):
        if u + 1 < blocks_per_trip:
            scores(j_last + u + 1, bufs[(u + 1) % 2])
        absorb(j_last + u, bufs[u % 2])

    inv = 1.0 / l_ref[...]
    o = acc_ref[0] * inv[:, 0:tq] - lam_ref[0] * (acc_ref[1] * inv[:, tq:2 * tq])
    ms = jnp.mean(o * o, axis=0, keepdims=True)
    o = o * lax.rsqrt(ms + EPS) * gsub_ref[...] * out_scale
    o_ref[0] = o.T.astype(o_ref.dtype)


def _diff_attn(lam, q, k, v, g_sub_col, lam_init, *, tq, tk, blocks_per_trip):
    B, S, _ = q.shape
    assert blocks_per_trip % 2 == 0 and S % (blocks_per_trip * tk) == 0, (S, tk, blocks_per_trip)
    est = (2 * 2 * S * (HEAD_COLS + DIFF_VDIM) * 2 + 4 * tq * HEAD_COLS * 2 + 2 * tq * HEAD_COLS * 2
           + 2 * tq * DIFF_VDIM * 4 + 8 * tk * 2 * tq * 4 + (4 << 20))
    return pl.pallas_call(
        functools.partial(_diff_attn_kernel, tq=tq, tk=tk, blocks_per_trip=blocks_per_trip,
                          out_scale=1.0 - lam_init),
        grid=(B, N_DIFF_HEADS, S // tq),
        in_specs=[pl.BlockSpec(memory_space=pltpu.SMEM),
                  pl.BlockSpec((1, tq, HEAD_COLS), lambda b, h, i: (b, i, h)),
                  pl.BlockSpec((1, S, HEAD_COLS), lambda b, h, i: (b, 0, h)),
                  pl.BlockSpec((1, S, DIFF_VDIM), lambda b, h, i: (b, 0, h)),
                  pl.BlockSpec((DIFF_VDIM, 1), lambda b, h, i: (0, 0))],
        out_specs=pl.BlockSpec((1, tq, DIFF_VDIM), lambda b, h, i: (b, i, h)),
        out_shape=jax.ShapeDtypeStruct((B, S, N_DIFF_HEADS * DIFF_VDIM), BF16),
        scratch_shapes=[pltpu.VMEM((HEAD_COLS, 2 * tq), BF16),
                        pltpu.VMEM((tk, 2 * tq), F32),
                        pltpu.VMEM((tk, 2 * tq), F32),
                        pltpu.VMEM((1, 2 * tq), F32),
                        pltpu.VMEM((1, 2 * tq), F32),
                        pltpu.VMEM((2, DIFF_VDIM, tq), F32)],
        compiler_params=_params(("parallel", "parallel", "parallel"), est),
        name="diff_attn",
    )(lam, q, k, v, g_sub_col)


def _diff_attn_bounded_kernel(lam_ref, q_ref, k_ref, v_ref, gsub_ref, o_ref, qbd_ref, l_ref, acc_ref,
                              *, tq, tk, blocks_per_trip, out_scale):
    row = lax.broadcasted_iota(jnp.int32, (HEAD_COLS, 1), 0)
    for hh in range(q_ref.shape[2] // HEAD_COLS):
        qk_cols = slice(hh * HEAD_COLS, (hh + 1) * HEAD_COLS)
        v_cols = slice(hh * DIFF_VDIM, (hh + 1) * DIFF_VDIM)
        qt = q_ref[0, :, qk_cols].astype(F32).T
        zero = jnp.zeros_like(qt)
        qbd_ref[hh, :, 0:tq] = jnp.where(row < DIFF_QKDIM, qt, zero).astype(BF16)
        qbd_ref[hh, :, tq:2 * tq] = jnp.where(row >= DIFF_QKDIM, qt, zero).astype(BF16)
        l_ref[hh] = jnp.zeros(l_ref.shape[1:], F32)
        acc_ref[hh] = jnp.zeros(acc_ref.shape[1:], F32)

        def trip(i, carry):
            def scores(u):
                k0 = pl.multiple_of((blocks_per_trip * i + u) * tk, tk)
                return _dot(k_ref[0, pl.ds(k0, tk), qk_cols], qbd_ref[hh])

            s = scores(0)
            for u in range(blocks_per_trip):
                s_next = scores(u + 1) if u + 1 < blocks_per_trip else None
                k0 = pl.multiple_of((blocks_per_trip * i + u) * tk, tk)
                vb = v_ref[0, pl.ds(k0, tk), v_cols]
                p = jnp.exp2(s)
                l_ref[hh] += jnp.sum(p, axis=0, keepdims=True)
                pb = p.astype(BF16)
                for c in range(2):
                    acc_ref[hh, c] += _dot_tn(vb, pb[:, c * tq:(c + 1) * tq])
                s = s_next
            return carry

        lax.fori_loop(0, k_ref.shape[1] // (blocks_per_trip * tk), trip, 0)

        inv = 1.0 / l_ref[hh]
        o = acc_ref[hh, 0] * inv[:, 0:tq] - lam_ref[0] * (acc_ref[hh, 1] * inv[:, tq:2 * tq])
        ms = jnp.mean(o * o, axis=0, keepdims=True)
        o = o * lax.rsqrt(ms + EPS) * gsub_ref[...] * out_scale
        o_ref[0, :, v_cols] = o.T.astype(o_ref.dtype)


def _diff_attn_bounded(lam, q, k, v, g_sub_col, lam_init, *, tq, tk, blocks_per_trip, heads_per_step):
    B, S, _ = q.shape
    assert S % (blocks_per_trip * tk) == 0, (S, tk, blocks_per_trip)
    assert N_DIFF_HEADS % heads_per_step == 0
    hps = heads_per_step
    est = (hps * (2 * 2 * S * (HEAD_COLS + DIFF_VDIM) * 2 + 4 * tq * HEAD_COLS * 2 + 2 * tq * HEAD_COLS * 2
                  + 2 * tq * DIFF_VDIM * 4) + 8 * tk * 2 * tq * 4 + (4 << 20))
    return pl.pallas_call(
        functools.partial(_diff_attn_bounded_kernel, tq=tq, tk=tk, blocks_per_trip=blocks_per_trip,
                          out_scale=1.0 - lam_init),
        grid=(B, N_DIFF_HEADS // hps, S // tq),
        in_specs=[pl.BlockSpec(memory_space=pltpu.SMEM),
                  pl.BlockSpec((1, tq, hps * HEAD_COLS), lambda b, h, i: (b, i, h)),
                  pl.BlockSpec((1, S, hps * HEAD_COLS), lambda b, h, i: (b, 0, h)),
                  pl.BlockSpec((1, S, hps * DIFF_VDIM), lambda b, h, i: (b, 0, h)),
                  pl.BlockSpec((DIFF_VDIM, 1), lambda b, h, i: (0, 0))],
        out_specs=pl.BlockSpec((1, tq, hps * DIFF_VDIM), lambda b, h, i: (b, i, h)),
        out_shape=jax.ShapeDtypeStruct((B, S, N_DIFF_HEADS * DIFF_VDIM), BF16),
        scratch_shapes=[pltpu.VMEM((hps, HEAD_COLS, 2 * tq), BF16),
                        pltpu.VMEM((hps, 1, 2 * tq), F32),
                        pltpu.VMEM((hps, 2, DIFF_VDIM, tq), F32)],
        compiler_params=_params(("parallel", "parallel", "parallel"), est),
        name="diff_attn_bounded",
    )(lam, q, k, v, g_sub_col)


def _pool_fold_kernel(wp_ref, ps_ref, wo_ref, o_ref):
    w = (wp_ref[0].astype(F32) * ps_ref[...]).astype(BF16)
    o_ref[...] = _dot(w, wo_ref[...]).astype(o_ref.dtype)


def _pool_fold(w_pool, pool_scale, w_out):
    groups, gw, _ = w_pool.shape
    D = w_out.shape[1]
    a_width = w_out.shape[0] - groups * gw
    folded = pl.pallas_call(
        _pool_fold_kernel,
        grid=(groups,),
        in_specs=[pl.BlockSpec((1, gw, gw), lambda g: (g, 0, 0)),
                  pl.BlockSpec((1, gw), lambda g: (0, g)),
                  pl.BlockSpec((gw, D), lambda g: (a_width // gw + g, 0))],
        out_specs=pl.BlockSpec((gw, D), lambda g: (g, 0)),
        out_shape=jax.ShapeDtypeStruct((groups * gw, D), BF16),
        compiler_params=_params(("parallel",), 8 * gw * D * 4 + (4 << 20)),
        name="pool_fold",
    )(w_pool, pool_scale, w_out)
    return jnp.concatenate([w_out[:a_width], folded], axis=0)


def _mix_out_kernel(a_ref, u_ref, up_ref, un_ref, x_ref, wo_ref, o_ref, *, tm, seq):
    i = pl.program_id(1)
    u_prev = jnp.where(i > 0, up_ref[0], 0.0)
    u_next = jnp.where(i < pl.num_programs(1) - 1, un_ref[0], 0.0)
    ue = jnp.concatenate([u_prev, u_ref[0], u_next], axis=0)
    rows = tm + 2 * POOL_HALO
    pos = i * tm + lax.broadcasted_iota(jnp.int32, (tm, 1), 0)
    a_width = a_ref.shape[2]
    acc = x_ref[0] + _dot(a_ref[0], wo_ref[0:a_width, :])
    pooled = []
    for g, w in enumerate(POOL_WINDOWS):
        c0 = g * POOL_GROUP_WIDTH
        ug = ue[:, c0:c0 + POOL_GROUP_WIDTH]
        win = ug + pltpu.roll(ug, 1, 0)
        shift = 1
        while 2 * shift < w:
            win = pltpu.roll(win, shift, 0) + pltpu.roll(win, rows - shift, 0)
            shift *= 2
        win = win[POOL_HALO:POOL_HALO + tm]
        cnt = jnp.minimum(pos + w // 2, seq) - jnp.maximum(pos - w // 2, 0)
        z = win / cnt.astype(F32) - ug[POOL_HALO:POOL_HALO + tm]
        pooled.append(z.astype(BF16))
    o_ref[0] = acc + _dot(jnp.concatenate(pooled, axis=1), wo_ref[a_width:, :])


def _mix_out(a, u, x, w_out, *, tm):
    B, S, D = x.shape
    a_width, u_width = a.shape[2], u.shape[2]
    halo_blocks = tm // POOL_HALO
    n_halo = S // POOL_HALO
    row = lambda b, i: (b, i, 0)
    est = (w_out.size * 2 + 4 * tm * D * 4 + 2 * tm * a_width * 2 + 2 * tm * u_width * 4
           + 6 * tm * u_width * 4 + 2 * tm * D * 4 + (4 << 20))
    return pl.pallas_call(
        functools.partial(_mix_out_kernel, tm=tm, seq=S),
        grid=(B, S // tm),
        in_specs=[pl.BlockSpec((1, tm, a_width), row),
                  pl.BlockSpec((1, tm, u_width), row),
                  pl.BlockSpec((1, POOL_HALO, u_width),
                               lambda b, i: (b, jnp.maximum(i * halo_blocks - 1, 0), 0)),
                  pl.BlockSpec((1, POOL_HALO, u_width),
                               lambda b, i: (b, jnp.minimum((i + 1) * halo_blocks, n_halo - 1), 0)),
                  pl.BlockSpec((1, tm, D), row),
                  _resident(w_out.shape)],
        out_specs=pl.BlockSpec((1, tm, D), row),
        out_shape=jax.ShapeDtypeStruct((B, S, D), F32),
        compiler_params=_params(("parallel", "parallel"), est),
        name="mix_out",
    )(a, u, u, u, x, w_out)


def _mem_k_kernel(mem_ref, g_ref, w_ref, gk_ref, o_ref):
    m = _rms(mem_ref[0], g_ref[...]).astype(BF16)
    k = _dot(m, w_ref[...])
    o_ref[0] = _rms(k, gk_ref[...]).astype(o_ref.dtype)


def _mem_k(mem, g_mem, wc_kv, gc_k):
    B, n_mem, D = mem.shape
    hd = D // N_CROSS_HEADS
    est = 2 * n_mem * D * 4 + 2 * D * hd * 2 + 4 * n_mem * hd * 4 + n_mem * D * 4 + (4 << 20)
    return pl.pallas_call(
        _mem_k_kernel,
        grid=(N_CROSS_HEADS, B),
        in_specs=[pl.BlockSpec((1, n_mem, D), lambda h, b: (b, 0, 0)),
                  pl.BlockSpec((1, D), lambda h, b: (0, 0)),
                  pl.BlockSpec((D, hd), lambda h, b: (0, h)),
                  pl.BlockSpec((1, hd), lambda h, b: (0, 0))],
        out_specs=pl.BlockSpec((1, n_mem, hd), lambda h, b: (b, 0, h)),
        out_shape=jax.ShapeDtypeStruct((B, n_mem, D), BF16),
        compiler_params=_params(("parallel", "parallel"), est),
        name="mem_k",
    )(mem, g_mem, wc_kv, gc_k)


def _mem_vo_kernel(mem_ref, g_ref, w_ref, wo_ref, o_ref):
    m = _rms(mem_ref[0], g_ref[...]).astype(BF16)
    v = _dot(m, w_ref[...]).astype(BF16)
    o_ref[0] = _dot(v, wo_ref[...]).astype(o_ref.dtype)


def _mem_vo(mem, g_mem, wc_kv, wc_o):
    B, n_mem, D = mem.shape
    hd = D // N_CROSS_HEADS
    est = (2 * n_mem * D * 4 + 2 * D * hd * 2 + 2 * hd * D * 2 + 2 * n_mem * D * 2 + 4 * n_mem * D * 4 + (4 << 20))
    return pl.pallas_call(
        _mem_vo_kernel,
        grid=(N_CROSS_HEADS, B),
        in_specs=[pl.BlockSpec((1, n_mem, D), lambda h, b: (b, 0, 0)),
                  pl.BlockSpec((1, D), lambda h, b: (0, 0)),
                  pl.BlockSpec((D, hd), lambda h, b: (0, N_CROSS_HEADS + h)),
                  pl.BlockSpec((hd, D), lambda h, b: (h, 0))],
        out_specs=pl.BlockSpec((1, n_mem, D), lambda h, b: (b, h, 0)),
        out_shape=jax.ShapeDtypeStruct((B, N_CROSS_HEADS * n_mem, D), BF16),
        compiler_params=_params(("parallel", "parallel"), est),
        name="mem_vo",
    )(mem, g_mem, wc_kv, wc_o)


def _cross_attn_kernel(x_ref, g_ref, wq_ref, gq_ref, k_ref, vo_ref, gn_ref, o_ref, hn_ref):
    x = x_ref[0]
    h = _rms(x, g_ref[...]).astype(BF16)
    hd = gq_ref.shape[1]
    scale = hd ** -0.5
    heads = [slice(c0, c0 + hd) for c0 in range(0, x.shape[1], hd)]
    qs = [_dot(h, wq_ref[:, c]) for c in heads]
    qn = [_rms(q, gq_ref[...]).astype(BF16) for q in qs]
    ss = [_dot_nt(q, k_ref[0, :, c]) * scale for q, c in zip(qn, heads)]
    ps = []
    for s in ss:
        p = jnp.exp(s - jnp.max(s, axis=-1, keepdims=True))
        ps.append((p / jnp.sum(p, axis=-1, keepdims=True)).astype(BF16))
    out = x + _dot(jnp.concatenate(ps, axis=1), vo_ref[0])
    o_ref[0] = out
    hn_ref[0] = _rms(out, gn_ref[...]).astype(hn_ref.dtype)


def _cross_attn(x, g_cross, wc_q, gc_q, k_mem, vo_mem, g_next, *, tm):
    B, S, D = x.shape
    n_mem = k_mem.shape[1]
    row = lambda b, i: (b, i, 0)
    est = (wc_q.size * 2 + 2 * n_mem * D * 2 + 2 * vo_mem.shape[1] * D * 2 + 4 * tm * D * 4 + 5 * tm * D * 4
           + (4 << 20))
    return pl.pallas_call(
        _cross_attn_kernel,
        grid=(B, S // tm),
        in_specs=[pl.BlockSpec((1, tm, D), row),
                  pl.BlockSpec((1, D), lambda b, i: (0, 0)),
                  _resident(wc_q.shape),
                  pl.BlockSpec(gc_q.shape, lambda b, i: (0, 0)),
                  pl.BlockSpec((1, n_mem, D), lambda b, i: (b, 0, 0)),
                  pl.BlockSpec((1, vo_mem.shape[1], D), lambda b, i: (b, 0, 0)),
                  pl.BlockSpec((1, D), lambda b, i: (0, 0))],
        out_specs=[pl.BlockSpec((1, tm, D), row), pl.BlockSpec((1, tm, D), row)],
        out_shape=[jax.ShapeDtypeStruct((B, S, D), F32), jax.ShapeDtypeStruct((B, S, D), BF16)],
        compiler_params=_params(("parallel", "parallel"), est),
        name="cross_attn",
    )(x, g_cross, wc_q, gc_q, k_mem, vo_mem, g_next)


def _conv_ffn_kernel(hin_ref, xr_ref, xp_ref, xn_ref, g_ref, wg_ref, wv_ref, cg_ref, cv_ref,
                     wd_ref, o_ref, h_ref, *, tm, res_chunks):
    i = pl.program_id(1)
    f = pl.program_id(2)
    rows = tm + 2 * CONV_HALO
    slab = tm // res_chunks

    @pl.when(f == 0)
    def _():
        g = g_ref[...]
        h_prev = jnp.where(i > 0, _rms(xp_ref[0], g), 0.0)
        h_next = jnp.where(i < pl.num_programs(1) - 1, _rms(xn_ref[0], g), 0.0)
        h_ref[0:CONV_HALO, :] = h_prev.astype(BF16)
        h_ref[CONV_HALO:CONV_HALO + tm, :] = hin_ref[0]
        h_ref[CONV_HALO + tm:rows, :] = h_next.astype(BF16)
        o_ref[0] = jnp.zeros(o_ref.shape[1:], o_ref.dtype)

    h = h_ref[...]

    def conv(w_ref, c_ref):
        u = _dot(h, w_ref[...])
        below = pltpu.roll(u, 1, 0)[CONV_HALO:CONV_HALO + tm]
        above = pltpu.roll(u, rows - 1, 0)[CONV_HALO:CONV_HALO + tm]
        mid = u[CONV_HALO:CONV_HALO + tm]
        return below * c_ref[0:1, :] + mid * c_ref[1:2, :] + above * c_ref[2:3, :] + c_ref[3:4, :]

    act = jax.nn.gelu(conv(wg_ref, cg_ref)) * conv(wv_ref, cv_ref)
    o_ref[0] += _dot(act.astype(BF16), wd_ref[...])

    @pl.when(f < res_chunks)
    def _():
        r0 = pl.multiple_of(f * slab, slab)
        o_ref[0, pl.ds(r0, slab), :] += xr_ref[0]


def _conv_ffn(x, h, g_ffn, w_up, conv_taps_bias, w_down, *, tm, tf):
    B, S, D = x.shape
    d_ff = w_down.shape[0]
    nf = d_ff // tf
    halo_blocks = tm // CONV_HALO
    n_halo = S // CONV_HALO
    rows = tm + 2 * CONV_HALO
    res_chunks = min(nf, 8)
    assert tm % (res_chunks * CONV_HALO) == 0, (tm, res_chunks)
    est = (2 * tm * D * 4 + 2 * tm * D * 2 + 2 * 3 * D * tf * 2 + rows * D * 2 + 8 * rows * tf * 4 + 2 * tm * D * 4
           + (4 << 20))
    gate = lambda b, i, f: (0, f)
    val = lambda b, i, f: (0, nf + f)
    return pl.pallas_call(
        functools.partial(_conv_ffn_kernel, tm=tm, res_chunks=res_chunks),
        grid=(B, S // tm, nf),
        in_specs=[pl.BlockSpec((1, tm, D), lambda b, i, f: (b, i, 0)),
                  pl.BlockSpec((1, tm // res_chunks, D),
                               lambda b, i, f: (b, i * res_chunks + jnp.minimum(f, res_chunks - 1), 0)),
                  pl.BlockSpec((1, CONV_HALO, D), lambda b, i, f: (b, jnp.maximum(i * halo_blocks - 1, 0), 0)),
                  pl.BlockSpec((1, CONV_HALO, D),
                               lambda b, i, f: (b, jnp.minimum((i + 1) * halo_blocks, n_halo - 1), 0)),
                  pl.BlockSpec((1, D), lambda b, i, f: (0, 0)),
                  pl.BlockSpec((D, tf), gate),
                  pl.BlockSpec((D, tf), val),
                  pl.BlockSpec((conv_taps_bias.shape[0], tf), gate),
                  pl.BlockSpec((conv_taps_bias.shape[0], tf), val),
                  pl.BlockSpec((tf, D), lambda b, i, f: (f, 0))],
        out_specs=pl.BlockSpec((1, tm, D), lambda b, i, f: (b, i, 0)),
        out_shape=jax.ShapeDtypeStruct((B, S, D), F32),
        scratch_shapes=[pltpu.VMEM((rows, D), BF16)],
        compiler_params=_params(("parallel", "parallel", "arbitrary"), est),
        name="conv_ffn",
    )(h, x, x, x, g_ffn, w_up, w_up, conv_taps_bias, conv_taps_bias, w_down)


def _rope_tables(seq, gain, scale):
    half = DIFF_QKDIM // 2
    inv = ROPE_THETA ** (-jnp.arange(half, dtype=F32) / half)
    ang = jnp.arange(seq, dtype=F32)[:, None] * inv[None, :]
    cos = jnp.cos(ang)
    sin = jnp.sin(ang)
    gain = gain.astype(F32)
    a = jnp.concatenate([gain[:half] * cos, gain[half:] * cos], axis=1) * scale
    b = jnp.concatenate([-gain[half:] * sin, gain[:half] * sin], axis=1) * scale
    reps = V7X_LANES // DIFF_QKDIM
    return jnp.tile(a, (1, reps)), jnp.tile(b, (1, reps))


def _segment_ones(width):
    seg = jnp.arange(width) // DIFF_QKDIM
    return (seg[:, None] == seg[None, :]).astype(BF16)


def _tile(n, target):
    t = min(n, target)
    assert n % t == 0, (n, t)
    return t


def _run_trunk(x, mem, layers):
    B, S, D = x.shape
    tm = _tile(S, TOKEN_TILE)
    tq = _tile(S, ATTN_Q_TILE)
    flash_tk = _tile(S // FLASH_BLOCKS_PER_TRIP, ATTN_K_TILE)
    tk = _tile(S // 8, ATTN_K_TILE)
    seg = _segment_ones(V7X_MXU_WIDTH)
    for l, p in enumerate(layers):
        lam_init = 0.8 - 0.6 * math.exp(-0.3 * l)
        q_scale = DIFF_QKDIM ** -0.5 * math.log2(math.e)
        aq, bq = _rope_tables(S, p["g_q"], q_scale)
        ak, bk = _rope_tables(S, p["g_k"], 1.0)
        q, k, v, u = _mix_in(x, p["g_mix"], p["w_in"], seg, aq, bq, ak, bk, tm=tm)
        bound = (DIFF_QKDIM * q_scale * 1.01 * jnp.max(jnp.abs(p["g_q"])) * jnp.max(jnp.abs(p["g_k"]))).astype(F32)
        attn = functools.partial(_diff_attn, p["lam"], q, k, v, p["g_sub"].reshape(-1, 1), lam_init,
                                 tq=tq, tk=flash_tk, blocks_per_trip=FLASH_BLOCKS_PER_TRIP)
        attn_bounded = functools.partial(_diff_attn_bounded, p["lam"], q, k, v, p["g_sub"].reshape(-1, 1), lam_init,
                                         tq=tq, tk=tk, blocks_per_trip=min(ATTN_BLOCKS_PER_TRIP, S // tk),
                                         heads_per_step=ATTN_HEADS_PER_STEP)
        a = lax.cond(bound <= UNSHIFTED_SOFTMAX_SCORE_LIMIT, attn_bounded, attn)
        x = _mix_out(a, u, x, p["w_out_folded"], tm=tm)
        k_mem = _mem_k(mem, p["g_mem"], p["wc_kv"], p["gc_k"])
        vo_mem = _mem_vo(mem, p["g_mem"], p["wc_kv"], p["wc_o"])
        x, h_ffn = _cross_attn(x, p["g_cross"], p["wc_q"], p["gc_q"], k_mem, vo_mem, p["g_ffn"], tm=tm)
        x = _conv_ffn(x, h_ffn, p["g_ffn"], p["w_up"], p["conv_taps_bias"], p["w_down"],
                      tm=_tile(S, FFN_TOKEN_TILE), tf=FFN_CHUNK)
    return x


def kernel(x_prompt, x_sample, mem_prompt, mem_sample, g_mix, w_in, g_q, g_k, lam_q1, lam_k1, lam_q2, lam_k2,
           g_sub, w_pool, pool_scale, w_out, g_cross, g_mem, wc_q, wc_kv, gc_q, gc_k, wc_o, g_ffn, w_up,
           conv_w, conv_b, w_down):
    depth = w_in.shape[0]
    layers = []
    for l in range(depth):
        lam_init = 0.8 - 0.6 * math.exp(-0.3 * l)
        lam = (jnp.exp(jnp.sum(lam_q1[l].astype(F32) * lam_k1[l].astype(F32)))
               - jnp.exp(jnp.sum(lam_q2[l].astype(F32) * lam_k2[l].astype(F32))) + lam_init)
        row = lambda t: t[l].reshape(1, -1).astype(F32)
        layers.append(dict(
            lam=lam.reshape(1).astype(F32),
            g_mix=row(g_mix), g_q=g_q[l], g_k=g_k[l], g_sub=row(g_sub),
            g_cross=row(g_cross), g_mem=row(g_mem), gc_q=row(gc_q), gc_k=row(gc_k), g_ffn=row(g_ffn),
            conv_taps_bias=jnp.concatenate([conv_w[l].astype(F32), row(conv_b)], axis=0),
            w_in=w_in[l].astype(BF16),
            w_out_folded=_pool_fold(w_pool[l].astype(BF16), row(pool_scale), w_out[l].astype(BF16)),
            wc_q=wc_q[l].astype(BF16), wc_kv=wc_kv[l].astype(BF16), wc_o=wc_o[l].astype(BF16),
            w_up=w_up[l].astype(BF16), w_down=w_down[l].astype(BF16)))
    return (_run_trunk(x_prompt, mem_prompt, layers), _run_trunk(x_sample, mem_sample, layers))
```

```python
import functools
import math

import jax
import jax.numpy as jnp
from jax import lax
from jax.experimental import pallas as pl
from jax.experimental.pallas import tpu as pltpu

F32 = jnp.float32
BF16 = jnp.bfloat16

N_DIFF_HEADS = 8
DIFF_QKDIM = 64
DIFF_VDIM = 128
HEAD_COLS = 2 * DIFF_QKDIM
POOL_WINDOWS = (2, 4, 8, 16)
POOL_GROUP_WIDTH = 256
POOL_HALO = 16
N_CROSS_HEADS = 4
CONV_HALO = 8
FFN_CHUNK = 512
ROPE_THETA = 10000.0
EPS = 1e-6
UNSHIFTED_SOFTMAX_SCORE_LIMIT = 30.0

V7X_VMEM_BYTES = 64 * 1024 * 1024
V7X_LANES = 128
V7X_MXU_WIDTH = 256
VMEM_LIMIT_CAP = V7X_VMEM_BYTES - 6 * 1024 * 1024

TOKEN_TILE = 512
FFN_TOKEN_TILE = 1024
ATTN_Q_TILE = 512
ATTN_K_TILE = 512
ATTN_BLOCKS_PER_TRIP = 16
ATTN_HEADS_PER_STEP = 4
FLASH_BLOCKS_PER_TRIP = 4


def _vmem_limit(estimate_bytes):
    return int(min(VMEM_LIMIT_CAP, max(32 * 1024 * 1024, estimate_bytes)))


def _params(semantics, vmem_estimate):
    return pltpu.CompilerParams(dimension_semantics=semantics, vmem_limit_bytes=_vmem_limit(vmem_estimate))


def _resident(shape):
    return pl.BlockSpec(shape, lambda *_: (0,) * len(shape), pipeline_mode=pl.Buffered(1))


def _rms(x, gain):
    ms = jnp.mean(x * x, axis=-1, keepdims=True)
    return x * lax.rsqrt(ms + EPS) * gain


def _dot(a, b):
    return jnp.dot(a, b, preferred_element_type=F32)


def _dot_nt(a, b):
    return lax.dot_general(a, b, (((1,), (1,)), ((), ())), preferred_element_type=F32)


def _mix_in_kernel(x_ref, g_ref, w_ref, seg_ref, aq_ref, bq_ref, ak_ref, bk_ref,
                   q_ref, k_ref, v_ref, u_ref, *, qk_width, v_width):
    h = _rms(x_ref[0], g_ref[...]).astype(BF16)
    lane = lax.broadcasted_iota(jnp.int32, (1, V7X_LANES), 1)
    partner_is_above = (lane & (DIFF_QKDIM // 2)) == 0
    seg = seg_ref[...]
    chunk = seg.shape[0]
    for col0, a_ref, b_ref, o_ref in ((0, aq_ref, bq_ref, q_ref), (qk_width, ak_ref, bk_ref, k_ref)):
        a = a_ref[...]
        b = b_ref[...]
        zfull = _dot(h, w_ref[:, col0:col0 + qk_width])
        for c0 in range(0, qk_width, chunk):
            z = zfull[:, c0:c0 + chunk]
            ss = _dot((z * z).astype(BF16), seg)
            zn = z * lax.rsqrt(ss * (1.0 / DIFF_QKDIM) + EPS)
            for c in range(0, chunk, V7X_LANES):
                zc = zn[:, c:c + V7X_LANES]
                partner = jnp.where(partner_is_above,
                                    pltpu.roll(zc, V7X_LANES - DIFF_QKDIM // 2, 1),
                                    pltpu.roll(zc, DIFF_QKDIM // 2, 1))
                o_ref[0, :, c0 + c:c0 + c + V7X_LANES] = (zc * a + partner * b).astype(o_ref.dtype)
    v0 = 2 * qk_width
    v_ref[0] = _dot(h, w_ref[:, v0:v0 + v_width]).astype(v_ref.dtype)
    u_ref[0] = _dot(h, w_ref[:, v0 + v_width:])


def _mix_in(x, g, w_in, seg, aq, bq, ak, bk, *, tm):
    B, S, D = x.shape
    qk_width = N_DIFF_HEADS * HEAD_COLS
    v_width = N_DIFF_HEADS * DIFF_VDIM
    u_width = w_in.shape[1] - 2 * qk_width - v_width
    row = lambda b, i: (b, i, 0)
    tab = pl.BlockSpec((tm, V7X_LANES), lambda b, i: (i, 0))
    est = (w_in.size * 2 + 2 * tm * D * 4 + 2 * tm * (2 * qk_width + v_width) * 2 + 2 * tm * u_width * 4
           + tm * D * 2 + 8 * tm * 1024 * 4 + 8 * tm * V7X_LANES * 4 + (4 << 20))
    return pl.pallas_call(
        functools.partial(_mix_in_kernel, qk_width=qk_width, v_width=v_width),
        grid=(B, S // tm),
        in_specs=[pl.BlockSpec((1, tm, D), row),
                  pl.BlockSpec((1, D), lambda b, i: (0, 0)),
                  _resident(w_in.shape),
                  pl.BlockSpec(seg.shape, lambda b, i: (0, 0)),
                  tab, tab, tab, tab],
        out_specs=[pl.BlockSpec((1, tm, qk_width), row),
                   pl.BlockSpec((1, tm, qk_width), row),
                   pl.BlockSpec((1, tm, v_width), row),
                   pl.BlockSpec((1, tm, u_width), row)],
        out_shape=[jax.ShapeDtypeStruct((B, S, qk_width), BF16),
                   jax.ShapeDtypeStruct((B, S, qk_width), BF16),
                   jax.ShapeDtypeStruct((B, S, v_width), BF16),
                   jax.ShapeDtypeStruct((B, S, u_width), F32)],
        compiler_params=_params(("parallel", "parallel"), est),
        name="mix_in",
    )(x, g, w_in, seg, aq, bq, ak, bk)


def _dot_tn(a, b):
    return lax.dot_general(a, b, (((0,), (0,)), ((), ())), preferred_element_type=F32)


def _diff_attn_kernel(lam_ref, q_ref, k_ref, v_ref, gsub_ref, o_ref, qbd_ref, s0_ref, s1_ref, m_ref, l_ref, acc_ref,
                      *, tq, tk, blocks_per_trip, out_scale):
    qt = q_ref[0].astype(F32).T
    row = lax.broadcasted_iota(jnp.int32, (HEAD_COLS, 1), 0)
    zero = jnp.zeros_like(qt)
    qbd_ref[:, 0:tq] = jnp.where(row < DIFF_QKDIM, qt, zero).astype(BF16)
    qbd_ref[:, tq:2 * tq] = jnp.where(row >= DIFF_QKDIM, qt, zero).astype(BF16)
    m_ref[...] = jnp.full(m_ref.shape, -jnp.inf, F32)
    l_ref[...] = jnp.zeros(l_ref.shape, F32)
    acc_ref[...] = jnp.zeros(acc_ref.shape, F32)

    def scores(j, s_ref):
        k0 = pl.multiple_of(j * tk, tk)
        s_ref[...] = _dot(k_ref[0, pl.ds(k0, tk), :], qbd_ref[...])

    def absorb(j, s_ref):
        k0 = pl.multiple_of(j * tk, tk)
        vb = v_ref[0, pl.ds(k0, tk), :]
        s = s_ref[...]
        m_prev = m_ref[...]
        m_new = jnp.maximum(m_prev, jnp.max(s, axis=0, keepdims=True))
        alpha = jnp.exp2(m_prev - m_new)
        p = jnp.exp2(s - m_new)
        l_ref[...] = alpha * l_ref[...] + jnp.sum(p, axis=0, keepdims=True)
        m_ref[...] = m_new
        pb = p.astype(BF16)
        for c in range(2):
            cols = slice(c * tq, (c + 1) * tq)
            acc_ref[c] = alpha[:, cols] * acc_ref[c] + _dot_tn(vb, pb[:, cols])

    bufs = (s0_ref, s1_ref)
    n_trips = k_ref.shape[1] // (blocks_per_trip * tk)
    scores(0, s0_ref)

    def trip(i, carry):
        j = blocks_per_trip * i
        for u in range(blocks_per_trip):
            scores(j + u + 1, bufs[(u + 1) % 2])
            absorb(j + u, bufs[u % 2])
        return carry

    lax.fori_loop(0, n_trips - 1, trip, 0)
    j_last = blocks_per_trip * (n_trips - 1)
    for u in range(blocks_per_trip):
        if u + 1 < blocks_per_trip:
            scores(j_last + u + 1, bufs[(u + 1) % 2])
        absorb(j_last + u, bufs[u % 2])

    inv = 1.0 / l_ref[...]
    o = acc_ref[0] * inv[:, 0:tq] - lam_ref[0] * (acc_ref[1] * inv[:, tq:2 * tq])
    ms = jnp.mean(o * o, axis=0, keepdims=True)
    o = o * lax.rsqrt(ms + EPS) * gsub_ref[...] * out_scale
    o_ref[0] = o.T.astype(o_ref.dtype)


def _diff_attn(lam, q, k, v, g_sub_col, lam_init, *, tq, tk, blocks_per_trip):
    B, S, _ = q.shape
    assert blocks_per_trip % 2 == 0 and S % (blocks_per_trip * tk) == 0, (S, tk, blocks_per_trip)
    est = (2 * 2 * S * (HEAD_COLS + DIFF_VDIM) * 2 + 4 * tq * HEAD_COLS * 2 + 2 * tq * HEAD_COLS * 2
           + 2 * tq * DIFF_VDIM * 4 + 8 * tk * 2 * tq * 4 + (4 << 20))
    return pl.pallas_call(
        functools.partial(_diff_attn_kernel, tq=tq, tk=tk, blocks_per_trip=blocks_per_trip,
                          out_scale=1.0 - lam_init),
        grid=(B, N_DIFF_HEADS, S // tq),
        in_specs=[pl.BlockSpec(memory_space=pltpu.SMEM),
                  pl.BlockSpec((1, tq, HEAD_COLS), lambda b, h, i: (b, i, h)),
                  pl.BlockSpec((1, S, HEAD_COLS), lambda b, h, i: (b, 0, h)),
                  pl.BlockSpec((1, S, DIFF_VDIM), lambda b, h, i: (b, 0, h)),
                  pl.BlockSpec((DIFF_VDIM, 1), lambda b, h, i: (0, 0))],
        out_specs=pl.BlockSpec((1, tq, DIFF_VDIM), lambda b, h, i: (b, i, h)),
        out_shape=jax.ShapeDtypeStruct((B, S, N_DIFF_HEADS * DIFF_VDIM), BF16),
        scratch_shapes=[pltpu.VMEM((HEAD_COLS, 2 * tq), BF16),
                        pltpu.VMEM((tk, 2 * tq), F32),
                        pltpu.VMEM((tk, 2 * tq), F32),
                        pltpu.VMEM((1, 2 * tq), F32),
                        pltpu.VMEM((1, 2 * tq), F32),
                        pltpu.VMEM((2, DIFF_VDIM, tq), F32)],
        compiler_params=_params(("parallel", "parallel", "parallel"), est),
        name="diff_attn",
    )(lam, q, k, v, g_sub_col)


def _diff_attn_bounded_kernel(lam_ref, q_ref, k_ref, v_ref, gsub_ref, o_ref, qbd_ref, l_ref, acc_ref,
                              *, tq, tk, blocks_per_trip, out_scale):
    row = lax.broadcasted_iota(jnp.int32, (HEAD_COLS, 1), 0)
    for hh in range(q_ref.shape[2] // HEAD_COLS):
        qk_cols = slice(hh * HEAD_COLS, (hh + 1) * HEAD_COLS)
        v_cols = slice(hh * DIFF_VDIM, (hh + 1) * DIFF_VDIM)
        qt = q_ref[0, :, qk_cols].astype(F32).T
        zero = jnp.zeros_like(qt)
        qbd_ref[hh, :, 0:tq] = jnp.where(row < DIFF_QKDIM, qt, zero).astype(BF16)
        qbd_ref[hh, :, tq:2 * tq] = jnp.where(row >= DIFF_QKDIM, qt, zero).astype(BF16)
        l_ref[hh] = jnp.zeros(l_ref.shape[1:], F32)
        acc_ref[hh] = jnp.zeros(acc_ref.shape[1:], F32)

        def trip(i, carry):
            def scores(u):
                k0 = pl.multiple_of((blocks_per_trip * i + u) * tk, tk)
                return _dot(k_ref[0, pl.ds(k0, tk), qk_cols], qbd_ref[hh])

            s = scores(0)
            for u in range(blocks_per_trip):
                s_next = scores(u + 1) if u + 1 < blocks_per_trip else None
                k0 = pl.multiple_of((blocks_per_trip * i + u) * tk, tk)
                vb = v_ref[0, pl.ds(k0, tk), v_cols]
                p = jnp.exp2(s)
                l_ref[hh] += jnp.sum(p, axis=0, keepdims=True)
                pb = p.astype(BF16)
                for c in range(2):
                    acc_ref[hh, c] += _dot_tn(vb, pb[:, c * tq:(c + 1) * tq])
                s = s_next
            return carry

        lax.fori_loop(0, k_ref.shape[1] // (blocks_per_trip * tk), trip, 0)

        inv = 1.0 / l_ref[hh]
        o = acc_ref[hh, 0] * inv[:, 0:tq] - lam_ref[0] * (acc_ref[hh, 1] * inv[:, tq:2 * tq])
        ms = jnp.mean(o * o, axis=0, keepdims=True)
        o = o * lax.rsqrt(ms + EPS) * gsub_ref[...] * out_scale
        o_ref[0, :, v_cols] = o.T.astype(o_ref.dtype)


def _diff_attn_bounded(lam, q, k, v, g_sub_col, lam_init, *, tq, tk, blocks_per_trip, heads_per_step):
    B, S, _ = q.shape
    assert S % (blocks_per_trip * tk) == 0, (S, tk, blocks_per_trip)
    assert N_DIFF_HEADS % heads_per_step == 0
    hps = heads_per_step
    est = (hps * (2 * 2 * S * (HEAD_COLS + DIFF_VDIM) * 2 + 4 * tq * HEAD_COLS * 2 + 2 * tq * HEAD_COLS * 2
                  + 2 * tq * DIFF_VDIM * 4) + 8 * tk * 2 * tq * 4 + (4 << 20))
    return pl.pallas_call(
        functools.partial(_diff_attn_bounded_kernel, tq=tq, tk=tk, blocks_per_trip=blocks_per_trip,
                          out_scale=1.0 - lam_init),
        grid=(B, N_DIFF_HEADS // hps, S // tq),
        in_specs=[pl.BlockSpec(memory_space=pltpu.SMEM),
                  pl.BlockSpec((1, tq, hps * HEAD_COLS), lambda b, h, i: (b, i, h)),
                  pl.BlockSpec((1, S, hps * HEAD_COLS), lambda b, h, i: (b, 0, h)),
                  pl.BlockSpec((1, S, hps * DIFF_VDIM), lambda b, h, i: (b, 0, h)),
                  pl.BlockSpec((DIFF_VDIM, 1), lambda b, h, i: (0, 0))],
        out_specs=pl.BlockSpec((1, tq, hps * DIFF_VDIM), lambda b, h, i: (b, i, h)),
        out_shape=jax.ShapeDtypeStruct((B, S, N_DIFF_HEADS * DIFF_VDIM), BF16),
        scratch_shapes=[pltpu.VMEM((hps, HEAD_COLS, 2 * tq), BF16),
                        pltpu.VMEM((hps, 1, 2 * tq), F32),
                        pltpu.VMEM((hps, 2, DIFF_VDIM, tq), F32)],
        compiler_params=_params(("parallel", "parallel", "parallel"), est),
        name="diff_attn_bounded",
    )(lam, q, k, v, g_sub_col)


def _pool_fold_kernel(wp_ref, ps_ref, wo_ref, o_ref):
    w = (wp_ref[0].astype(F32) * ps_ref[...]).astype(BF16)
    o_ref[...] = _dot(w, wo_ref[...]).astype(o_ref.dtype)


def _pool_fold(w_pool, pool_scale, w_out):
    groups, gw, _ = w_pool.shape
    D = w_out.shape[1]
    a_width = w_out.shape[0] - groups * gw
    folded = pl.pallas_call(
        _pool_fold_kernel,
        grid=(groups,),
        in_specs=[pl.BlockSpec((1, gw, gw), lambda g: (g, 0, 0)),
                  pl.BlockSpec((1, gw), lambda g: (0, g)),
                  pl.BlockSpec((gw, D), lambda g: (a_width // gw + g, 0))],
        out_specs=pl.BlockSpec((gw, D), lambda g: (g, 0)),
        out_shape=jax.ShapeDtypeStruct((groups * gw, D), BF16),
        compiler_params=_params(("parallel",), 8 * gw * D * 4 + (4 << 20)),
        name="pool_fold",
    )(w_pool, pool_scale, w_out)
    return jnp.concatenate([w_out[:a_width], folded], axis=0)


def _mix_out_kernel(a_ref, u_ref, up_ref, un_ref, x_ref, wo_ref, o_ref, *, tm, seq):
    i = pl.program_id(1)
    u_prev = jnp.where(i > 0, up_ref[0], 0.0)
    u_next = jnp.where(i < pl.num_programs(1) - 1, un_ref[0], 0.0)
    ue = jnp.concatenate([u_prev, u_ref[0], u_next], axis=0)
    rows = tm + 2 * POOL_HALO
    pos = i * tm + lax.broadcasted_iota(jnp.int32, (tm, 1), 0)
    a_width = a_ref.shape[2]
    acc = x_ref[0] + _dot(a_ref[0], wo_ref[0:a_width, :])
    pooled = []
    for g, w in enumerate(POOL_WINDOWS):
        c0 = g * POOL_GROUP_WIDTH
        ug = ue[:, c0:c0 + POOL_GROUP_WIDTH]
        win = ug + pltpu.roll(ug, 1, 0)
        shift = 1
        while 2 * shift < w:
            win = pltpu.roll(win, shift, 0) + pltpu.roll(win, rows - shift, 0)
            shift *= 2
        win = win[POOL_HALO:POOL_HALO + tm]
        cnt = jnp.minimum(pos + w // 2, seq) - jnp.maximum(pos - w // 2, 0)
        z = win / cnt.astype(F32) - ug[POOL_HALO:POOL_HALO + tm]
        pooled.append(z.astype(BF16))
    o_ref[0] = acc + _dot(jnp.concatenate(pooled, axis=1), wo_ref[a_width:, :])


def _mix_out(a, u, x, w_out, *, tm):
    B, S, D = x.shape
    a_width, u_width = a.shape[2], u.shape[2]
    halo_blocks = tm // POOL_HALO
    n_halo = S // POOL_HALO
    row = lambda b, i: (b, i, 0)
    est = (w_out.size * 2 + 4 * tm * D * 4 + 2 * tm * a_width * 2 + 2 * tm * u_width * 4
           + 6 * tm * u_width * 4 + 2 * tm * D * 4 + (4 << 20))
    return pl.pallas_call(
        functools.partial(_mix_out_kernel, tm=tm, seq=S),
        grid=(B, S // tm),
        in_specs=[pl.BlockSpec((1, tm, a_width), row),
                  pl.BlockSpec((1, tm, u_width), row),
                  pl.BlockSpec((1, POOL_HALO, u_width),
                               lambda b, i: (b, jnp.maximum(i * halo_blocks - 1, 0), 0)),
                  pl.BlockSpec((1, POOL_HALO, u_width),
                               lambda b, i: (b, jnp.minimum((i + 1) * halo_blocks, n_halo - 1), 0)),
                  pl.BlockSpec((1, tm, D), row),
                  _resident(w_out.shape)],
        out_specs=pl.BlockSpec((1, tm, D), row),
        out_shape=jax.ShapeDtypeStruct((B, S, D), F32),
        compiler_params=_params(("parallel", "parallel"), est),
        name="mix_out",
    )(a, u, u, u, x, w_out)


def _mem_k_kernel(mem_ref, g_ref, w_ref, gk_ref, o_ref):
    m = _rms(mem_ref[0], g_ref[...]).astype(BF16)
    k = _dot(m, w_ref[...])
    o_ref[0] = _rms(k, gk_ref[...]).astype(o_ref.dtype)


def _mem_k(mem, g_mem, wc_kv, gc_k):
    B, n_mem, D = mem.shape
    hd = D // N_CROSS_HEADS
    est = 2 * n_mem * D * 4 + 2 * D * hd * 2 + 4 * n_mem * hd * 4 + n_mem * D * 4 + (4 << 20)
    return pl.pallas_call(
        _mem_k_kernel,
        grid=(N_CROSS_HEADS, B),
        in_specs=[pl.BlockSpec((1, n_mem, D), lambda h, b: (b, 0, 0)),
                  pl.BlockSpec((1, D), lambda h, b: (0, 0)),
                  pl.BlockSpec((D, hd), lambda h, b: (0, h)),
                  pl.BlockSpec((1, hd), lambda h, b: (0, 0))],
        out_specs=pl.BlockSpec((1, n_mem, hd), lambda h, b: (b, 0, h)),
        out_shape=jax.ShapeDtypeStruct((B, n_mem, D), BF16),
        compiler_params=_params(("parallel", "parallel"), est),
        name="mem_k",
    )(mem, g_mem, wc_kv, gc_k)


def _mem_vo_kernel(mem_ref, g_ref, w_ref, wo_ref, o_ref):
    m = _rms(mem_ref[0], g_ref[...]).astype(BF16)
    v = _dot(m, w_ref[...]).astype(BF16)
    o_ref[0] = _dot(v, wo_ref[...]).astype(o_ref.dtype)


def _mem_vo(mem, g_mem, wc_kv, wc_o):
    B, n_mem, D = mem.shape
    hd = D // N_CROSS_HEADS
    est = (2 * n_mem * D * 4 + 2 * D * hd * 2 + 2 * hd * D * 2 + 2 * n_mem * D * 2 + 4 * n_mem * D * 4 + (4 << 20))
    return pl.pallas_call(
        _mem_vo_kernel,
        grid=(N_CROSS_HEADS, B),
        in_specs=[pl.BlockSpec((1, n_mem, D), lambda h, b: (b, 0, 0)),
                  pl.BlockSpec((1, D), lambda h, b: (0, 0)),
                  pl.BlockSpec((D, hd), lambda h, b: (0, N_CROSS_HEADS + h)),
                  pl.BlockSpec((hd, D), lambda h, b: (h, 0))],
        out_specs=pl.BlockSpec((1, n_mem, D), lambda h, b: (b, h, 0)),
        out_shape=jax.ShapeDtypeStruct((B, N_CROSS_HEADS * n_mem, D), BF16),
        compiler_params=_params(("parallel", "parallel"), est),
        name="mem_vo",
    )(mem, g_mem, wc_kv, wc_o)


def _cross_attn_kernel(x_ref, g_ref, wq_ref, gq_ref, k_ref, vo_ref, gn_ref, o_ref, hn_ref):
    x = x_ref[0]
    h = _rms(x, g_ref[...]).astype(BF16)
    hd = gq_ref.shape[1]
    scale = hd ** -0.5
    heads = [slice(c0, c0 + hd) for c0 in range(0, x.shape[1], hd)]
    qs = [_dot(h, wq_ref[:, c]) for c in heads]
    qn = [_rms(q, gq_ref[...]).astype(BF16) for q in qs]
    ss = [_dot_nt(q, k_ref[0, :, c]) * scale for q, c in zip(qn, heads)]
    ps = []
    for s in ss:
        p = jnp.exp(s - jnp.max(s, axis=-1, keepdims=True))
        ps.append((p / jnp.sum(p, axis=-1, keepdims=True)).astype(BF16))
    n_mem = k_ref.shape[1]
    out = x
    for i, p in enumerate(ps):
        out = out + _dot(p, vo_ref[0, i * n_mem:(i + 1) * n_mem, :])
    o_ref[0] = out
    hn_ref[0] = _rms(out, gn_ref[...]).astype(hn_ref.dtype)


def _cross_attn(x, g_cross, wc_q, gc_q, k_mem, vo_mem, g_next, *, tm):
    B, S, D = x.shape
    n_mem = k_mem.shape[1]
    row = lambda b, i: (b, i, 0)
    est = (wc_q.size * 2 + 2 * n_mem * D * 2 + 2 * vo_mem.shape[1] * D * 2 + 4 * tm * D * 4 + 5 * tm * D * 4
           + (4 << 20))
    return pl.pallas_call(
        _cross_attn_kernel,
        grid=(B, S // tm),
        in_specs=[pl.BlockSpec((1, tm, D), row),
                  pl.BlockSpec((1, D), lambda b, i: (0, 0)),
                  _resident(wc_q.shape),
                  pl.BlockSpec(gc_q.shape, lambda b, i: (0, 0)),
                  pl.BlockSpec((1, n_mem, D), lambda b, i: (b, 0, 0)),
                  pl.BlockSpec((1, vo_mem.shape[1], D), lambda b, i: (b, 0, 0)),
                  pl.BlockSpec((1, D), lambda b, i: (0, 0))],
        out_specs=[pl.BlockSpec((1, tm, D), row), pl.BlockSpec((1, tm, D), row)],
        out_shape=[jax.ShapeDtypeStruct((B, S, D), F32), jax.ShapeDtypeStruct((B, S, D), BF16)],
        compiler_params=_params(("parallel", "parallel"), est),
        name="cross_attn",
    )(x, g_cross, wc_q, gc_q, k_mem, vo_mem, g_next)


def _conv_ffn_kernel(hin_ref, xr_ref, xp_ref, xn_ref, g_ref, wg_ref, wv_ref, cwg_ref, cwv_ref, cbg_ref, cbv_ref,
                     wd_ref, o_ref, h_ref, *, tm, res_chunks):
    i = pl.program_id(1)
    f = pl.program_id(2)
    rows = tm + 2 * CONV_HALO
    slab = tm // res_chunks

    @pl.when(f == 0)
    def _():
        g = g_ref[...]
        h_prev = jnp.where(i > 0, _rms(xp_ref[0], g), 0.0)
        h_next = jnp.where(i < pl.num_programs(1) - 1, _rms(xn_ref[0], g), 0.0)
        h_ref[0:CONV_HALO, :] = h_prev.astype(BF16)
        h_ref[CONV_HALO:CONV_HALO + tm, :] = hin_ref[0]
        h_ref[CONV_HALO + tm:rows, :] = h_next.astype(BF16)
        o_ref[0] = jnp.zeros(o_ref.shape[1:], o_ref.dtype)

    h = h_ref[...]

    def conv(w_ref, cw_ref, cb_ref):
        u = _dot(h, w_ref[...])
        below = pltpu.roll(u, 1, 0)[CONV_HALO:CONV_HALO + tm]
        above = pltpu.roll(u, rows - 1, 0)[CONV_HALO:CONV_HALO + tm]
        mid = u[CONV_HALO:CONV_HALO + tm]
        return below * cw_ref[0:1, :] + mid * cw_ref[1:2, :] + above * cw_ref[2:3, :] + cb_ref[...]

    act = jax.nn.gelu(conv(wg_ref, cwg_ref, cbg_ref)) * conv(wv_ref, cwv_ref, cbv_ref)
    o_ref[0] += _dot(act.astype(BF16), wd_ref[...])

    @pl.when(f < res_chunks)
    def _():
        r0 = pl.multiple_of(f * slab, slab)
        o_ref[0, pl.ds(r0, slab), :] += xr_ref[0]


def _conv_ffn(x, h, g_ffn, w_up, conv_w, conv_b, w_down, *, tm, tf):
    B, S, D = x.shape
    d_ff = w_down.shape[0]
    nf = d_ff // tf
    halo_blocks = tm // CONV_HALO
    n_halo = S // CONV_HALO
    rows = tm + 2 * CONV_HALO
    res_chunks = min(nf, 8)
    assert tm % (res_chunks * CONV_HALO) == 0, (tm, res_chunks)
    est = (2 * tm * D * 4 + 2 * tm * D * 2 + 2 * 3 * D * tf * 2 + rows * D * 2 + 8 * rows * tf * 4 + 2 * tm * D * 4
           + (4 << 20))
    gate = lambda b, i, f: (0, f)
    val = lambda b, i, f: (0, nf + f)
    return pl.pallas_call(
        functools.partial(_conv_ffn_kernel, tm=tm, res_chunks=res_chunks),
        grid=(B, S // tm, nf),
        in_specs=[pl.BlockSpec((1, tm, D), lambda b, i, f: (b, i, 0)),
                  pl.BlockSpec((1, tm // res_chunks, D),
                               lambda b, i, f: (b, i * res_chunks + jnp.minimum(f, res_chunks - 1), 0)),
                  pl.BlockSpec((1, CONV_HALO, D), lambda b, i, f: (b, jnp.maximum(i * halo_blocks - 1, 0), 0)),
                  pl.BlockSpec((1, CONV_HALO, D),
                               lambda b, i, f: (b, jnp.minimum((i + 1) * halo_blocks, n_halo - 1), 0)),
                  pl.BlockSpec((1, D), lambda b, i, f: (0, 0)),
                  pl.BlockSpec((D, tf), gate),
                  pl.BlockSpec((D, tf), val),
                  pl.BlockSpec((conv_w.shape[0], tf), gate),
                  pl.BlockSpec((conv_w.shape[0], tf), val),
                  pl.BlockSpec((1, tf), gate),
                  pl.BlockSpec((1, tf), val),
                  pl.BlockSpec((tf, D), lambda b, i, f: (f, 0))],
        out_specs=pl.BlockSpec((1, tm, D), lambda b, i, f: (b, i, 0)),
        out_shape=jax.ShapeDtypeStruct((B, S, D), F32),
        scratch_shapes=[pltpu.VMEM((rows, D), BF16)],
        compiler_params=_params(("parallel", "parallel", "arbitrary"), est),
        name="conv_ffn",
    )(h, x, x, x, g_ffn, w_up, w_up, conv_w, conv_w, conv_b, conv_b, w_down)


def _rope_tables(seq, gain, scale):
    half = DIFF_QKDIM // 2
    inv = ROPE_THETA ** (-jnp.arange(half, dtype=F32) / half)
    ang = jnp.arange(seq, dtype=F32)[:, None] * inv[None, :]
    cos = jnp.cos(ang)
    sin = jnp.sin(ang)
    gain = gain.astype(F32)
    a = jnp.concatenate([gain[:half] * cos, gain[half:] * cos], axis=1) * scale
    b = jnp.concatenate([-gain[half:] * sin, gain[:half] * sin], axis=1) * scale
    reps = V7X_LANES // DIFF_QKDIM
    return jnp.tile(a, (1, reps)), jnp.tile(b, (1, reps))


def _segment_ones(width):
    seg = jnp.arange(width) // DIFF_QKDIM
    return (seg[:, None] == seg[None, :]).astype(BF16)


def _tile(n, target):
    t = min(n, target)
    assert n % t == 0, (n, t)
    return t


def _run_trunk(x, mem, layers):
    B, S, D = x.shape
    tm = _tile(S, TOKEN_TILE)
    tq = _tile(S, ATTN_Q_TILE)
    flash_tk = _tile(S // FLASH_BLOCKS_PER_TRIP, ATTN_K_TILE)
    tk = _tile(S // 8, ATTN_K_TILE)
    seg = _segment_ones(V7X_MXU_WIDTH)
    for l, p in enumerate(layers):
        lam_init = 0.8 - 0.6 * math.exp(-0.3 * l)
        q_scale = DIFF_QKDIM ** -0.5 * math.log2(math.e)
        aq, bq = _rope_tables(S, p["g_q"], q_scale)
        ak, bk = _rope_tables(S, p["g_k"], 1.0)
        q, k, v, u = _mix_in(x, p["g_mix"], p["w_in"], seg, aq, bq, ak, bk, tm=tm)
        bound = (DIFF_QKDIM * q_scale * 1.01 * jnp.max(jnp.abs(p["g_q"])) * jnp.max(jnp.abs(p["g_k"]))).astype(F32)
        attn = functools.partial(_diff_attn, p["lam"], q, k, v, p["g_sub"].reshape(-1, 1), lam_init,
                                 tq=tq, tk=flash_tk, blocks_per_trip=FLASH_BLOCKS_PER_TRIP)
        attn_bounded = functools.partial(_diff_attn_bounded, p["lam"], q, k, v, p["g_sub"].reshape(-1, 1), lam_init,
                                         tq=tq, tk=tk, blocks_per_trip=min(ATTN_BLOCKS_PER_TRIP, S // tk),
                                         heads_per_step=ATTN_HEADS_PER_STEP)
        a = lax.cond(bound <= UNSHIFTED_SOFTMAX_SCORE_LIMIT, attn_bounded, attn)
        x = _mix_out(a, u, x, p["w_out_folded"], tm=tm)
        k_mem = _mem_k(mem, p["g_mem"], p["wc_kv"], p["gc_k"])
        vo_mem = _mem_vo(mem, p["g_mem"], p["wc_kv"], p["wc_o"])
        x, h_ffn = _cross_attn(x, p["g_cross"], p["wc_q"], p["gc_q"], k_mem, vo_mem, p["g_ffn"], tm=tm)
        x = _conv_ffn(x, h_ffn, p["g_ffn"], p["w_up"], p["conv_w"], p["conv_b"], p["w_down"],
                      tm=_tile(S, FFN_TOKEN_TILE), tf=FFN_CHUNK)
    return x


def kernel(x_prompt, x_sample, mem_prompt, mem_sample, g_mix, w_in, g_q, g_k, lam_q1, lam_k1, lam_q2, lam_k2,
           g_sub, w_pool, pool_scale, w_out, g_cross, g_mem, wc_q, wc_kv, gc_q, gc_k, wc_o, g_ffn, w_up,
           conv_w, conv_b, w_down):
    depth = w_in.shape[0]
    layers = []
    for l in range(depth):
        lam_init = 0.8 - 0.6 * math.exp(-0.3 * l)
        lam = (jnp.exp(jnp.sum(lam_q1[l].astype(F32) * lam_k1[l].astype(F32)))
               - jnp.exp(jnp.sum(lam_q2[l].astype(F32) * lam_k2[l].astype(F32))) + lam_init)
        row = lambda t: t[l].reshape(1, -1).astype(F32)
        layers.append(dict(
            lam=lam.reshape(1).astype(F32),
            g_mix=row(g_mix), g_q=g_q[l], g_k=g_k[l], g_sub=row(g_sub),
            g_cross=row(g_cross), g_mem=row(g_mem), gc_q=row(gc_q), gc_k=row(gc_k), g_ffn=row(g_ffn),
            conv_w=conv_w[l].astype(F32), conv_b=row(conv_b),
            w_in=w_in[l].astype(BF16),
            w_out_folded=_pool_fold(w_pool[l].astype(BF16), row(pool_scale), w_out[l].astype(BF16)),
            wc_q=wc_q[l].astype(BF16), wc_kv=wc_kv[l].astype(BF16), wc_o=wc_o[l].astype(BF16),
            w_up=w_up[l].astype(BF16), w_down=w_down[l].astype(BF16)))
    return (_run_trunk(x_prompt, mem_prompt, layers), _run_trunk(x_sample, mem_sample, layers))
```
